```python
import math
import jax
import jax.numpy as jnp
from jax import lax
import numpy as np

D_MODEL = 1024
BATCH = 16
SEQ = 4096
DEPTH = 1

CTX_LEN = 256
GRID_W = 64
D_MIX = D_MODEL
D_DN = D_MIX // 2
D_ML = D_MIX - D_DN
DN_HEADS = 4
DN_HEAD_DIM = D_DN // DN_HEADS
ML_HEADS = 4
ML_HEAD_DIM = D_ML // ML_HEADS
CONV_K = 5
CHUNK = 64
D_FF = ((8 * D_MODEL // 3 + 255) // 256) * 256
N_MOD = 6
NORM_EPS = 1e-6
DN_SPLITS = (3 * D_DN, 4 * D_DN, 4 * D_DN + 2 * DN_HEADS)
ML_SPLITS = (3 * D_ML, 4 * D_ML, 4 * D_ML + 2 * ML_HEADS)
DN_COLS = 4 * D_DN + 4 * DN_HEADS
ML_COLS = 4 * D_ML + 4 * ML_HEADS
D_IN = DN_COLS + ML_COLS

kernel_name = 'hybrid_gdn_mlstm_diffusion_block'


def rmsnorm(x, gain):
    xf = x.astype(jnp.float32)
    y = xf * lax.rsqrt(jnp.mean(xf * xf, axis=-1, keepdims=True) + NORM_EPS)
    return (y * gain.astype(jnp.float32)).astype(x.dtype)


def l2norm(x):
    return x * lax.rsqrt(jnp.sum(x * x, axis=-1, keepdims=True) + NORM_EPS)


def to_heads(t, n_heads):
    b, s, _ = t.shape
    return t.reshape(b, s, n_heads, -1).transpose(0, 2, 1, 3)


def direction_split(t):
    b, s, _ = t.shape
    return t.reshape(b, s, 2, -1).transpose(2, 0, 3, 1)


def to_col_major(t):
    b, s = t.shape[:2]
    rows = s // GRID_W
    return t.reshape(b, rows, GRID_W, *t.shape[2:]).swapaxes(1, 2).reshape(t.shape)


def to_row_major(t):
    b, s = t.shape[:2]
    rows = s // GRID_W
    return t.reshape(b, GRID_W, rows, *t.shape[2:]).swapaxes(1, 2).reshape(t.shape)


def to_chunks(a):
    b, h, s = a.shape[:3]
    return jnp.moveaxis(a.reshape(b, h, s // CHUNK, CHUNK, *a.shape[3:]), 2, 0)


def from_chunks(a):
    a = jnp.moveaxis(a, 0, 2)
    return a.reshape(a.shape[0], a.shape[1], -1, *a.shape[4:])


def flip_t(a):
    return jnp.flip(a, axis=2)


def short_conv(u, w):
    out = lax.conv_general_dilated(u, w[:, None, :].astype(u.dtype), window_strides=(1,), padding='SAME',
                                   dimension_numbers=('NWC', 'WIO', 'NWC'), feature_group_count=u.shape[-1])
    return jax.nn.silu(out)


def gated_delta_chunked(q, k, v, g, beta, s0, with_output):
    q, k, v, g, beta = map(to_chunks, (q, k, v, g, beta))
    causal = jnp.tril(jnp.ones((CHUNK, CHUNK), bool))
    strict = jnp.tril(jnp.ones((CHUNK, CHUNK), bool), -1)
    G = jnp.cumsum(g, axis=-1)
    decay = jnp.exp(jnp.where(causal, G[..., :, None] - G[..., None, :], -jnp.inf))
    kbeta = k * beta[..., None]
    A = jnp.where(strict, jnp.einsum('nbhik,nbhjk->nbhij', kbeta, k) * decay, 0.0)
    eye_plus_a = A + jnp.eye(CHUNK, dtype=A.dtype)
    W = lax.linalg.triangular_solve(eye_plus_a, kbeta * jnp.exp(G)[..., None], left_side=True, lower=True, unit_diagonal=True)
    U = lax.linalg.triangular_solve(eye_plus_a, v * beta[..., None], left_side=True, lower=True, unit_diagonal=True)

    def body(S, inp):
        qc, kc, wc, uc, Gc, dc = inp
        v_new = uc - jnp.einsum('bhck,bhkv->bhcv', wc, S)
        g_last = Gc[..., -1]
        S_next = S * jnp.exp(g_last)[..., None, None] + jnp.einsum(
            'bhck,bhcv->bhkv', kc * jnp.exp(g_last[..., None] - Gc)[..., None], v_new)
        if with_output:
            scores = jnp.einsum('bhik,bhjk->bhij', qc, kc) * dc
            o = (jnp.einsum('bhck,bhkv->bhcv', qc * jnp.exp(Gc)[..., None], S)
                 + jnp.einsum('bhij,bhjv->bhiv', scores, v_new))
            return S_next, o
        return S_next, None

    s_fin, o = lax.scan(body, s0, (q, k, W, U, G, decay))
    return (from_chunks(o) if with_output else None), s_fin


def mlstm_chunked(q, k, v, ig, lf, state, with_output):
    q, k, v, ig, lf = map(to_chunks, (q, k, v, ig, lf))
    causal = jnp.tril(jnp.ones((CHUNK, CHUNK), bool))

    def body(carry, inp):
        C, n, m = carry
        qc, kc, vc, ic, fc = inp
        b = jnp.cumsum(fc, axis=-1)
        b_last = b[..., -1]
        e = b_last[..., None] - b + ic
        m_next = jnp.maximum(b_last + m, jnp.max(e, axis=-1))
        w_tok = jnp.exp(e - m_next[..., None])
        carry_scale = jnp.exp(b_last + m - m_next)
        C_next = carry_scale[..., None, None] * C + jnp.einsum('bhck,bhcv->bhkv', kc * w_tok[..., None], vc)
        n_next = carry_scale[..., None] * n + jnp.einsum('bhck,bhc->bhk', kc, w_tok)
        if with_output:
            log_d = jnp.where(causal, b[..., :, None] - b[..., None, :] + ic[..., None, :], -jnp.inf)
            inter = b + m[..., None]
            m_i = jnp.maximum(inter, jnp.max(log_d, axis=-1))
            s = jnp.einsum('bhik,bhjk->bhij', qc, kc) * jnp.exp(log_d - m_i[..., None])
            w_inter = jnp.exp(inter - m_i)
            num = w_inter[..., None] * jnp.einsum('bhik,bhkv->bhiv', qc, C) + jnp.einsum('bhij,bhjv->bhiv', s, vc)
            den = w_inter * jnp.einsum('bhik,bhk->bhi', qc, n) + jnp.sum(s, axis=-1)
            h = num / jnp.maximum(jnp.abs(den), jnp.exp(-m_i))[..., None]
            return (C_next, n_next, m_next), h
        return (C_next, n_next, m_next), None

    state_fin, h = lax.scan(body, state, (q, k, v, ig, lf))
    return (from_chunks(h) if with_output else None), state_fin


def run_bidirectional(chunk_fn, shared, per_dir, init_f, init_b, with_output):
    o_f, s_f = chunk_fn(*shared, *(a[0] for a in per_dir), init_f, with_output)
    o_b, s_b = chunk_fn(*(flip_t(a) for a in shared), *(flip_t(a[1]) for a in per_dir), init_b, with_output)
    out = o_f + flip_t(o_b) if with_output else None
    return out, s_f, s_b


def deltanet_group(p_ctx, p_lat, conv_w, a_log, dt_bias, norm_w, need_ctx_out):
    conv_w = conv_w.astype(jnp.float32)
    a = jnp.exp(a_log.astype(jnp.float32))[:, None, :, None]
    dtb = dt_bias.astype(jnp.float32)[:, None, :, None]

    def prep(p):
        qkv, gate, beta_raw, alpha_raw = jnp.split(p, DN_SPLITS, axis=-1)
        q, k, v = [to_heads(t, DN_HEADS) for t in jnp.split(short_conv(qkv, conv_w), 3, axis=-1)]
        q = l2norm(q) * DN_HEAD_DIM ** -0.5
        k = l2norm(k)
        beta = jax.nn.sigmoid(direction_split(beta_raw))
        g = -a * jax.nn.softplus(direction_split(alpha_raw) + dtb)
        return (q, k, v), (g, beta), gate

    def finalize(o, gate):
        b, _, s, _ = o.shape
        o = o.transpose(0, 2, 1, 3)
        y = rmsnorm(o, norm_w) * jax.nn.silu(gate.reshape(b, s, DN_HEADS, DN_HEAD_DIM))
        return y.reshape(b, s, D_DN)

    s0 = jnp.zeros((p_ctx.shape[0], DN_HEADS, DN_HEAD_DIM, DN_HEAD_DIM), jnp.float32)
    sh_c, pd_c, gate_c = prep(p_ctx)
    o_ctx, s_f, s_b = run_bidirectional(gated_delta_chunked, sh_c, pd_c, s0, s0, need_ctx_out)
    sh_l, pd_l, gate_l = prep(p_lat)
    o_lat, _, _ = run_bidirectional(gated_delta_chunked, sh_l, pd_l, s_f, s_b, True)
    y_ctx = finalize(o_ctx, gate_c) if need_ctx_out else None
    return finalize(o_lat, gate_l), y_ctx


def mlstm_group(p_ctx, p_lat, ig_bias, fg_bias, norm_w, need_ctx_out):
    igb = ig_bias.astype(jnp.float32)[:, None, :, None]
    fgb = fg_bias.astype(jnp.float32)[:, None, :, None]

    def prep(p):
        qkv, ogate, ig_raw, fg_raw = jnp.split(p, ML_SPLITS, axis=-1)
        q, k, v = [to_heads(t, ML_HEADS) for t in jnp.split(qkv, 3, axis=-1)]
        k = k * ML_HEAD_DIM ** -0.5
        ig = direction_split(ig_raw) + igb
        lf = jax.nn.log_sigmoid(direction_split(fg_raw) + fgb)
        return (q, k, v), (ig, lf), ogate

    def finalize(h, ogate):
        b, _, s, _ = h.shape
        h = h.transpose(0, 2, 1, 3)
        y = rmsnorm(h, norm_w.reshape(ML_HEADS, ML_HEAD_DIM)).reshape(b, s, D_ML)
        return y * jax.nn.sigmoid(ogate)

    bsz = p_ctx.shape[0]
    st0 = (jnp.zeros((bsz, ML_HEADS, ML_HEAD_DIM, ML_HEAD_DIM), jnp.float32),
           jnp.zeros((bsz, ML_HEADS, ML_HEAD_DIM), jnp.float32),
           jnp.zeros((bsz, ML_HEADS), jnp.float32))
    sh_c, pd_c, og_c = prep(p_ctx)
    h_ctx, st_f, st_b = run_bidirectional(mlstm_chunked, sh_c, pd_c, st0, st0, need_ctx_out)
    sh_l, pd_l, og_l = prep(to_col_major(p_lat))
    h_lat, _, _ = run_bidirectional(mlstm_chunked, sh_l, pd_l, st_f, st_b, True)
    y_ctx = finalize(h_ctx, og_c) if need_ctx_out else None
    return to_row_major(finalize(h_lat, og_l)), y_ctx


def parallel_mixers(h_ctx, h_lat, w_in, dn_conv, dn_a_log, dn_dt_bias, dn_norm,
                    ml_ig_bias, ml_fg_bias, ml_norm, need_ctx_out):
    p_ctx = (h_ctx @ w_in).astype(jnp.float32)
    p_lat = (h_lat @ w_in).astype(jnp.float32)
    dn_lat, dn_ctx = deltanet_group(p_ctx[..., :DN_COLS], p_lat[..., :DN_COLS], dn_conv, dn_a_log,
                                    dn_dt_bias, dn_norm, need_ctx_out)
    ml_lat, ml_ctx = mlstm_group(p_ctx[..., DN_COLS:], p_lat[..., DN_COLS:], ml_ig_bias, ml_fg_bias,
                                 ml_norm, need_ctx_out)
    y_lat = jnp.concatenate([dn_lat, ml_lat], axis=-1).astype(h_lat.dtype)
    y_ctx = jnp.concatenate([dn_ctx, ml_ctx], axis=-1).astype(h_ctx.dtype) if need_ctx_out else None
    return y_lat, y_ctx


def swiglu(h, w_ffn_in, w_ffn_out):
    gate, up = jnp.split(h @ w_ffn_in, 2, axis=-1)
    return (jax.nn.silu(gate) * up) @ w_ffn_out


def setup_inputs(seed: int = 0) -> dict:
    key = jax.random.key(seed)
    ks = jax.random.split(key, 20)
    f32 = jnp.float32

    def nrm(k, shape, scale):
        return jax.random.normal(k, shape, f32) * scale

    def gain(k, shape):
        return 1.0 + 0.02 * jax.random.normal(k, shape, f32)

    dt = jnp.exp(jax.random.uniform(ks[9], (DEPTH, 2, DN_HEADS), f32, math.log(1e-3), math.log(1e-1)))
    return {
        'x': nrm(ks[0], (BATCH, SEQ, D_MODEL), 1.0),
        'c': nrm(ks[1], (BATCH, D_MODEL), 1.0),
        'ctx': nrm(ks[2], (BATCH, CTX_LEN, D_MODEL), 1.0),
        'c_ctx': nrm(ks[3], (D_MODEL,), 1.0),
        'w_mod': nrm(ks[4], (DEPTH, D_MODEL, N_MOD * D_MODEL), 0.5 * D_MODEL ** -0.5),
        'b_mod': nrm(ks[5], (DEPTH, N_MOD * D_MODEL), 0.01),
        'norm1': gain(ks[6], (DEPTH, D_MODEL)),
        'w_in': nrm(ks[7], (DEPTH, D_MODEL, D_IN), D_MODEL ** -0.5),
        'dn_conv': nrm(ks[8], (DEPTH, CONV_K, 3 * D_DN), CONV_K ** -0.5),
        'dn_a_log': jnp.log(jax.random.uniform(ks[10], (DEPTH, 2, DN_HEADS), f32, 1.0, 16.0)),
        'dn_dt_bias': dt + jnp.log(-jnp.expm1(-dt)),
        'dn_norm': gain(ks[11], (DEPTH, DN_HEAD_DIM)),
        'ml_ig_bias': nrm(ks[12], (DEPTH, 2, ML_HEADS), 0.1),
        'ml_fg_bias': jax.random.uniform(ks[13], (DEPTH, 2, ML_HEADS), f32, 3.0, 6.0),
        'ml_norm': gain(ks[14], (DEPTH, D_ML)),
        'w_out': nrm(ks[15], (DEPTH, D_MIX, D_MODEL), D_MIX ** -0.5),
        'norm2': gain(ks[16], (DEPTH, D_MODEL)),
        'w_ffn_in': nrm(ks[17], (DEPTH, D_MODEL, 2 * D_FF), D_MODEL ** -0.5),
        'w_ffn_out': nrm(ks[18], (DEPTH, D_FF, D_MODEL), D_FF ** -0.5),
        'final_norm': gain(ks[19], (D_MODEL,)),
    }


def reference(x, c, ctx, c_ctx, w_mod, b_mod, norm1, w_in, dn_conv, dn_a_log, dn_dt_bias, dn_norm,
              ml_ig_bias, ml_fg_bias, ml_norm, w_out, norm2, w_ffn_in, w_ffn_out, final_norm):
    silu_c = jax.nn.silu(c)
    silu_cc = jax.nn.silu(c_ctx)
    for layer in range(DEPTH):
        need_ctx = layer < DEPTH - 1
        sh1, sc1, g1, sh2, sc2, g2 = jnp.split((silu_c @ w_mod[layer] + b_mod[layer])[:, None, :], N_MOD, axis=-1)
        sh1c, sc1c, g1c, sh2c, sc2c, g2c = jnp.split(silu_cc @ w_mod[layer] + b_mod[layer], N_MOD, axis=-1)
        h_lat = rmsnorm(x, norm1[layer]) * (1 + sc1) + sh1
        h_ctx = rmsnorm(ctx, norm1[layer]) * (1 + sc1c) + sh1c
        y_lat, y_ctx = parallel_mixers(h_ctx, h_lat, w_in[layer], dn_conv[layer], dn_a_log[layer],
                                       dn_dt_bias[layer], dn_norm[layer], ml_ig_bias[layer],
                                       ml_fg_bias[layer], ml_norm[layer], need_ctx)
        x = x + g1 * (y_lat @ w_out[layer])
        x = x + g2 * swiglu(rmsnorm(x, norm2[layer]) * (1 + sc2) + sh2, w_ffn_in[layer], w_ffn_out[layer])
        if need_ctx:
            ctx = ctx + g1c * (y_ctx @ w_out[layer])
            ctx = ctx + g2c * swiglu(rmsnorm(ctx, norm2[layer]) * (1 + sc2c) + sh2c, w_ffn_in[layer], w_ffn_out[layer])
    return rmsnorm(x, final_norm)
```

```python
import functools

import jax
import jax.numpy as jnp
from jax import lax
from jax.experimental import pallas as pl
from jax.experimental.pallas import tpu as pltpu

F32 = jnp.float32
BF16 = jnp.bfloat16

CHUNK = 64
GRID_W = 64
HEAD_DIM = 128
N_HEADS = 4
CONV_K = 5
NORM_EPS = 1e-6
N_MOD = 6
GATE_ROWS = 8
NEG_BIG = -1e30
VMEM_LIMIT = 56 * 1024 * 1024

NT_DIMS = (((1,), (1,)), ((), ()))
TN_DIMS = (((0,), (0,)), ((), ()))


def _dot(a, b):
    return jnp.dot(a.astype(BF16), b.astype(BF16), preferred_element_type=F32)


def _dot_nt(a, b):
    return lax.dot_general(a.astype(BF16), b.astype(BF16), NT_DIMS, preferred_element_type=F32)


def _dot_tn(a, b):
    return lax.dot_general(a.astype(BF16), b.astype(BF16), TN_DIMS, preferred_element_type=F32)


def _split3(x):
    hi = x.astype(BF16)
    r1 = x - hi.astype(F32)
    mid = r1.astype(BF16)
    lo = (r1 - mid.astype(F32)).astype(BF16)
    return hi, mid, lo


def _dot_exact_rhs(x, m_bf16):
    hi, mid, lo = _split3(x)
    f = lambda t: jnp.dot(t, m_bf16, preferred_element_type=F32)
    return f(hi) + f(mid) + f(lo)


def _to_columns(x):
    ii = lax.broadcasted_iota(jnp.int32, (CHUNK, CHUNK), 0)
    jj = lax.broadcasted_iota(jnp.int32, (CHUNK, CHUNK), 1)
    eye = (ii == jj).astype(BF16)
    hi, mid, lo = _split3(x)
    f = lambda t: lax.dot_general(eye, t, NT_DIMS, preferred_element_type=F32)
    return f(hi) + f(mid) + f(lo)


def _masks(fwd):
    ii = lax.broadcasted_iota(jnp.int32, (CHUNK, CHUNK), 0)
    jj = lax.broadcasted_iota(jnp.int32, (CHUNK, CHUNK), 1)
    if fwd:
        return ii >= jj, ii > jj
    return ii <= jj, ii < jj


def _softplus(x):
    return jnp.maximum(x, 0.0) + jnp.log(1.0 + jnp.exp(-jnp.abs(x)))


def _sigmoid(x):
    return 1.0 / (1.0 + jnp.exp(-x))


def _silu(x):
    return x * _sigmoid(x)


def _mod_kernel(c_ref, w_ref, b_ref, o_ref):
    sc = _silu(c_ref[...])
    o_ref[0] = jnp.dot(sc, w_ref[...], preferred_element_type=F32,
                       precision=lax.Precision.HIGHEST) + b_ref[0]


def _modulation(cc, w_mod, b_mod):
    rows, d = cc.shape
    return pl.pallas_call(
        _mod_kernel,
        grid=(N_MOD,),
        in_specs=[pl.BlockSpec((rows, d), lambda j: (0, 0)),
                  pl.BlockSpec((d, d), lambda j: (0, j)),
                  pl.BlockSpec((1, 1, d), lambda j: (j, 0, 0))],
        out_specs=pl.BlockSpec((1, rows, d), lambda j: (j, 0, 0)),
        out_shape=jax.ShapeDtypeStruct((N_MOD, rows, d), F32),
        compiler_params=pltpu.CompilerParams(vmem_limit_bytes=VMEM_LIMIT),
        name="modulation",
    )(cc, w_mod, b_mod.reshape(N_MOD, 1, d))


def _rms_mod(x, gain, scale, shift):
    ms = jnp.mean(x * x, axis=-1, keepdims=True)
    return (x * lax.rsqrt(ms + NORM_EPS) * gain) * (1.0 + scale) + shift


def _inproj_kernel(x_ref, mod_ref, n1_ref, wm_ref, wg_ref, p_ref, gt_ref, h_ref, *, n_tile):
    h = _rms_mod(x_ref[0], n1_ref[...], mod_ref[0, 1:2, :], mod_ref[0, 0:1, :])
    h_ref[...] = h.astype(BF16)
    n_cols = p_ref.shape[2]
    for j in range(n_cols // n_tile):
        p_ref[0, :, j * n_tile:(j + 1) * n_tile] = jnp.dot(
            h_ref[...], wm_ref[:, j * n_tile:(j + 1) * n_tile], preferred_element_type=F32)
    gt_ref[0] = lax.dot_general(wg_ref[...], h_ref[...], NT_DIMS, preferred_element_type=F32)


def _in_projection(x, mod, norm1, w_main, w_gate_t, tm):
    b, t, d = x.shape
    n_main = w_main.shape[1]
    n_gate = w_gate_t.shape[0]
    kern = functools.partial(_inproj_kernel, n_tile=512)
    return pl.pallas_call(
        kern,
        grid=(b, t // tm),
        in_specs=[pl.BlockSpec((1, tm, d), lambda i, j: (i, j, 0)),
                  pl.BlockSpec((1, N_MOD, d), lambda i, j: (i, 0, 0)),
                  pl.BlockSpec((1, d), lambda i, j: (0, 0)),
                  pl.BlockSpec((d, n_main), lambda i, j: (0, 0)),
                  pl.BlockSpec((n_gate, d), lambda i, j: (0, 0))],
        out_specs=[pl.BlockSpec((1, tm, n_main), lambda i, j: (i, j, 0)),
                   pl.BlockSpec((1, n_gate, tm), lambda i, j: (i, 0, j))],
        out_shape=[jax.ShapeDtypeStruct((b, t, n_main), F32),
                   jax.ShapeDtypeStruct((b, n_gate, t), F32)],
        scratch_shapes=[pltpu.VMEM((tm, d), BF16)],
        compiler_params=pltpu.CompilerParams(
            dimension_semantics=("arbitrary", "arbitrary"), vmem_limit_bytes=VMEM_LIMIT),
        name="in_projection",
    )(x, mod, norm1.reshape(1, d), w_main, w_gate_t)


def _cumsum_mats():
    ii = lax.broadcasted_iota(jnp.int32, (CHUNK, CHUNK), 0)
    jj = lax.broadcasted_iota(jnp.int32, (CHUNK, CHUNK), 1)
    prefix = (ii <= jj).astype(BF16)
    suffix = (ii >= jj).astype(BF16)
    ones = jnp.ones((CHUNK, CHUNK), BF16)
    return prefix, suffix, ones


def _row_parity(shape):
    return lax.broadcasted_iota(jnp.int32, shape, len(shape) - 2) % 2


def _pack_gate_rows(first, logdecay, n_chunks):
    prefix, suffix, ones = _cumsum_mats()
    flat = logdecay.reshape(n_chunks * GATE_ROWS, CHUNK)
    par = _row_parity(flat.shape)
    cum = jnp.where(par == 0, _dot_exact_rhs(flat, prefix), _dot_exact_rhs(flat, suffix))
    tot = _dot_exact_rhs(flat, ones)
    row = lax.broadcasted_iota(jnp.int32, flat.shape, 0) % GATE_ROWS
    return cum, tot, row


def _conv_block(u_ref, w, t0, rows, total):
    main = u_ref[0, pl.ds(t0, rows), :]
    lo = jnp.maximum(t0 - 8, 0)
    hi = jnp.minimum(t0 + rows, total - 8)
    prev = jnp.where(t0 > 0, u_ref[0, pl.ds(pl.multiple_of(lo, 8), 8), :], 0.0)
    nxt = jnp.where(t0 + rows < total, u_ref[0, pl.ds(pl.multiple_of(hi, 8), 8), :], 0.0)
    ext = jnp.concatenate([prev, main, nxt], axis=0)
    acc = None
    for j in range(CONV_K):
        off = 8 + j - CONV_K // 2
        term = ext[off:off + rows, :] * w[j:j + 1, :]
        acc = term if acc is None else acc + term
    return _silu(acc)


def _l2norm(x):
    return x * lax.rsqrt(jnp.sum(x * x, axis=-1, keepdims=True) + NORM_EPS)


def _unit_tri_inverse(a):
    ii = lax.broadcasted_iota(jnp.int32, (CHUNK, CHUNK), 0)
    jj = lax.broadcasted_iota(jnp.int32, (CHUNK, CHUNK), 1)
    t = (ii == jj).astype(F32) - jnp.where((ii >> 1) == (jj >> 1), a, 0.0)
    for level in range(1, CHUNK.bit_length() - 1):
        same_big = (ii >> (level + 1)) == (jj >> (level + 1))
        same_small = (ii >> level) == (jj >> level)
        e = jnp.where(same_big & jnp.logical_not(same_small), a, 0.0)
        t = t - _dot(_dot(t, e), t)
    return t


def _dn_chunk(q, k, v, x, s, fwd, with_out):
    col = _to_columns(x)
    d = 0 if fwd else 1
    beta_c = col[:, d:d + 1]
    g_c = col[:, 2 + d:3 + d]
    tot_c = col[:, 4 + d:5 + d]
    g_r = x[2 + d:3 + d, :]
    incl, strict = _masks(fwd)
    decay = jnp.exp(jnp.where(incl, g_c - g_r, NEG_BIG))
    e_g = jnp.exp(g_c)
    kb = k * beta_c
    a = jnp.where(strict, _dot_nt(kb, k) * decay, 0.0)
    t = _unit_tri_inverse(a)
    wu = _dot(t, jnp.concatenate([kb * e_g, v * beta_c], axis=1))
    w = wu[:, :HEAD_DIM]
    u = wu[:, HEAD_DIM:]
    v_new = u - _dot(w, s)
    k_tail = k * jnp.exp(tot_c - g_c)
    s_next = s * jnp.exp(tot_c[0:1, 0:1]) + _dot_tn(k_tail, v_new)
    if not with_out:
        return s_next, None
    scores = _dot_nt(q, k) * decay
    o = _dot(q * e_g, s) + _dot(scores, v_new)
    return s_next, o


def _dn_kernel(sc_ref, ql_ref, kl_ref, vl_ref, gate_ref, qc_ref, kc_ref, vc_ref,
               gl_ref, gc_ref, wq_ref, wk_ref, wv_ref, nw_ref, y_ref,
               qs, ks, vs, qcs, kcs, vcs, rl, rc, obuf, s_ref):
    head = pl.program_id(1)
    t_lat = ql_ref.shape[1]
    t_ctx = qc_ref.shape[1]
    nc_lat = t_lat // CHUNK
    nc_ctx = t_ctx // CHUNK
    half = nc_lat // 2
    conv_rows = 256

    def prep(src_refs, dst_refs, total):
        def body(i, carry):
            t0 = pl.multiple_of(i * conv_rows, conv_rows)
            q = _l2norm(_conv_block(src_refs[0], wq_ref[...], t0, conv_rows, total))
            dst_refs[0][pl.ds(t0, conv_rows), :] = q * (HEAD_DIM ** -0.5)
            dst_refs[1][pl.ds(t0, conv_rows), :] = _l2norm(
                _conv_block(src_refs[1], wk_ref[...], t0, conv_rows, total))
            dst_refs[2][pl.ds(t0, conv_rows), :] = _conv_block(
                src_refs[2], wv_ref[...], t0, conv_rows, total)
            return carry
        lax.fori_loop(0, total // conv_rows, body, 0)

    prep((ql_ref, kl_ref, vl_ref), (qs, ks, vs), t_lat)
    prep((qc_ref, kc_ref, vc_ref), (qcs, kcs, vcs), t_ctx)

    a_log_f, a_log_b = sc_ref[head, 0], sc_ref[head, 1]
    dtb_f, dtb_b = sc_ref[head, 2], sc_ref[head, 3]

    def gate_rows(g_ref, dst, n_chunks):
        beta_raw = g_ref[0, 0, 0]
        alpha_raw = g_ref[0, 0, 1]
        par = _row_parity(alpha_raw.shape)
        a_vec = jnp.exp(jnp.where(par == 0, a_log_f, a_log_b))
        dtb = jnp.where(par == 0, dtb_f, dtb_b)
        g = -a_vec * _softplus(alpha_raw + dtb)
        cum, tot, row = _pack_gate_rows(None, g, n_chunks)
        beta = _sigmoid(beta_raw).reshape(n_chunks * GATE_ROWS, CHUNK)
        packed = jnp.where(row < 2, beta, jnp.where(row < 4, cum, tot))
        dst[...] = packed.reshape(n_chunks, GATE_ROWS, CHUNK)

    gate_rows(gl_ref, rl, nc_lat)
    gate_rows(gc_ref, rc, nc_ctx)

    s_ref[...] = jnp.zeros_like(s_ref)

    def ctx_body(n, carry):
        cf = n
        cb = nc_ctx - 1 - n
        rf = pl.multiple_of(cf * CHUNK, CHUNK)
        rb = pl.multiple_of(cb * CHUNK, CHUNK)
        sf, _ = _dn_chunk(qcs[pl.ds(rf, CHUNK), :], kcs[pl.ds(rf, CHUNK), :], vcs[pl.ds(rf, CHUNK), :],
                          rc[cf], s_ref[0], True, False)
        sb, _ = _dn_chunk(qcs[pl.ds(rb, CHUNK), :], kcs[pl.ds(rb, CHUNK), :], vcs[pl.ds(rb, CHUNK), :],
                          rc[cb], s_ref[1], False, False)
        s_ref[0] = sf
        s_ref[1] = sb
        return carry

    lax.fori_loop(0, nc_ctx, ctx_body, 0)

    def finalize(o, rows):
        ms = jnp.mean(o * o, axis=-1, keepdims=True)
        y = o * lax.rsqrt(ms + NORM_EPS) * nw_ref[...]
        return y * _silu(gate_ref[0, pl.ds(rows, CHUNK), :])

    def lat_step(n, second):
        cf = n
        cb = nc_lat - 1 - n
        rf = pl.multiple_of(cf * CHUNK, CHUNK)
        rb = pl.multiple_of(cb * CHUNK, CHUNK)
        sf, of = _dn_chunk(qs[pl.ds(rf, CHUNK), :], ks[pl.ds(rf, CHUNK), :], vs[pl.ds(rf, CHUNK), :],
                           rl[cf], s_ref[0], True, True)
        sb, ob = _dn_chunk(qs[pl.ds(rb, CHUNK), :], ks[pl.ds(rb, CHUNK), :], vs[pl.ds(rb, CHUNK), :],
                           rl[cb], s_ref[1], False, True)
        s_ref[0] = sf
        s_ref[1] = sb
        if second:
            y_ref[0, pl.ds(rf, CHUNK), :] = finalize(of + obuf[pl.ds(rf, CHUNK), :], rf)
            y_ref[0, pl.ds(rb, CHUNK), :] = finalize(ob + obuf[pl.ds(rb, CHUNK), :], rb)
        else:
            obuf[pl.ds(rf, CHUNK), :] = of
            obuf[pl.ds(rb, CHUNK), :] = ob

    def first_body(n, carry):
        lat_step(n, False)
        return carry

    def second_body(n, carry):
        lat_step(n, True)
        return carry

    lax.fori_loop(0, half, first_body, 0)
    lax.fori_loop(half, nc_lat, second_body, 0)


def _deltanet(p_lat, p_ctx, g_lat, g_ctx, scalars, dn_conv, dn_norm):
    b, t_lat, _ = p_lat.shape
    t_ctx = p_ctx.shape[1]
    nc_lat, nc_ctx = t_lat // CHUNK, t_ctx // CHUNK
    assert nc_lat % 2 == 0 and t_lat % 256 == 0 and t_ctx % 256 == 0
    h = N_HEADS
    col = lambda off: (lambda i, j: (i, 0, off + j))
    lat_spec = lambda off: pl.BlockSpec((1, t_lat, HEAD_DIM), col(off))
    ctx_spec = lambda off: pl.BlockSpec((1, t_ctx, HEAD_DIM), col(off))
    conv_spec = lambda off: pl.BlockSpec((CONV_K, HEAD_DIM), lambda i, j: (0, off + j))
    gate_spec = lambda nc: pl.BlockSpec((1, 1, 2, nc, GATE_ROWS, CHUNK), lambda i, j: (i, j, 0, 0, 0, 0))
    return pl.pallas_call(
        _dn_kernel,
        grid=(b, h),
        in_specs=[pl.BlockSpec(memory_space=pltpu.SMEM),
                  lat_spec(0), lat_spec(h), lat_spec(2 * h), lat_spec(3 * h),
                  ctx_spec(0), ctx_spec(h), ctx_spec(2 * h),
                  gate_spec(nc_lat), gate_spec(nc_ctx),
                  conv_spec(0), conv_spec(h), conv_spec(2 * h),
                  pl.BlockSpec((1, HEAD_DIM), lambda i, j: (0, 0))],
        out_specs=pl.BlockSpec((1, t_lat, HEAD_DIM), lambda i, j: (i, 0, j)),
        out_shape=jax.ShapeDtypeStruct((b, t_lat, h * HEAD_DIM), F32),
        scratch_shapes=[pltpu.VMEM((t_lat, HEAD_DIM), F32)] * 3
        + [pltpu.VMEM((t_ctx, HEAD_DIM), F32)] * 3
        + [pltpu.VMEM((nc_lat, GATE_ROWS, CHUNK), F32), pltpu.VMEM((nc_ctx, GATE_ROWS, CHUNK), F32),
           pltpu.VMEM((t_lat, HEAD_DIM), F32), pltpu.VMEM((2, HEAD_DIM, HEAD_DIM), F32)],
        compiler_params=pltpu.CompilerParams(
            dimension_semantics=("arbitrary", "arbitrary"), vmem_limit_bytes=VMEM_LIMIT),
        name="deltanet_scan",
    )(scalars, p_lat, p_lat, p_lat, p_lat, p_ctx, p_ctx, p_ctx, g_lat, g_ctx,
      dn_conv, dn_conv, dn_conv, dn_norm.reshape(1, HEAD_DIM))


def _ml_chunk(q, k, v, x, c_ext, m, fwd, with_out):
    col = _to_columns(x)
    d = 0 if fwd else 1
    a_r = x[d:d + 1, :]
    a_c = col[:, d:d + 1]
    b_c = col[:, 2 + d:3 + d]
    b_tot = x[4 + d:5 + d, 0:1]
    amax = jnp.max(a_r, axis=1, keepdims=True)
    mx = jnp.maximum(m, amax)
    lane = lax.broadcasted_iota(jnp.int32, (CHUNK, HEAD_DIM), 1)
    v_ext = jnp.concatenate([v, (lane == 0).astype(F32)], axis=1)
    c_next = jnp.exp(m - mx) * c_ext + _dot_tn(k * jnp.exp(a_c - mx), v_ext)
    m_next = b_tot + mx
    if not with_out:
        return c_next, m_next, None
    incl, _ = _masks(fwd)
    a_b = jnp.broadcast_to(a_r, (CHUNK, CHUNK))
    cm = jnp.max(jnp.where(incl, a_b, NEG_BIG), axis=1, keepdims=True)
    mm = jnp.maximum(m, cm)
    s = _dot_nt(q, k) * jnp.exp(jnp.where(incl, a_b - mm, NEG_BIG))
    nd = jnp.exp(m - mm) * _dot(q, c_ext) + _dot(s, v_ext)
    num = nd[:, :HEAD_DIM]
    den = nd[:, HEAD_DIM:HEAD_DIM + 1]
    h = num / jnp.maximum(jnp.abs(den), jnp.exp(-(b_c + mm)))
    return c_next, m_next, h


def _ml_kernel(sc_ref, ql_ref, kl_ref, vl_ref, og_ref, qc_ref, kc_ref, vc_ref,
               gl_ref, gc_ref, nw_ref, y_ref, rl, rc, obuf, c_ref, m_ref):
    head = pl.program_id(1)
    t_lat = ql_ref.shape[1]
    t_ctx = qc_ref.shape[1]
    nc_lat = t_lat // CHUNK
    nc_ctx = t_ctx // CHUNK
    half = nc_lat // 2
    k_scale = HEAD_DIM ** -0.5

    igb_f, igb_b = sc_ref[head, 0], sc_ref[head, 1]
    fgb_f, fgb_b = sc_ref[head, 2], sc_ref[head, 3]

    def gate_rows(g_ref, dst, n_chunks):
        ig_raw = g_ref[0, 0, 0]
        fg_raw = g_ref[0, 0, 1]
        par = _row_parity(fg_raw.shape)
        lf = -_softplus(-(fg_raw + jnp.where(par == 0, fgb_f, fgb_b)))
        cum, tot, row = _pack_gate_rows(None, lf, n_chunks)
        ic = (ig_raw + jnp.where(par == 0, igb_f, igb_b)).reshape(n_chunks * GATE_ROWS, CHUNK)
        packed = jnp.where(row < 2, ic - cum, jnp.where(row < 4, cum, tot))
        dst[...] = packed.reshape(n_chunks, GATE_ROWS, CHUNK)

    gate_rows(gl_ref, rl, nc_lat)
    gate_rows(gc_ref, rc, nc_ctx)

    c_ref[...] = jnp.zeros_like(c_ref)
    m_ref[...] = jnp.zeros_like(m_ref)

    def ctx_rows(ref, c):
        return ref[0, pl.ds(pl.multiple_of(c * CHUNK, CHUNK), CHUNK), :]

    def ctx_body(n, carry):
        cf = n
        cb = nc_ctx - 1 - n
        cfn, mf, _ = _ml_chunk(ctx_rows(qc_ref, cf), ctx_rows(kc_ref, cf) * k_scale, ctx_rows(vc_ref, cf),
                               rc[cf], c_ref[0], m_ref[0, 0:1, 0:1], True, False)
        cbn, mb, _ = _ml_chunk(ctx_rows(qc_ref, cb), ctx_rows(kc_ref, cb) * k_scale, ctx_rows(vc_ref, cb),
                               rc[cb], c_ref[1], m_ref[1, 0:1, 0:1], False, False)
        c_ref[0] = cfn
        c_ref[1] = cbn
        m_ref[0] = jnp.broadcast_to(mf, m_ref.shape[1:])
        m_ref[1] = jnp.broadcast_to(mb, m_ref.shape[1:])
        return carry

    lax.fori_loop(0, nc_ctx, ctx_body, 0)

    def lat_rows(ref, w):
        return ref[0, pl.ds(w, CHUNK, stride=GRID_W), :]

    def finalize(hh, w):
        ms = jnp.mean(hh * hh, axis=-1, keepdims=True)
        y = hh * lax.rsqrt(ms + NORM_EPS) * nw_ref[...]
        return y * _sigmoid(lat_rows(og_ref, w))

    def lat_step(n, second):
        cf = n
        cb = nc_lat - 1 - n
        cfn, mf, hf = _ml_chunk(lat_rows(ql_ref, cf), lat_rows(kl_ref, cf) * k_scale, lat_rows(vl_ref, cf),
                                rl[cf], c_ref[0], m_ref[0, 0:1, 0:1], True, True)
        cbn, mb, hb = _ml_chunk(lat_rows(ql_ref, cb), lat_rows(kl_ref, cb) * k_scale, lat_rows(vl_ref, cb),
                                rl[cb], c_ref[1], m_ref[1, 0:1, 0:1], False, True)
        c_ref[0] = cfn
        c_ref[1] = cbn
        m_ref[0] = jnp.broadcast_to(mf, m_ref.shape[1:])
        m_ref[1] = jnp.broadcast_to(mb, m_ref.shape[1:])
        rf = pl.multiple_of(cf * CHUNK, CHUNK)
        rb = pl.multiple_of(cb * CHUNK, CHUNK)
        if second:
            y_ref[0, pl.ds(cf, CHUNK, stride=GRID_W), :] = finalize(hf + obuf[pl.ds(rf, CHUNK), :], cf)
            y_ref[0, pl.ds(cb, CHUNK, stride=GRID_W), :] = finalize(hb + obuf[pl.ds(rb, CHUNK), :], cb)
        else:
            obuf[pl.ds(rf, CHUNK), :] = hf
            obuf[pl.ds(rb, CHUNK), :] = hb

    def first_body(n, carry):
        lat_step(n, False)
        return carry

    def second_body(n, carry):
        lat_step(n, True)
        return carry

    lax.fori_loop(0, half, first_body, 0)
    lax.fori_loop(half, nc_lat, second_body, 0)


def _mlstm(p_lat, p_ctx, g_lat, g_ctx, scalars, ml_norm):
    b, t_lat, _ = p_lat.shape
    t_ctx = p_ctx.shape[1]
    nc_lat, nc_ctx = t_lat // CHUNK, t_ctx // CHUNK
    assert nc_lat % 2 == 0 and t_lat == CHUNK * GRID_W
    h = N_HEADS
    base = 4 * h
    col = lambda off: (lambda i, j: (i, 0, base + off + j))
    lat_spec = lambda off: pl.BlockSpec((1, t_lat, HEAD_DIM), col(off))
    ctx_spec = lambda off: pl.BlockSpec((1, t_ctx, HEAD_DIM), col(off))
    gate_spec = lambda nc: pl.BlockSpec((1, 1, 2, nc, GATE_ROWS, CHUNK), lambda i, j: (i, j, 0, 0, 0, 0))
    return pl.pallas_call(
        _ml_kernel,
        grid=(b, h),
        in_specs=[pl.BlockSpec(memory_space=pltpu.SMEM),
                  lat_spec(0), lat_spec(h), lat_spec(2 * h), lat_spec(3 * h),
                  ctx_spec(0), ctx_spec(h), ctx_spec(2 * h),
                  gate_spec(nc_lat), gate_spec(nc_ctx),
                  pl.BlockSpec((1, HEAD_DIM), lambda i, j: (0, j))],
        out_specs=pl.BlockSpec((1, t_lat, HEAD_DIM), lambda i, j: (i, 0, j)),
        out_shape=jax.ShapeDtypeStruct((b, t_lat, h * HEAD_DIM), F32),
        scratch_shapes=[pltpu.VMEM((nc_lat, GATE_ROWS, CHUNK), F32), pltpu.VMEM((nc_ctx, GATE_ROWS, CHUNK), F32),
                        pltpu.VMEM((t_lat, HEAD_DIM), F32), pltpu.VMEM((2, HEAD_DIM, 2 * HEAD_DIM), F32),
                        pltpu.VMEM((2, 8, HEAD_DIM), F32)],
        compiler_params=pltpu.CompilerParams(
            dimension_semantics=("arbitrary", "arbitrary"), vmem_limit_bytes=VMEM_LIMIT),
        name="mlstm_scan",
    )(scalars, p_lat, p_lat, p_lat, p_lat, p_ctx, p_ctx, p_ctx, g_lat, g_ctx,
      ml_norm.reshape(1, h * HEAD_DIM))


def _ffn_kernel(x_ref, ydn_ref, yml_ref, mod_ref, n2_ref, fn_ref, wo_dn_ref, wo_ml_ref,
                wg_ref, wu_ref, wd_ref, o_ref, h_ref, acc_ref, *, f_tile):
    mix = _dot(ydn_ref[0], wo_dn_ref[...]) + _dot(yml_ref[0], wo_ml_ref[...])
    x1 = x_ref[0] + mod_ref[0, 2:3, :] * mix
    h_ref[...] = _rms_mod(x1, n2_ref[...], mod_ref[0, 4:5, :], mod_ref[0, 3:4, :]).astype(BF16)
    acc_ref[...] = x1
    g2 = mod_ref[0, 5:6, :]
    d_ff = wg_ref.shape[1]
    for j in range(d_ff // f_tile):
        sl = slice(j * f_tile, (j + 1) * f_tile)
        gate = jnp.dot(h_ref[...], wg_ref[:, sl], preferred_element_type=F32)
        up = jnp.dot(h_ref[...], wu_ref[:, sl], preferred_element_type=F32)
        act = (_silu(gate) * up).astype(BF16)
        acc_ref[...] += g2 * jnp.dot(act, wd_ref[sl, :], preferred_element_type=F32)
    x2 = acc_ref[...]
    ms = jnp.mean(x2 * x2, axis=-1, keepdims=True)
    o_ref[0] = x2 * lax.rsqrt(ms + NORM_EPS) * fn_ref[...]


def _out_ffn(x, y_dn, y_ml, mod, norm2, final_norm, wo_dn, wo_ml, w_gate, w_up, w_down, tm):
    b, t, d = x.shape
    d_mix = y_dn.shape[2]
    d_ff = w_gate.shape[1]
    const = lambda shape: pl.BlockSpec(shape, lambda i, j: (0,) * len(shape),
                                       pipeline_mode=pl.Buffered(1))
    kern = functools.partial(_ffn_kernel, f_tile=256)
    return pl.pallas_call(
        kern,
        grid=(b, t // tm),
        in_specs=[pl.BlockSpec((1, tm, d), lambda i, j: (i, j, 0)),
                  pl.BlockSpec((1, tm, d_mix), lambda i, j: (i, j, 0)),
                  pl.BlockSpec((1, tm, d_mix), lambda i, j: (i, j, 0)),
                  pl.BlockSpec((1, N_MOD, d), lambda i, j: (i, 0, 0)),
                  const((1, d)), const((1, d)),
                  const((d_mix, d)), const((d_mix, d)),
                  const((d, d_ff)), const((d, d_ff)), const((d_ff, d))],
        out_specs=pl.BlockSpec((1, tm, d), lambda i, j: (i, j, 0)),
        out_shape=jax.ShapeDtypeStruct((b, t, d), F32),
        scratch_shapes=[pltpu.VMEM((tm, d), BF16), pltpu.VMEM((tm, d), F32)],
        compiler_params=pltpu.CompilerParams(
            dimension_semantics=("arbitrary", "arbitrary"), vmem_limit_bytes=VMEM_LIMIT),
        name="out_ffn",
    )(x, y_dn, y_ml, mod, norm2.reshape(1, d), final_norm.reshape(1, d),
      wo_dn, wo_ml, w_gate, w_up, w_down)


def _gate_weight_rows(w_in, d_group):
    h = N_HEADS
    cols = []
    for mixer in range(2):
        base = mixer * (4 * d_group + 4 * h) + 4 * d_group
        for head in range(h):
            for slab in range(2):
                pair = [base + slab * 2 * h + head, base + slab * 2 * h + h + head]
                cols += pair * (GATE_ROWS // 2)
    return w_in[:, jnp.array(cols)].T


def _chunk_major_gates(gt, col_major):
    b, _, t = gt.shape
    nc = t // CHUNK
    g = gt.reshape(b, 2 * N_HEADS, 2, GATE_ROWS, nc, CHUNK)
    if col_major:
        return g.transpose(0, 1, 2, 5, 3, 4)
    return g.transpose(0, 1, 2, 4, 3, 5)


def kernel(x, c, ctx, c_ctx, w_mod, b_mod, norm1, w_in, dn_conv, dn_a_log, dn_dt_bias, dn_norm,
           ml_ig_bias, ml_fg_bias, ml_norm, w_out, norm2, w_ffn_in, w_ffn_out, final_norm):
    depth = w_mod.shape[0]
    assert depth == 1, "context outputs are only skipped for a single layer"
    b, t_lat, d = x.shape
    h = N_HEADS
    d_group = h * HEAD_DIM
    d_ff = w_ffn_out.shape[1]
    layer = 0

    pad_rows = -(b + 1) % 8
    cc = jnp.concatenate([c, c_ctx[None, :], jnp.zeros((pad_rows, d), F32)], axis=0)
    mod = _modulation(cc, w_mod[layer], b_mod[layer])
    mod_lat = mod[:, :b].transpose(1, 0, 2)
    mod_ctx = jnp.broadcast_to(mod[:, b][None], (b, N_MOD, d))

    w = w_in[layer]
    dn_cols = 4 * d_group + 4 * h
    w_main = jnp.concatenate([w[:, :4 * d_group], w[:, dn_cols:dn_cols + 4 * d_group]], axis=1).astype(BF16)
    w_gate_t = _gate_weight_rows(w, d_group).astype(BF16)
    p_lat, gt_lat = _in_projection(x, mod_lat, norm1[layer], w_main, w_gate_t, tm=512)
    p_ctx, gt_ctx = _in_projection(ctx, mod_ctx, norm1[layer], w_main, w_gate_t, tm=ctx.shape[1])

    g_lat_rm = _chunk_major_gates(gt_lat, False)
    g_lat_cm = _chunk_major_gates(gt_lat, True)
    g_ctx = _chunk_major_gates(gt_ctx, False)

    dn_scal = jnp.concatenate([dn_a_log[layer].T, dn_dt_bias[layer].T], axis=1)
    ml_scal = jnp.concatenate([ml_ig_bias[layer].T, ml_fg_bias[layer].T], axis=1)

    y_dn = _deltanet(p_lat, p_ctx, g_lat_rm[:, :h], g_ctx[:, :h], dn_scal, dn_conv[layer], dn_norm[layer])
    y_ml = _mlstm(p_lat, p_ctx, g_lat_cm[:, h:], g_ctx[:, h:], ml_scal, ml_norm[layer])

    wo = w_out[layer].astype(BF16)
    wf = w_ffn_in[layer].astype(BF16)
    return _out_ffn(x, y_dn, y_ml, mod_lat, norm2[layer], final_norm,
                    wo[:d_group], wo[d_group:], wf[:, :d_ff], wf[:, d_ff:],
                    w_ffn_out[layer].astype(BF16), tm=512)
```

```python
import functools

import jax
import jax.numpy as jnp
from jax import lax
from jax.experimental import pallas as pl
from jax.experimental.pallas import tpu as pltpu

F32 = jnp.float32
BF16 = jnp.bfloat16

DN_CHUNK = 128
ML_CHUNK = 64
ML_UNROLL = 4
GRID_W = 64
HEAD_DIM = 128
N_HEADS = 4
CONV_K = 5
NORM_EPS = 1e-6
N_MOD = 6
GATE_ROWS = 8
NEG_BIG = -1e30
VMEM_LIMIT = 56 * 1024 * 1024

NT_DIMS = (((1,), (1,)), ((), ()))
TN_DIMS = (((0,), (0,)), ((), ()))


def _dot(a, b):
    return jnp.dot(a.astype(BF16), b.astype(BF16), preferred_element_type=F32)


def _dot_nt(a, b):
    return lax.dot_general(a.astype(BF16), b.astype(BF16), NT_DIMS, preferred_element_type=F32)


def _dot_tn(a, b):
    return lax.dot_general(a.astype(BF16), b.astype(BF16), TN_DIMS, preferred_element_type=F32)


def _split3(x):
    hi = x.astype(BF16)
    r1 = x - hi.astype(F32)
    mid = r1.astype(BF16)
    lo = (r1 - mid.astype(F32)).astype(BF16)
    return hi, mid, lo


def _dot_exact_rhs(x, m_bf16):
    hi, mid, lo = _split3(x)
    f = lambda t: jnp.dot(t, m_bf16, preferred_element_type=F32)
    return f(hi) + f(mid) + f(lo)


def _softplus(x):
    return jnp.maximum(x, 0.0) + jnp.log(1.0 + jnp.exp(-jnp.abs(x)))


def _sigmoid(x):
    return 1.0 / (1.0 + jnp.exp(-x))


def _silu(x):
    return x * _sigmoid(x)


def _mod_kernel(c_ref, w_ref, b_ref, o_ref):
    sc = _silu(c_ref[...])
    o_ref[0] = jnp.dot(sc, w_ref[...], preferred_element_type=F32,
                       precision=lax.Precision.HIGHEST) + b_ref[0]


def _modulation(cc, w_mod, b_mod):
    rows, d = cc.shape
    return pl.pallas_call(
        _mod_kernel,
        grid=(N_MOD,),
        in_specs=[pl.BlockSpec((rows, d), lambda j: (0, 0)),
                  pl.BlockSpec((d, d), lambda j: (0, j)),
                  pl.BlockSpec((1, 1, d), lambda j: (j, 0, 0))],
        out_specs=pl.BlockSpec((1, rows, d), lambda j: (j, 0, 0)),
        out_shape=jax.ShapeDtypeStruct((N_MOD, rows, d), F32),
        compiler_params=pltpu.CompilerParams(vmem_limit_bytes=VMEM_LIMIT),
        name="modulation",
    )(cc, w_mod, b_mod.reshape(N_MOD, 1, d))


def _rms_mod(x, gain, scale, shift):
    ms = jnp.mean(x * x, axis=-1, keepdims=True)
    return (x * lax.rsqrt(ms + NORM_EPS) * gain) * (1.0 + scale) + shift


def _inproj_kernel(x_ref, mod_ref, n1_ref, wm_ref, wg_ref, p_ref, gt_ref, h_ref, *, n_tile):
    h = _rms_mod(x_ref[0], n1_ref[...], mod_ref[0, 1:2, :], mod_ref[0, 0:1, :])
    h_ref[...] = h.astype(BF16)
    n_cols = p_ref.shape[2]
    for j in range(n_cols // n_tile):
        p_ref[0, :, j * n_tile:(j + 1) * n_tile] = jnp.dot(
            h_ref[...], wm_ref[:, j * n_tile:(j + 1) * n_tile], preferred_element_type=F32)
    gt_ref[0] = lax.dot_general(wg_ref[...], h_ref[...], NT_DIMS, preferred_element_type=F32)


def _in_projection(x, mod, norm1, w_main, w_gate_t, tm):
    b, t, d = x.shape
    n_main = w_main.shape[1]
    n_gate = w_gate_t.shape[0]
    kern = functools.partial(_inproj_kernel, n_tile=512)
    return pl.pallas_call(
        kern,
        grid=(b, t // tm),
        in_specs=[pl.BlockSpec((1, tm, d), lambda i, j: (i, j, 0)),
                  pl.BlockSpec((1, N_MOD, d), lambda i, j: (i, 0, 0)),
                  pl.BlockSpec((1, d), lambda i, j: (0, 0)),
                  pl.BlockSpec((d, n_main), lambda i, j: (0, 0)),
                  pl.BlockSpec((n_gate, d), lambda i, j: (0, 0))],
        out_specs=[pl.BlockSpec((1, tm, n_main), lambda i, j: (i, j, 0)),
                   pl.BlockSpec((1, n_gate, tm), lambda i, j: (i, 0, j))],
        out_shape=[jax.ShapeDtypeStruct((b, t, n_main), F32),
                   jax.ShapeDtypeStruct((b, n_gate, t), F32)],
        scratch_shapes=[pltpu.VMEM((tm, d), BF16)],
        compiler_params=pltpu.CompilerParams(
            dimension_semantics=("arbitrary", "arbitrary"), vmem_limit_bytes=VMEM_LIMIT),
        name="in_projection",
    )(x, mod, norm1.reshape(1, d), w_main, w_gate_t)


def _iota2(n):
    return (lax.broadcasted_iota(jnp.int32, (n, n), 0), lax.broadcasted_iota(jnp.int32, (n, n), 1))


def _to_columns(x):
    ii, jj = _iota2(x.shape[1])
    eye = (ii == jj).astype(BF16)
    hi, mid, lo = _split3(x)
    f = lambda t: lax.dot_general(eye, t, NT_DIMS, preferred_element_type=F32)
    return f(hi) + f(mid) + f(lo)


def _masks(fwd, n):
    ii, jj = _iota2(n)
    if fwd:
        return ii >= jj, ii > jj
    return ii <= jj, ii < jj


def _row_parity(shape):
    return lax.broadcasted_iota(jnp.int32, shape, len(shape) - 2) % 2


def _cumulate_gate_rows(logdecay):
    n_chunks, _, n = logdecay.shape
    ii, jj = _iota2(n)
    prefix = (ii <= jj).astype(BF16)
    suffix = (ii >= jj).astype(BF16)
    ones = jnp.ones((n, n), BF16)
    flat = logdecay.reshape(n_chunks * GATE_ROWS, n)
    par = _row_parity(flat.shape)
    cum = jnp.where(par == 0, _dot_exact_rhs(flat, prefix), _dot_exact_rhs(flat, suffix))
    tot = _dot_exact_rhs(flat, ones)
    row = lax.broadcasted_iota(jnp.int32, flat.shape, 0) % GATE_ROWS
    return cum, tot, row


def _conv_block(u_ref, w, t0, rows, total):
    main = u_ref[0, pl.ds(t0, rows), :]
    lo = jnp.maximum(t0 - 8, 0)
    hi = jnp.minimum(t0 + rows, total - 8)
    prev = jnp.where(t0 > 0, u_ref[0, pl.ds(pl.multiple_of(lo, 8), 8), :], 0.0)
    nxt = jnp.where(t0 + rows < total, u_ref[0, pl.ds(pl.multiple_of(hi, 8), 8), :], 0.0)
    ext = jnp.concatenate([prev, main, nxt], axis=0)
    acc = None
    for j in range(CONV_K):
        off = 8 + j - CONV_K // 2
        term = ext[off:off + rows, :] * w[j:j + 1, :]
        acc = term if acc is None else acc + term
    return _silu(acc)


def _l2norm(x):
    return x * lax.rsqrt(jnp.sum(x * x, axis=-1, keepdims=True) + NORM_EPS)


def _unit_tri_inverses(a_list):
    n = a_list[0].shape[0]
    ii, jj = _iota2(n)
    eye = (ii == jj).astype(F32)
    pair = (ii >> 1) == (jj >> 1)
    ts = [eye - jnp.where(pair, a, 0.0) for a in a_list]
    for level in range(1, n.bit_length() - 1):
        same_big = (ii >> (level + 1)) == (jj >> (level + 1))
        same_small = (ii >> level) == (jj >> level)
        couple = same_big & jnp.logical_not(same_small)
        es = [jnp.where(couple, a, 0.0).astype(BF16) for a in a_list]
        tbs = [t.astype(BF16) for t in ts]
        tes = [jnp.dot(tb, e, preferred_element_type=F32) for tb, e in zip(tbs, es)]
        ts = [t - jnp.dot(te.astype(BF16), tb, preferred_element_type=F32)
              for t, te, tb in zip(ts, tes, tbs)]
    return ts


def _dn_intra_chunks(qkvx, with_out):
    n = qkvx[0][0].shape[0]
    cols = [_to_columns(x) for _, _, _, x in qkvx]
    kks = [_dot_nt(k, k) for _, k, _, _ in qkvx]
    qks = [_dot_nt(q, k) if with_out else None for q, k, _, _ in qkvx]
    parts = []
    for (q, k, v, x), col, kk in zip(qkvx, cols, kks):
        for d in range(2):
            beta_c = col[:, d:d + 1]
            g_c = col[:, 2 + d:3 + d]
            tot_c = col[:, 4 + d:5 + d]
            g_r = x[2 + d:3 + d, :]
            incl, strict = _masks(d == 0, n)
            decay = jnp.exp(jnp.where(incl, g_c - g_r, NEG_BIG))
            e_g = jnp.exp(g_c)
            a = jnp.where(strict, kk * beta_c * decay, 0.0)
            rhs = jnp.concatenate([k * (beta_c * e_g), v * beta_c], axis=1).astype(BF16)
            parts.append((a, rhs, decay, e_g, tot_c - g_c))
    ts = _unit_tri_inverses([p[0] for p in parts])
    wus = [jnp.dot(t.astype(BF16), p[1], preferred_element_type=F32) for t, p in zip(ts, parts)]
    res = []
    for ci, (q, k, v, x) in enumerate(qkvx):
        per_dir = []
        for d in range(2):
            _, _, decay, e_g, tail = parts[2 * ci + d]
            wu = wus[2 * ci + d]
            k_tail_t = (k * jnp.exp(tail)).T
            qg = q * e_g if with_out else None
            scores = qks[ci] * decay if with_out else None
            per_dir.append((wu[:, :HEAD_DIM], wu[:, HEAD_DIM:], k_tail_t, qg, scores))
        res.append(per_dir)
    return res


def _dn_kernel(sc_ref, ql_ref, kl_ref, vl_ref, gate_ref, qc_ref, kc_ref, vc_ref,
               gl_ref, gc_ref, wq_ref, wk_ref, wv_ref, nw_ref, y_ref,
               qs, ks, vs, rows_s, w_s, u_s, kt_s, qg_s, sc_s, obuf, s_ref):
    head = pl.program_id(1)
    n = DN_CHUNK
    t_lat = ql_ref.shape[1]
    t_ctx = qc_ref.shape[1]
    nc_lat = t_lat // n
    nc_ctx = t_ctx // n
    half = nc_lat // 2
    conv_rows = 256
    ctx_group = 2 if nc_ctx % 2 == 0 else 1
    lat_group = 4 if nc_lat % 4 == 0 else 2

    a_log_f, a_log_b = sc_ref[head, 0], sc_ref[head, 1]
    dtb_f, dtb_b = sc_ref[head, 2], sc_ref[head, 3]

    def chunk_rows(c):
        return pl.ds(pl.multiple_of(c * n, n), n)

    def prep(src_refs, total):
        def body(i, carry):
            t0 = pl.multiple_of(i * conv_rows, conv_rows)
            q = _l2norm(_conv_block(src_refs[0], wq_ref[...], t0, conv_rows, total))
            qs[pl.ds(t0, conv_rows), :] = q * (HEAD_DIM ** -0.5)
            ks[pl.ds(t0, conv_rows), :] = _l2norm(_conv_block(src_refs[1], wk_ref[...], t0, conv_rows, total))
            vs[pl.ds(t0, conv_rows), :] = _conv_block(src_refs[2], wv_ref[...], t0, conv_rows, total)
            return carry
        lax.fori_loop(0, total // conv_rows, body, 0)

    def gate_rows(g_ref, n_chunks):
        beta_raw = g_ref[0, 0, 0]
        alpha_raw = g_ref[0, 0, 1]
        par = _row_parity(alpha_raw.shape)
        a_vec = jnp.exp(jnp.where(par == 0, a_log_f, a_log_b))
        dtb = jnp.where(par == 0, dtb_f, dtb_b)
        cum, tot, row = _cumulate_gate_rows(-a_vec * _softplus(alpha_raw + dtb))
        beta = _sigmoid(beta_raw).reshape(cum.shape)
        packed = jnp.where(row < 2, beta, jnp.where(row < 4, cum, tot))
        rows_s[0:n_chunks] = packed.reshape(n_chunks, GATE_ROWS, n)

    def intra(n_chunks, group, with_out):
        def body(i, carry):
            cs = [i * group + j for j in range(group)]
            rs = [chunk_rows(c) for c in cs]
            res = _dn_intra_chunks([(qs[r, :], ks[r, :], vs[r, :], rows_s[c]) for c, r in zip(cs, rs)], with_out)
            for c, r, per_dir in zip(cs, rs, res):
                for d in range(2):
                    w, u, k_tail_t, qg, scores = per_dir[d]
                    w_s[d, r, :] = w.astype(BF16)
                    u_s[d, r, :] = u
                    kt_s[d, c] = k_tail_t.astype(BF16)
                    if with_out:
                        qg_s[d, r, :] = qg.astype(BF16)
                        sc_s[d, c] = scores.astype(BF16)
            return carry
        lax.fori_loop(0, n_chunks // group, body, 0)

    def state_steps(cf, cb, with_out):
        dc = ((0, cf), (1, cb))
        rs = [chunk_rows(c) for _, c in dc]
        ss = [s_ref[d] for d, _ in dc]
        sbs = [s.astype(BF16) for s in ss]
        ws = [jnp.dot(w_s[d, r, :], sb, preferred_element_type=F32) for (d, _), r, sb in zip(dc, rs, sbs)]
        os = [jnp.dot(qg_s[d, r, :], sb, preferred_element_type=F32) if with_out else None
              for (d, _), r, sb in zip(dc, rs, sbs)]
        vbs = [(u_s[d, r, :] - w).astype(BF16) for (d, _), r, w in zip(dc, rs, ws)]
        ups = [jnp.dot(kt_s[d, c], vb, preferred_element_type=F32) for (d, c), vb in zip(dc, vbs)]
        if with_out:
            os = [o + jnp.dot(sc_s[d, c], vb, preferred_element_type=F32) for (d, c), vb, o in zip(dc, vbs, os)]
        for (d, c), s, up in zip(dc, ss, ups):
            s_ref[d] = s * jnp.exp(rows_s[c][4 + d:5 + d, :]) + up
        return os

    def finalize(o, r):
        ms = jnp.mean(o * o, axis=-1, keepdims=True)
        return o * lax.rsqrt(ms + NORM_EPS) * nw_ref[...] * _silu(gate_ref[0, r, :])

    s_ref[...] = jnp.zeros_like(s_ref)
    prep((qc_ref, kc_ref, vc_ref), t_ctx)
    gate_rows(gc_ref, nc_ctx)
    intra(nc_ctx, ctx_group, False)

    def ctx_body(i, carry):
        state_steps(i, nc_ctx - 1 - i, False)
        return carry
    lax.fori_loop(0, nc_ctx, ctx_body, 0)

    prep((ql_ref, kl_ref, vl_ref), t_lat)
    gate_rows(gl_ref, nc_lat)
    intra(nc_lat, lat_group, True)

    def first_body(i, carry):
        cb = nc_lat - 1 - i
        o_f, o_b = state_steps(i, cb, True)
        obuf[chunk_rows(i), :] = o_f
        obuf[chunk_rows(cb), :] = o_b
        return carry

    def second_body(i, carry):
        cb = nc_lat - 1 - i
        o_f, o_b = state_steps(i, cb, True)
        for c, o in ((i, o_f), (cb, o_b)):
            r = chunk_rows(c)
            y_ref[0, r, :] = finalize(o + obuf[r, :], r)
        return carry

    lax.fori_loop(0, half, first_body, 0)
    lax.fori_loop(half, nc_lat, second_body, 0)


def _deltanet(p_lat, p_ctx, g_lat, g_ctx, scalars, dn_conv, dn_norm):
    b, t_lat, _ = p_lat.shape
    t_ctx = p_ctx.shape[1]
    n = DN_CHUNK
    nc_lat, nc_ctx = t_lat // n, t_ctx // n
    assert n == HEAD_DIM and nc_lat % 2 == 0 and t_lat % 256 == 0 and t_ctx % 256 == 0 and t_ctx <= t_lat
    h = N_HEADS
    col = lambda off: (lambda i, j: (i, 0, off + j))
    lat_spec = lambda off: pl.BlockSpec((1, t_lat, HEAD_DIM), col(off))
    ctx_spec = lambda off: pl.BlockSpec((1, t_ctx, HEAD_DIM), col(off))
    conv_spec = lambda off: pl.BlockSpec((CONV_K, HEAD_DIM), lambda i, j: (0, off + j))
    gate_spec = lambda nc: pl.BlockSpec((1, 1, 2, nc, GATE_ROWS, n), lambda i, j: (i, j, 0, 0, 0, 0))
    seq = lambda dt: pltpu.VMEM((t_lat, HEAD_DIM), dt)
    seq2 = lambda dt: pltpu.VMEM((2, t_lat, HEAD_DIM), dt)
    return pl.pallas_call(
        _dn_kernel,
        grid=(b, h),
        in_specs=[pl.BlockSpec(memory_space=pltpu.SMEM),
                  lat_spec(0), lat_spec(h), lat_spec(2 * h), lat_spec(3 * h),
                  ctx_spec(0), ctx_spec(h), ctx_spec(2 * h),
                  gate_spec(nc_lat), gate_spec(nc_ctx),
                  conv_spec(0), conv_spec(h), conv_spec(2 * h),
                  pl.BlockSpec((1, HEAD_DIM), lambda i, j: (0, 0))],
        out_specs=pl.BlockSpec((1, t_lat, HEAD_DIM), lambda i, j: (i, 0, j)),
        out_shape=jax.ShapeDtypeStruct((b, t_lat, h * HEAD_DIM), F32),
        scratch_shapes=[seq(F32), seq(F32), seq(F32),
                        pltpu.VMEM((nc_lat, GATE_ROWS, n), F32),
                        seq2(BF16), seq2(F32),
                        pltpu.VMEM((2, nc_lat, HEAD_DIM, n), BF16),
                        seq2(BF16),
                        pltpu.VMEM((2, nc_lat, n, n), BF16),
                        seq(F32),
                        pltpu.VMEM((2, HEAD_DIM, HEAD_DIM), F32)],
        compiler_params=pltpu.CompilerParams(
            dimension_semantics=("arbitrary", "arbitrary"), vmem_limit_bytes=VMEM_LIMIT),
        name="deltanet_scan",
    )(scalars, p_lat, p_lat, p_lat, p_lat, p_ctx, p_ctx, p_ctx, g_lat, g_ctx,
      dn_conv, dn_conv, dn_conv, dn_norm.reshape(1, HEAD_DIM))


def _ml_group(problems, states, with_out):
    n = problems[0][1].shape[0]
    lane = lax.broadcasted_iota(jnp.int32, (n, HEAD_DIM), 1)
    ones_col = (lane == 0).astype(BF16)
    cols = [_to_columns(x) for _, _, _, _, x in problems]
    pre = []
    for (d, q, k, v, x), col in zip(problems, cols):
        a_r = x[d:d + 1, :]
        amax = jnp.max(a_r, axis=1, keepdims=True)
        v_ext = jnp.concatenate([v.astype(BF16), ones_col], axis=1)
        kw = (k * jnp.exp(col[:, d:d + 1] - amax)).astype(BF16)
        pre.append((a_r, amax, v_ext, kw))
    ups = [lax.dot_general(kw, v_ext, TN_DIMS, preferred_element_type=F32) for _, _, v_ext, kw in pre]
    if with_out:
        qks = [_dot_nt(q, k) for _, q, k, _, _ in problems]
        mids = []
        for (d, _, _, _, _), (a_r, _, _, _), qk in zip(problems, pre, qks):
            incl, _ = _masks(d == 0, n)
            a_b = jnp.broadcast_to(a_r, (n, n))
            cm = jnp.max(jnp.where(incl, a_b, NEG_BIG), axis=1, keepdims=True)
            mids.append((cm, (qk * jnp.exp(jnp.where(incl, a_b - cm, NEG_BIG))).astype(BF16)))
        intra = [jnp.dot(s, v_ext, preferred_element_type=F32) for (_, s), (_, _, v_ext, _) in zip(mids, pre)]
    states = list(states)
    starts = []
    for (d, _, _, _, x), (_, amax, _, _), up in zip(problems, pre, ups):
        c_ext, m = states[d]
        starts.append((c_ext, m))
        mx = jnp.maximum(m, amax)
        states[d] = (jnp.exp(m - mx) * c_ext + jnp.exp(amax - mx) * up, x[4 + d:5 + d, 0:1] + mx)
    if not with_out:
        return states, [None] * len(problems)
    inter = [_dot(q, c_ext) for (_, q, _, _, _), (c_ext, _) in zip(problems, starts)]
    hs = []
    for (d, _, _, _, _), col, (_, m), (cm, _), qc, sv in zip(problems, cols, starts, mids, inter, intra):
        mm = jnp.maximum(m, cm)
        nd = jnp.exp(m - mm) * qc + jnp.exp(cm - mm) * sv
        den = nd[:, HEAD_DIM:HEAD_DIM + 1]
        hs.append(nd[:, :HEAD_DIM] / jnp.maximum(jnp.abs(den), jnp.exp(-(col[:, 2 + d:3 + d] + mm))))
    return states, hs


def _ml_kernel(sc_ref, ql_ref, kl_ref, vl_ref, og_ref, qc_ref, kc_ref, vc_ref,
               gl_ref, gc_ref, nw_ref, y_ref, rl, rc, obuf, c_ref, m_ref):
    head = pl.program_id(1)
    t_lat = ql_ref.shape[1]
    t_ctx = qc_ref.shape[1]
    nc_lat = t_lat // ML_CHUNK
    nc_ctx = t_ctx // ML_CHUNK
    half = nc_lat // 2
    k_scale = HEAD_DIM ** -0.5

    igb_f, igb_b = sc_ref[head, 0], sc_ref[head, 1]
    fgb_f, fgb_b = sc_ref[head, 2], sc_ref[head, 3]

    def gate_rows(g_ref, dst, n_chunks):
        ig_raw = g_ref[0, 0, 0]
        fg_raw = g_ref[0, 0, 1]
        par = _row_parity(fg_raw.shape)
        lf = -_softplus(-(fg_raw + jnp.where(par == 0, fgb_f, fgb_b)))
        cum, tot, row = _cumulate_gate_rows(lf)
        ic = (ig_raw + jnp.where(par == 0, igb_f, igb_b)).reshape(n_chunks * GATE_ROWS, ML_CHUNK)
        packed = jnp.where(row < 2, ic - cum, jnp.where(row < 4, cum, tot))
        dst[...] = packed.reshape(n_chunks, GATE_ROWS, ML_CHUNK)

    gate_rows(gl_ref, rl, nc_lat)
    gate_rows(gc_ref, rc, nc_ctx)

    c_ref[...] = jnp.zeros_like(c_ref)
    m_ref[...] = jnp.zeros_like(m_ref)

    def run_steps(first_step, n_steps, total, load, gates, with_out):
        problems = []
        for j in range(n_steps):
            for d, c in ((0, first_step + j), (1, total - 1 - first_step - j)):
                problems.append((d, load(0, c), load(1, c) * k_scale, load(2, c), gates[c]))
        states = [(c_ref[d], m_ref[d, 0:1, 0:1]) for d in range(2)]
        states, hs = _ml_group(problems, states, with_out)
        for d, (c_ext, m) in enumerate(states):
            c_ref[d] = c_ext
            m_ref[d] = jnp.broadcast_to(m, m_ref.shape[1:])
        return hs

    ctx_refs = (qc_ref, kc_ref, vc_ref)
    lat_refs = (ql_ref, kl_ref, vl_ref)

    def ctx_load(which, c):
        return ctx_refs[which][0, pl.ds(pl.multiple_of(c * ML_CHUNK, ML_CHUNK), ML_CHUNK), :]

    def lat_rows(ref, w):
        return ref[0, pl.ds(w, ML_CHUNK, stride=GRID_W), :]

    def lat_load(which, c):
        return lat_rows(lat_refs[which], c)

    ctx_unroll = ML_UNROLL if nc_ctx % ML_UNROLL == 0 else 1

    def ctx_body(i, carry):
        run_steps(i * ctx_unroll, ctx_unroll, nc_ctx, ctx_load, rc, False)
        return carry

    lax.fori_loop(0, nc_ctx // ctx_unroll, ctx_body, 0)

    def finalize(hh, w):
        ms = jnp.mean(hh * hh, axis=-1, keepdims=True)
        y = hh * lax.rsqrt(ms + NORM_EPS) * nw_ref[...]
        return y * _sigmoid(lat_rows(og_ref, w))

    def lat_body(i, second):
        first_step = i * ML_UNROLL
        hs = run_steps(first_step, ML_UNROLL, nc_lat, lat_load, rl, True)
        for j in range(ML_UNROLL):
            for d, c in ((0, first_step + j), (1, nc_lat - 1 - first_step - j)):
                hh = hs[2 * j + d]
                r = pl.ds(pl.multiple_of(c * ML_CHUNK, ML_CHUNK), ML_CHUNK)
                if second:
                    y_ref[0, pl.ds(c, ML_CHUNK, stride=GRID_W), :] = finalize(hh + obuf[r, :], c)
                else:
                    obuf[r, :] = hh

    def first_body(i, carry):
        lat_body(i, False)
        return carry

    def second_body(i, carry):
        lat_body(i, True)
        return carry

    lax.fori_loop(0, half // ML_UNROLL, first_body, 0)
    lax.fori_loop(half // ML_UNROLL, nc_lat // ML_UNROLL, second_body, 0)


def _mlstm(p_lat, p_ctx, g_lat, g_ctx, scalars, ml_norm):
    b, t_lat, _ = p_lat.shape
    t_ctx = p_ctx.shape[1]
    nc_lat, nc_ctx = t_lat // ML_CHUNK, t_ctx // ML_CHUNK
    assert nc_lat % (2 * ML_UNROLL) == 0 and t_lat == ML_CHUNK * GRID_W
    h = N_HEADS
    base = 4 * h
    col = lambda off: (lambda i, j: (i, 0, base + off + j))
    lat_spec = lambda off: pl.BlockSpec((1, t_lat, HEAD_DIM), col(off))
    ctx_spec = lambda off: pl.BlockSpec((1, t_ctx, HEAD_DIM), col(off))
    gate_spec = lambda nc: pl.BlockSpec((1, 1, 2, nc, GATE_ROWS, ML_CHUNK), lambda i, j: (i, j, 0, 0, 0, 0))
    return pl.pallas_call(
        _ml_kernel,
        grid=(b, h),
        in_specs=[pl.BlockSpec(memory_space=pltpu.SMEM),
                  lat_spec(0), lat_spec(h), lat_spec(2 * h), lat_spec(3 * h),
                  ctx_spec(0), ctx_spec(h), ctx_spec(2 * h),
                  gate_spec(nc_lat), gate_spec(nc_ctx),
                  pl.BlockSpec((1, HEAD_DIM), lambda i, j: (0, j))],
        out_specs=pl.BlockSpec((1, t_lat, HEAD_DIM), lambda i, j: (i, 0, j)),
        out_shape=jax.ShapeDtypeStruct((b, t_lat, h * HEAD_DIM), F32),
        scratch_shapes=[pltpu.VMEM((nc_lat, GATE_ROWS, ML_CHUNK), F32), pltpu.VMEM((nc_ctx, GATE_ROWS, ML_CHUNK), F32),
                        pltpu.VMEM((t_lat, HEAD_DIM), F32), pltpu.VMEM((2, HEAD_DIM, 2 * HEAD_DIM), F32),
                        pltpu.VMEM((2, 8, HEAD_DIM), F32)],
        compiler_params=pltpu.CompilerParams(
            dimension_semantics=("arbitrary", "arbitrary"), vmem_limit_bytes=VMEM_LIMIT),
        name="mlstm_scan",
    )(scalars, p_lat, p_lat, p_lat, p_lat, p_ctx, p_ctx, p_ctx, g_lat, g_ctx,
      ml_norm.reshape(1, h * HEAD_DIM))


def _ffn_kernel(x_ref, ydn_ref, yml_ref, mod_ref, n2_ref, fn_ref, wo_dn_ref, wo_ml_ref,
                wg_ref, wu_ref, wd_ref, o_ref, h_ref, acc_ref, *, f_tile):
    mix = _dot(ydn_ref[0], wo_dn_ref[...]) + _dot(yml_ref[0], wo_ml_ref[...])
    x1 = x_ref[0] + mod_ref[0, 2:3, :] * mix
    h_ref[...] = _rms_mod(x1, n2_ref[...], mod_ref[0, 4:5, :], mod_ref[0, 3:4, :]).astype(BF16)
    acc_ref[...] = x1
    g2 = mod_ref[0, 5:6, :]
    d_ff = wg_ref.shape[1]
    for j in range(d_ff // f_tile):
        sl = slice(j * f_tile, (j + 1) * f_tile)
        gate = jnp.dot(h_ref[...], wg_ref[:, sl], preferred_element_type=F32)
        up = jnp.dot(h_ref[...], wu_ref[:, sl], preferred_element_type=F32)
        act = (_silu(gate) * up).astype(BF16)
        acc_ref[...] += g2 * jnp.dot(act, wd_ref[sl, :], preferred_element_type=F32)
    x2 = acc_ref[...]
    ms = jnp.mean(x2 * x2, axis=-1, keepdims=True)
    o_ref[0] = x2 * lax.rsqrt(ms + NORM_EPS) * fn_ref[...]


def _out_ffn(x, y_dn, y_ml, mod, norm2, final_norm, wo_dn, wo_ml, w_gate, w_up, w_down, tm):
    b, t, d = x.shape
    d_mix = y_dn.shape[2]
    d_ff = w_gate.shape[1]
    const = lambda shape: pl.BlockSpec(shape, lambda i, j: (0,) * len(shape),
                                       pipeline_mode=pl.Buffered(1))
    kern = functools.partial(_ffn_kernel, f_tile=256)
    return pl.pallas_call(
        kern,
        grid=(b, t // tm),
        in_specs=[pl.BlockSpec((1, tm, d), lambda i, j: (i, j, 0)),
                  pl.BlockSpec((1, tm, d_mix), lambda i, j: (i, j, 0)),
                  pl.BlockSpec((1, tm, d_mix), lambda i, j: (i, j, 0)),
                  pl.BlockSpec((1, N_MOD, d), lambda i, j: (i, 0, 0)),
                  const((1, d)), const((1, d)),
                  const((d_mix, d)), const((d_mix, d)),
                  const((d, d_ff)), const((d, d_ff)), const((d_ff, d))],
        out_specs=pl.BlockSpec((1, tm, d), lambda i, j: (i, j, 0)),
        out_shape=jax.ShapeDtypeStruct((b, t, d), F32),
        scratch_shapes=[pltpu.VMEM((tm, d), BF16), pltpu.VMEM((tm, d), F32)],
        compiler_params=pltpu.CompilerParams(
            dimension_semantics=("arbitrary", "arbitrary"), vmem_limit_bytes=VMEM_LIMIT),
        name="out_ffn",
    )(x, y_dn, y_ml, mod, norm2.reshape(1, d), final_norm.reshape(1, d),
      wo_dn, wo_ml, w_gate, w_up, w_down)


def _gate_weight_rows(w_in, d_group):
    h = N_HEADS
    cols = []
    for mixer in range(2):
        base = mixer * (4 * d_group + 4 * h) + 4 * d_group
        for head in range(h):
            for slab in range(2):
                pair = [base + slab * 2 * h + head, base + slab * 2 * h + h + head]
                cols += pair * (GATE_ROWS // 2)
    return w_in[:, jnp.array(cols)].T


def _chunk_major_gates(gt, chunk, col_major):
    b, _, t = gt.shape
    nc = t // chunk
    g = gt.reshape(b, N_HEADS, 2, GATE_ROWS, nc, chunk)
    if col_major:
        return g.transpose(0, 1, 2, 5, 3, 4)
    return g.transpose(0, 1, 2, 4, 3, 5)


def kernel(x, c, ctx, c_ctx, w_mod, b_mod, norm1, w_in, dn_conv, dn_a_log, dn_dt_bias, dn_norm,
           ml_ig_bias, ml_fg_bias, ml_norm, w_out, norm2, w_ffn_in, w_ffn_out, final_norm):
    depth = w_mod.shape[0]
    assert depth == 1, "context outputs are only skipped for a single layer"
    b, t_lat, d = x.shape
    h = N_HEADS
    d_group = h * HEAD_DIM
    d_ff = w_ffn_out.shape[1]
    layer = 0

    pad_rows = -(b + 1) % 8
    cc = jnp.concatenate([c, c_ctx[None, :], jnp.zeros((pad_rows, d), F32)], axis=0)
    mod = _modulation(cc, w_mod[layer], b_mod[layer])
    mod_lat = mod[:, :b].transpose(1, 0, 2)
    mod_ctx = jnp.broadcast_to(mod[:, b][None], (b, N_MOD, d))

    w = w_in[layer]
    dn_cols = 4 * d_group + 4 * h
    w_main = jnp.concatenate([w[:, :4 * d_group], w[:, dn_cols:dn_cols + 4 * d_group]], axis=1).astype(BF16)
    w_gate_t = _gate_weight_rows(w, d_group).astype(BF16)
    p_lat, gt_lat = _in_projection(x, mod_lat, norm1[layer], w_main, w_gate_t, tm=512)
    p_ctx, gt_ctx = _in_projection(ctx, mod_ctx, norm1[layer], w_main, w_gate_t, tm=ctx.shape[1])

    dn_gate_rows = h * 2 * GATE_ROWS
    g_dn_lat = _chunk_major_gates(gt_lat[:, :dn_gate_rows], DN_CHUNK, False)
    g_dn_ctx = _chunk_major_gates(gt_ctx[:, :dn_gate_rows], DN_CHUNK, False)
    g_ml_lat = _chunk_major_gates(gt_lat[:, dn_gate_rows:], ML_CHUNK, True)
    g_ml_ctx = _chunk_major_gates(gt_ctx[:, dn_gate_rows:], ML_CHUNK, False)

    dn_scal = jnp.concatenate([dn_a_log[layer].T, dn_dt_bias[layer].T], axis=1)
    ml_scal = jnp.concatenate([ml_ig_bias[layer].T, ml_fg_bias[layer].T], axis=1)

    y_dn = _deltanet(p_lat, p_ctx, g_dn_lat, g_dn_ctx, dn_scal, dn_conv[layer], dn_norm[layer])
    y_ml = _mlstm(p_lat, p_ctx, g_ml_lat, g_ml_ctx, ml_scal, ml_norm[layer])

    wo = w_out[layer].astype(BF16)
    wf = w_ffn_in[layer].astype(BF16)
    return _out_ffn(x, y_dn, y_ml, mod_lat, norm2[layer], final_norm,
                    wo[:d_group], wo[d_group:], wf[:, :d_ff], wf[:, d_ff:],
                    w_ffn_out[layer].astype(BF16), tm=512)
```

```python
import functools

import jax
import jax.numpy as jnp
from jax import lax
from jax.experimental import pallas as pl
from jax.experimental.pallas import tpu as pltpu

F32 = jnp.float32
BF16 = jnp.bfloat16

DN_CHUNK = 128
ML_CHUNK = 64
ML_UNROLL = 4
GRID_W = 64
HEAD_DIM = 128
N_HEADS = 4
CONV_K = 5
NORM_EPS = 1e-6
N_MOD = 6
GATE_ROWS = 8
NEG_BIG = -1e30
VMEM_LIMIT = 56 * 1024 * 1024

NT_DIMS = (((1,), (1,)), ((), ()))
TN_DIMS = (((0,), (0,)), ((), ()))


def _dot(a, b):
    return jnp.dot(a.astype(BF16), b.astype(BF16), preferred_element_type=F32)


def _dot_nt(a, b):
    return lax.dot_general(a.astype(BF16), b.astype(BF16), NT_DIMS, preferred_element_type=F32)


def _dot_tn(a, b):
    return lax.dot_general(a.astype(BF16), b.astype(BF16), TN_DIMS, preferred_element_type=F32)


def _split3(x):
    hi = x.astype(BF16)
    r1 = x - hi.astype(F32)
    mid = r1.astype(BF16)
    lo = (r1 - mid.astype(F32)).astype(BF16)
    return hi, mid, lo


def _dot_exact_rhs(x, m_bf16):
    hi, mid, lo = _split3(x)
    f = lambda t: jnp.dot(t, m_bf16, preferred_element_type=F32)
    return f(hi) + f(mid) + f(lo)


def _softplus(x):
    return jnp.maximum(x, 0.0) + jnp.log(1.0 + jnp.exp(-jnp.abs(x)))


def _sigmoid(x):
    return 1.0 / (1.0 + jnp.exp(-x))


def _silu(x):
    return x * _sigmoid(x)


def _mod_kernel(c_ref, w_ref, b_ref, o_ref):
    sc = _silu(c_ref[...])
    o_ref[0] = jnp.dot(sc, w_ref[...], preferred_element_type=F32,
                       precision=lax.Precision.HIGHEST) + b_ref[0]


def _modulation(cc, w_mod, b_mod):
    rows, d = cc.shape
    return pl.pallas_call(
        _mod_kernel,
        grid=(N_MOD,),
        in_specs=[pl.BlockSpec((rows, d), lambda j: (0, 0)),
                  pl.BlockSpec((d, d), lambda j: (0, j)),
                  pl.BlockSpec((1, 1, d), lambda j: (j, 0, 0))],
        out_specs=pl.BlockSpec((1, rows, d), lambda j: (j, 0, 0)),
        out_shape=jax.ShapeDtypeStruct((N_MOD, rows, d), F32),
        compiler_params=pltpu.CompilerParams(vmem_limit_bytes=VMEM_LIMIT),
        name="modulation",
    )(cc, w_mod, b_mod.reshape(N_MOD, 1, d))


def _rms_mod(x, gain, scale, shift):
    ms = jnp.mean(x * x, axis=-1, keepdims=True)
    return (x * lax.rsqrt(ms + NORM_EPS) * gain) * (1.0 + scale) + shift


def _inproj_kernel(x_ref, mod_ref, n1_ref, wdn_ref, wml_ref, wg_ref, pdn_ref, pml_ref, gt_ref, h_ref,
                   *, n_tile, col_major):
    h = _rms_mod(x_ref[0], n1_ref[...], mod_ref[0, 1:2, :], mod_ref[0, 0:1, :])
    h_ref[...] = h.astype(BF16)
    tm = h_ref.shape[0]
    for j in range(wdn_ref.shape[1] // n_tile):
        cols = slice(j * n_tile, (j + 1) * n_tile)
        pdn_ref[0, :, cols] = jnp.dot(h_ref[...], wdn_ref[:, cols], preferred_element_type=F32)
    for j in range(wml_ref.shape[1] // n_tile):
        cols = slice(j * n_tile, (j + 1) * n_tile)
        res = jnp.dot(h_ref[...], wml_ref[:, cols], preferred_element_type=F32)
        if col_major:
            for r in range(tm // GRID_W):
                pml_ref[0, :, r, cols] = res[r * GRID_W:(r + 1) * GRID_W, :]
        else:
            pml_ref[0, :, cols] = res
    gt_ref[0] = lax.dot_general(wg_ref[...], h_ref[...], NT_DIMS, preferred_element_type=F32)


def _in_projection(x, mod, norm1, w_dn, w_ml, w_gate_t, tm, col_major):
    b, t, d = x.shape
    n_dn, n_ml = w_dn.shape[1], w_ml.shape[1]
    n_gate = w_gate_t.shape[0]
    kern = functools.partial(_inproj_kernel, n_tile=512, col_major=col_major)
    if col_major:
        assert tm % GRID_W == 0 and (tm // GRID_W) % 8 == 0 and t % tm == 0
        rows = t // GRID_W
        ml_spec = pl.BlockSpec((1, GRID_W, tm // GRID_W, n_ml), lambda i, j: (i, 0, j, 0))
        ml_shape = jax.ShapeDtypeStruct((b, GRID_W, rows, n_ml), F32)
    else:
        ml_spec = pl.BlockSpec((1, tm, n_ml), lambda i, j: (i, j, 0))
        ml_shape = jax.ShapeDtypeStruct((b, t, n_ml), F32)
    return pl.pallas_call(
        kern,
        grid=(b, t // tm),
        in_specs=[pl.BlockSpec((1, tm, d), lambda i, j: (i, j, 0)),
                  pl.BlockSpec((1, N_MOD, d), lambda i, j: (i, 0, 0)),
                  pl.BlockSpec((1, d), lambda i, j: (0, 0)),
                  pl.BlockSpec((d, n_dn), lambda i, j: (0, 0)),
                  pl.BlockSpec((d, n_ml), lambda i, j: (0, 0)),
                  pl.BlockSpec((n_gate, d), lambda i, j: (0, 0))],
        out_specs=[pl.BlockSpec((1, tm, n_dn), lambda i, j: (i, j, 0)),
                   ml_spec,
                   pl.BlockSpec((1, n_gate, tm), lambda i, j: (i, 0, j))],
        out_shape=[jax.ShapeDtypeStruct((b, t, n_dn), F32),
                   ml_shape,
                   jax.ShapeDtypeStruct((b, n_gate, t), F32)],
        scratch_shapes=[pltpu.VMEM((tm, d), BF16)],
        compiler_params=pltpu.CompilerParams(
            dimension_semantics=("arbitrary", "arbitrary"), vmem_limit_bytes=VMEM_LIMIT),
        name="in_projection",
    )(x, mod, norm1.reshape(1, d), w_dn, w_ml, w_gate_t)


def _iota2(n):
    return (lax.broadcasted_iota(jnp.int32, (n, n), 0), lax.broadcasted_iota(jnp.int32, (n, n), 1))


def _to_columns(x):
    ii, jj = _iota2(x.shape[1])
    eye = (ii == jj).astype(BF16)
    hi, mid, lo = _split3(x)
    f = lambda t: lax.dot_general(eye, t, NT_DIMS, preferred_element_type=F32)
    return f(hi) + f(mid) + f(lo)


def _masks(fwd, n):
    ii, jj = _iota2(n)
    if fwd:
        return ii >= jj, ii > jj
    return ii <= jj, ii < jj


def _row_parity(shape):
    return lax.broadcasted_iota(jnp.int32, shape, len(shape) - 2) % 2


def _cumulate_gate_rows(logdecay):
    n_chunks, _, n = logdecay.shape
    ii, jj = _iota2(n)
    prefix = (ii <= jj).astype(BF16)
    suffix = (ii >= jj).astype(BF16)
    ones = jnp.ones((n, n), BF16)
    flat = logdecay.reshape(n_chunks * GATE_ROWS, n)
    par = _row_parity(flat.shape)
    cum = jnp.where(par == 0, _dot_exact_rhs(flat, prefix), _dot_exact_rhs(flat, suffix))
    tot = _dot_exact_rhs(flat, ones)
    row = lax.broadcasted_iota(jnp.int32, flat.shape, 0) % GATE_ROWS
    return cum, tot, row


def _conv_block(u_ref, w, t0, rows, total):
    main = u_ref[0, pl.ds(t0, rows), :]
    lo = jnp.maximum(t0 - 8, 0)
    hi = jnp.minimum(t0 + rows, total - 8)
    prev = jnp.where(t0 > 0, u_ref[0, pl.ds(pl.multiple_of(lo, 8), 8), :], 0.0)
    nxt = jnp.where(t0 + rows < total, u_ref[0, pl.ds(pl.multiple_of(hi, 8), 8), :], 0.0)
    ext = jnp.concatenate([prev, main, nxt], axis=0)
    acc = None
    for j in range(CONV_K):
        off = 8 + j - CONV_K // 2
        term = ext[off:off + rows, :] * w[j:j + 1, :]
        acc = term if acc is None else acc + term
    return _silu(acc)


def _l2norm(x):
    return x * lax.rsqrt(jnp.sum(x * x, axis=-1, keepdims=True) + NORM_EPS)


def _unit_tri_inverses(a_list):
    n = a_list[0].shape[0]
    ii, jj = _iota2(n)
    eye = (ii == jj).astype(F32)
    pair = (ii >> 1) == (jj >> 1)
    ts = [eye - jnp.where(pair, a, 0.0) for a in a_list]
    for level in range(1, n.bit_length() - 1):
        same_big = (ii >> (level + 1)) == (jj >> (level + 1))
        same_small = (ii >> level) == (jj >> level)
        couple = same_big & jnp.logical_not(same_small)
        es = [jnp.where(couple, a, 0.0).astype(BF16) for a in a_list]
        tbs = [t.astype(BF16) for t in ts]
        tes = [jnp.dot(tb, e, preferred_element_type=F32) for tb, e in zip(tbs, es)]
        ts = [t - jnp.dot(te.astype(BF16), tb, preferred_element_type=F32)
              for t, te, tb in zip(ts, tes, tbs)]
    return ts


def _dn_intra_chunks(qkvx, with_out):
    n = qkvx[0][0].shape[0]
    cols = [_to_columns(x) for _, _, _, x in qkvx]
    kks = [_dot_nt(k, k) for _, k, _, _ in qkvx]
    qks = [_dot_nt(q, k) if with_out else None for q, k, _, _ in qkvx]
    parts = []
    for (q, k, v, x), col, kk in zip(qkvx, cols, kks):
        for d in range(2):
            beta_c = col[:, d:d + 1]
            g_c = col[:, 2 + d:3 + d]
            tot_c = col[:, 4 + d:5 + d]
            g_r = x[2 + d:3 + d, :]
            incl, strict = _masks(d == 0, n)
            decay = jnp.exp(jnp.where(incl, g_c - g_r, NEG_BIG))
            e_g = jnp.exp(g_c)
            a = jnp.where(strict, kk * beta_c * decay, 0.0)
            rhs = jnp.concatenate([k * (beta_c * e_g), v * beta_c], axis=1).astype(BF16)
            parts.append((a, rhs, decay, e_g, tot_c - g_c))
    ts = _unit_tri_inverses([p[0] for p in parts])
    wus = [jnp.dot(t.astype(BF16), p[1], preferred_element_type=F32).astype(BF16) for t, p in zip(ts, parts)]
    k_tail_ts = [(qkvx[i // 2][1] * jnp.exp(p[4])).T.astype(BF16) for i, p in enumerate(parts)]
    state_terms = [jnp.dot(kt, wu, preferred_element_type=F32) for kt, wu in zip(k_tail_ts, wus)]
    if with_out:
        scores = [(qks[i // 2] * p[2]).astype(BF16) for i, p in enumerate(parts)]
        out_terms = [jnp.dot(sc, wu, preferred_element_type=F32) for sc, wu in zip(scores, wus)]
    res = []
    for ci, (q, k, v, x) in enumerate(qkvx):
        per_dir = []
        for d in range(2):
            i = 2 * ci + d
            st = state_terms[i]
            if with_out:
                ot = out_terms[i]
                per_dir.append((-st[:, :HEAD_DIM], st[:, HEAD_DIM:],
                                q * parts[i][3] - ot[:, :HEAD_DIM], ot[:, HEAD_DIM:]))
            else:
                per_dir.append((-st[:, :HEAD_DIM], st[:, HEAD_DIM:], None, None))
        res.append(per_dir)
    return res


def _dn_kernel(sc_ref, ql_ref, kl_ref, vl_ref, gate_ref, qc_ref, kc_ref, vc_ref,
               gl_ref, gc_ref, wq_ref, wk_ref, wv_ref, nw_ref, y_ref,
               qs, ks, vs, rows_s, lhs_s, add_s, obuf, s_ref):
    head = pl.program_id(1)
    n = DN_CHUNK
    t_lat = ql_ref.shape[1]
    t_ctx = qc_ref.shape[1]
    nc_lat = t_lat // n
    nc_ctx = t_ctx // n
    half = nc_lat // 2
    conv_rows = 256
    ctx_group = 2 if nc_ctx % 2 == 0 else 1
    lat_group = 4 if nc_lat % 4 == 0 else 2

    a_log_f, a_log_b = sc_ref[head, 0], sc_ref[head, 1]
    dtb_f, dtb_b = sc_ref[head, 2], sc_ref[head, 3]

    def chunk_rows(c):
        return pl.ds(pl.multiple_of(c * n, n), n)

    def prep(src_refs, total):
        def body(i, carry):
            t0 = pl.multiple_of(i * conv_rows, conv_rows)
            q = _l2norm(_conv_block(src_refs[0], wq_ref[...], t0, conv_rows, total))
            qs[pl.ds(t0, conv_rows), :] = q * (HEAD_DIM ** -0.5)
            ks[pl.ds(t0, conv_rows), :] = _l2norm(_conv_block(src_refs[1], wk_ref[...], t0, conv_rows, total))
            vs[pl.ds(t0, conv_rows), :] = _conv_block(src_refs[2], wv_ref[...], t0, conv_rows, total)
            return carry
        lax.fori_loop(0, total // conv_rows, body, 0)

    def gate_rows(g_ref, n_chunks):
        beta_raw = g_ref[0, 0, 0]
        alpha_raw = g_ref[0, 0, 1]
        par = _row_parity(alpha_raw.shape)
        a_vec = jnp.exp(jnp.where(par == 0, a_log_f, a_log_b))
        dtb = jnp.where(par == 0, dtb_f, dtb_b)
        cum, tot, row = _cumulate_gate_rows(-a_vec * _softplus(alpha_raw + dtb))
        beta = _sigmoid(beta_raw).reshape(cum.shape)
        packed = jnp.where(row < 2, beta, jnp.where(row < 4, cum, tot))
        rows_s[0:n_chunks] = packed.reshape(n_chunks, GATE_ROWS, n)

    def intra(n_chunks, group, with_out):
        def body(i, carry):
            cs = [i * group + j for j in range(group)]
            rs = [chunk_rows(c) for c in cs]
            res = _dn_intra_chunks([(qs[r, :], ks[r, :], vs[r, :], rows_s[c]) for c, r in zip(cs, rs)], with_out)
            for c, per_dir in zip(cs, res):
                for d in range(2):
                    s_mul, s_add, o_mul, o_add = per_dir[d]
                    lhs_s[d, c, 0:HEAD_DIM, :] = s_mul.astype(BF16)
                    add_s[d, c, 0:HEAD_DIM, :] = s_add
                    if with_out:
                        lhs_s[d, c, HEAD_DIM:, :] = o_mul.astype(BF16)
                        add_s[d, c, HEAD_DIM:, :] = o_add
            return carry
        lax.fori_loop(0, n_chunks // group, body, 0)

    def state_steps(cf, cb, with_out):
        dc = ((0, cf), (1, cb))
        rows = slice(None) if with_out else slice(0, HEAD_DIM)
        ss = [s_ref[d] for d, _ in dc]
        rs = [jnp.dot(lhs_s[d, c, rows, :], s.astype(BF16), preferred_element_type=F32) + add_s[d, c, rows, :]
              for (d, c), s in zip(dc, ss)]
        for (d, c), s, r in zip(dc, ss, rs):
            s_ref[d] = s * jnp.exp(rows_s[c][4 + d:5 + d, :]) + r[:HEAD_DIM]
        return [r[HEAD_DIM:] if with_out else None for r in rs]

    def finalize(o, r):
        ms = jnp.mean(o * o, axis=-1, keepdims=True)
        return o * lax.rsqrt(ms + NORM_EPS) * nw_ref[...] * _silu(gate_ref[0, r, :])

    s_ref[...] = jnp.zeros_like(s_ref)
    prep((qc_ref, kc_ref, vc_ref), t_ctx)
    gate_rows(gc_ref, nc_ctx)
    intra(nc_ctx, ctx_group, False)

    def ctx_body(i, carry):
        state_steps(i, nc_ctx - 1 - i, False)
        return carry
    lax.fori_loop(0, nc_ctx, ctx_body, 0)

    prep((ql_ref, kl_ref, vl_ref), t_lat)
    gate_rows(gl_ref, nc_lat)
    intra(nc_lat, lat_group, True)

    def first_body(i, carry):
        cb = nc_lat - 1 - i
        o_f, o_b = state_steps(i, cb, True)
        obuf[chunk_rows(i), :] = o_f
        obuf[chunk_rows(cb), :] = o_b
        return carry

    def second_body(i, carry):
        cb = nc_lat - 1 - i
        o_f, o_b = state_steps(i, cb, True)
        for c, o in ((i, o_f), (cb, o_b)):
            r = chunk_rows(c)
            y_ref[0, r, :] = finalize(o + obuf[r, :], r)
        return carry

    lax.fori_loop(0, half, first_body, 0)
    lax.fori_loop(half, nc_lat, second_body, 0)


def _deltanet(p_lat, p_ctx, g_lat, g_ctx, scalars, dn_conv, dn_norm):
    b, t_lat, _ = p_lat.shape
    t_ctx = p_ctx.shape[1]
    n = DN_CHUNK
    nc_lat, nc_ctx = t_lat // n, t_ctx // n
    assert n == HEAD_DIM and nc_lat % 2 == 0 and t_lat % 256 == 0 and t_ctx % 256 == 0 and t_ctx <= t_lat
    h = N_HEADS
    col = lambda off: (lambda i, j: (i, 0, off + j))
    lat_spec = lambda off: pl.BlockSpec((1, t_lat, HEAD_DIM), col(off))
    ctx_spec = lambda off: pl.BlockSpec((1, t_ctx, HEAD_DIM), col(off))
    conv_spec = lambda off: pl.BlockSpec((CONV_K, HEAD_DIM), lambda i, j: (0, off + j))
    gate_spec = lambda nc: pl.BlockSpec((1, 1, 2, nc, GATE_ROWS, n), lambda i, j: (i, j, 0, 0, 0, 0))
    seq = lambda dt: pltpu.VMEM((t_lat, HEAD_DIM), dt)
    step_terms = lambda dt: pltpu.VMEM((2, nc_lat, HEAD_DIM + n, HEAD_DIM), dt)
    return pl.pallas_call(
        _dn_kernel,
        grid=(b, h),
        in_specs=[pl.BlockSpec(memory_space=pltpu.SMEM),
                  lat_spec(0), lat_spec(h), lat_spec(2 * h), lat_spec(3 * h),
                  ctx_spec(0), ctx_spec(h), ctx_spec(2 * h),
                  gate_spec(nc_lat), gate_spec(nc_ctx),
                  conv_spec(0), conv_spec(h), conv_spec(2 * h),
                  pl.BlockSpec((1, HEAD_DIM), lambda i, j: (0, 0))],
        out_specs=pl.BlockSpec((1, t_lat, HEAD_DIM), lambda i, j: (i, 0, j)),
        out_shape=jax.ShapeDtypeStruct((b, t_lat, h * HEAD_DIM), F32),
        scratch_shapes=[seq(F32), seq(F32), seq(F32),
                        pltpu.VMEM((nc_lat, GATE_ROWS, n), F32),
                        step_terms(BF16), step_terms(F32),
                        seq(F32),
                        pltpu.VMEM((2, HEAD_DIM, HEAD_DIM), F32)],
        compiler_params=pltpu.CompilerParams(
            dimension_semantics=("arbitrary", "arbitrary"), vmem_limit_bytes=VMEM_LIMIT),
        name="deltanet_scan",
    )(scalars, p_lat, p_lat, p_lat, p_lat, p_ctx, p_ctx, p_ctx, g_lat, g_ctx,
      dn_conv, dn_conv, dn_conv, dn_norm.reshape(1, HEAD_DIM))


def _ml_group(problems, states, with_out):
    n = problems[0][1].shape[0]
    lane = lax.broadcasted_iota(jnp.int32, (n, HEAD_DIM), 1)
    ones_col = (lane == 0).astype(BF16)
    cols = [_to_columns(x) for _, _, _, _, x in problems]
    pre = []
    for (d, q, k, v, x), col in zip(problems, cols):
        a_r = x[d:d + 1, :]
        amax = jnp.max(a_r, axis=1, keepdims=True)
        v_ext = jnp.concatenate([v.astype(BF16), ones_col], axis=1)
        kw = (k * jnp.exp(col[:, d:d + 1] - amax)).astype(BF16)
        pre.append((a_r, amax, v_ext, kw))
    ups = [lax.dot_general(kw, v_ext, TN_DIMS, preferred_element_type=F32) for _, _, v_ext, kw in pre]
    if with_out:
        qks = [_dot_nt(q, k) for _, q, k, _, _ in problems]
        mids = []
        for (d, _, _, _, _), (a_r, _, _, _), qk in zip(problems, pre, qks):
            incl, _ = _masks(d == 0, n)
            a_b = jnp.broadcast_to(a_r, (n, n))
            cm = jnp.max(jnp.where(incl, a_b, NEG_BIG), axis=1, keepdims=True)
            mids.append((cm, (qk * jnp.exp(jnp.where(incl, a_b - cm, NEG_BIG))).astype(BF16)))
        intra = [jnp.dot(s, v_ext, preferred_element_type=F32) for (_, s), (_, _, v_ext, _) in zip(mids, pre)]
    states = list(states)
    starts = []
    for (d, _, _, _, x), (_, amax, _, _), up in zip(problems, pre, ups):
        c_ext, m = states[d]
        starts.append((c_ext, m))
        mx = jnp.maximum(m, amax)
        states[d] = (jnp.exp(m - mx) * c_ext + jnp.exp(amax - mx) * up, x[4 + d:5 + d, 0:1] + mx)
    if not with_out:
        return states, [None] * len(problems)
    inter = [_dot(q, c_ext) for (_, q, _, _, _), (c_ext, _) in zip(problems, starts)]
    hs = []
    for (d, _, _, _, _), col, (_, m), (cm, _), qc, sv in zip(problems, cols, starts, mids, inter, intra):
        mm = jnp.maximum(m, cm)
        nd = jnp.exp(m - mm) * qc + jnp.exp(cm - mm) * sv
        den = nd[:, HEAD_DIM:HEAD_DIM + 1]
        hs.append(nd[:, :HEAD_DIM] / jnp.maximum(jnp.abs(den), jnp.exp(-(col[:, 2 + d:3 + d] + mm))))
    return states, hs


def _ml_kernel(sc_ref, ql_ref, kl_ref, vl_ref, og_ref, qc_ref, kc_ref, vc_ref,
               gl_ref, gc_ref, nw_ref, y_ref, rl, rc, obuf, c_ref, m_ref):
    head = pl.program_id(1)
    t_lat = ql_ref.shape[1]
    t_ctx = qc_ref.shape[1]
    nc_lat = t_lat // ML_CHUNK
    nc_ctx = t_ctx // ML_CHUNK
    half = nc_lat // 2
    k_scale = HEAD_DIM ** -0.5

    igb_f, igb_b = sc_ref[head, 0], sc_ref[head, 1]
    fgb_f, fgb_b = sc_ref[head, 2], sc_ref[head, 3]

    def gate_rows(g_ref, dst, n_chunks):
        ig_raw = g_ref[0, 0, 0]
        fg_raw = g_ref[0, 0, 1]
        par = _row_parity(fg_raw.shape)
        lf = -_softplus(-(fg_raw + jnp.where(par == 0, fgb_f, fgb_b)))
        cum, tot, row = _cumulate_gate_rows(lf)
        ic = (ig_raw + jnp.where(par == 0, igb_f, igb_b)).reshape(n_chunks * GATE_ROWS, ML_CHUNK)
        packed = jnp.where(row < 2, ic - cum, jnp.where(row < 4, cum, tot))
        dst[...] = packed.reshape(n_chunks, GATE_ROWS, ML_CHUNK)

    gate_rows(gl_ref, rl, nc_lat)
    gate_rows(gc_ref, rc, nc_ctx)

    c_ref[...] = jnp.zeros_like(c_ref)
    m_ref[...] = jnp.zeros_like(m_ref)

    def run_steps(first_step, n_steps, total, load, gates, with_out):
        problems = []
        for j in range(n_steps):
            for d, c in ((0, first_step + j), (1, total - 1 - first_step - j)):
                problems.append((d, load(0, c), load(1, c) * k_scale, load(2, c), gates[c]))
        states = [(c_ref[d], m_ref[d, 0:1, 0:1]) for d in range(2)]
        states, hs = _ml_group(problems, states, with_out)
        for d, (c_ext, m) in enumerate(states):
            c_ref[d] = c_ext
            m_ref[d] = jnp.broadcast_to(m, m_ref.shape[1:])
        return hs

    ctx_refs = (qc_ref, kc_ref, vc_ref)
    lat_refs = (ql_ref, kl_ref, vl_ref)

    def chunk_rows(c):
        return pl.ds(pl.multiple_of(c * ML_CHUNK, ML_CHUNK), ML_CHUNK)

    def ctx_load(which, c):
        return ctx_refs[which][0, chunk_rows(c), :]

    def lat_load(which, c):
        return lat_refs[which][0, chunk_rows(c), :]

    ctx_unroll = ML_UNROLL if nc_ctx % ML_UNROLL == 0 else 1

    def ctx_body(i, carry):
        run_steps(i * ctx_unroll, ctx_unroll, nc_ctx, ctx_load, rc, False)
        return carry

    lax.fori_loop(0, nc_ctx // ctx_unroll, ctx_body, 0)

    def finalize(hh, r):
        ms = jnp.mean(hh * hh, axis=-1, keepdims=True)
        y = hh * lax.rsqrt(ms + NORM_EPS) * nw_ref[...]
        return y * _sigmoid(og_ref[0, r, :])

    def lat_body(i, second):
        first_step = i * ML_UNROLL
        hs = run_steps(first_step, ML_UNROLL, nc_lat, lat_load, rl, True)
        for j in range(ML_UNROLL):
            for d, c in ((0, first_step + j), (1, nc_lat - 1 - first_step - j)):
                hh = hs[2 * j + d]
                r = chunk_rows(c)
                if second:
                    y_ref[0, r, :] = finalize(hh + obuf[r, :], r)
                else:
                    obuf[r, :] = hh

    def first_body(i, carry):
        lat_body(i, False)
        return carry

    def second_body(i, carry):
        lat_body(i, True)
        return carry

    lax.fori_loop(0, half // ML_UNROLL, first_body, 0)
    lax.fori_loop(half // ML_UNROLL, nc_lat // ML_UNROLL, second_body, 0)


def _mlstm(p_lat, p_ctx, g_lat, g_ctx, scalars, ml_norm):
    b, t_lat, _ = p_lat.shape
    t_ctx = p_ctx.shape[1]
    nc_lat, nc_ctx = t_lat // ML_CHUNK, t_ctx // ML_CHUNK
    assert nc_lat % (2 * ML_UNROLL) == 0 and t_lat == ML_CHUNK * GRID_W
    h = N_HEADS
    col = lambda off: (lambda i, j: (i, 0, off + j))
    lat_spec = lambda off: pl.BlockSpec((1, t_lat, HEAD_DIM), col(off))
    ctx_spec = lambda off: pl.BlockSpec((1, t_ctx, HEAD_DIM), col(off))
    gate_spec = lambda nc: pl.BlockSpec((1, 1, 2, nc, GATE_ROWS, ML_CHUNK), lambda i, j: (i, j, 0, 0, 0, 0))
    return pl.pallas_call(
        _ml_kernel,
        grid=(b, h),
        in_specs=[pl.BlockSpec(memory_space=pltpu.SMEM),
                  lat_spec(0), lat_spec(h), lat_spec(2 * h), lat_spec(3 * h),
                  ctx_spec(0), ctx_spec(h), ctx_spec(2 * h),
                  gate_spec(nc_lat), gate_spec(nc_ctx),
                  pl.BlockSpec((1, HEAD_DIM), lambda i, j: (0, j))],
        out_specs=pl.BlockSpec((1, t_lat, HEAD_DIM), lambda i, j: (i, 0, j)),
        out_shape=jax.ShapeDtypeStruct((b, t_lat, h * HEAD_DIM), F32),
        scratch_shapes=[pltpu.VMEM((nc_lat, GATE_ROWS, ML_CHUNK), F32), pltpu.VMEM((nc_ctx, GATE_ROWS, ML_CHUNK), F32),
                        pltpu.VMEM((t_lat, HEAD_DIM), F32), pltpu.VMEM((2, HEAD_DIM, 2 * HEAD_DIM), F32),
                        pltpu.VMEM((2, 8, HEAD_DIM), F32)],
        compiler_params=pltpu.CompilerParams(
            dimension_semantics=("arbitrary", "arbitrary"), vmem_limit_bytes=VMEM_LIMIT),
        name="mlstm_scan",
    )(scalars, p_lat, p_lat, p_lat, p_lat, p_ctx, p_ctx, p_ctx, g_lat, g_ctx,
      ml_norm.reshape(1, h * HEAD_DIM))


def _ffn_kernel(x_ref, ydn_ref, yml_ref, mod_ref, n2_ref, fn_ref, wo_dn_ref, wo_ml_ref,
                wg_ref, wu_ref, wd_ref, perm_ref, o_ref, h_ref, acc_ref, *, f_tile):
    tm = x_ref.shape[1]
    yml_cm = yml_ref[0].reshape(tm, yml_ref.shape[3]).astype(BF16)
    yml = jnp.dot(perm_ref[...], yml_cm, preferred_element_type=F32).astype(BF16)
    mix = _dot(ydn_ref[0], wo_dn_ref[...]) + jnp.dot(yml, wo_ml_ref[...], preferred_element_type=F32)
    x1 = x_ref[0] + mod_ref[0, 2:3, :] * mix
    h_ref[...] = _rms_mod(x1, n2_ref[...], mod_ref[0, 4:5, :], mod_ref[0, 3:4, :]).astype(BF16)
    acc_ref[...] = x1
    g2 = mod_ref[0, 5:6, :]
    d_ff = wg_ref.shape[1]
    for j in range(d_ff // f_tile):
        sl = slice(j * f_tile, (j + 1) * f_tile)
        gate = jnp.dot(h_ref[...], wg_ref[:, sl], preferred_element_type=F32)
        up = jnp.dot(h_ref[...], wu_ref[:, sl], preferred_element_type=F32)
        act = (_silu(gate) * up).astype(BF16)
        acc_ref[...] += g2 * jnp.dot(act, wd_ref[sl, :], preferred_element_type=F32)
    x2 = acc_ref[...]
    ms = jnp.mean(x2 * x2, axis=-1, keepdims=True)
    o_ref[0] = x2 * lax.rsqrt(ms + NORM_EPS) * fn_ref[...]


def _out_ffn(x, y_dn, y_ml, mod, norm2, final_norm, wo_dn, wo_ml, w_gate, w_up, w_down, tm):
    b, t, d = x.shape
    d_mix = y_dn.shape[2]
    d_ff = w_gate.shape[1]
    rows = tm // GRID_W
    assert tm % GRID_W == 0 and rows % 8 == 0
    out_row = jnp.arange(tm)
    src_row = (out_row % GRID_W) * rows + out_row // GRID_W
    perm = (src_row[:, None] == jnp.arange(tm)[None, :]).astype(BF16)
    const =lambda shape: pl.BlockSpec(shape, lambda i, j: (0,) * len(shape),
                                       pipeline_mode=pl.Buffered(1))
    kern = functools.partial(_ffn_kernel, f_tile=256)
    return pl.pallas_call(
        kern,
        grid=(b, t // tm),
        in_specs=[pl.BlockSpec((1, tm, d), lambda i, j: (i, j, 0)),
                  pl.BlockSpec((1, tm, d_mix), lambda i, j: (i, j, 0)),
                  pl.BlockSpec((1, GRID_W, tm // GRID_W, d_mix), lambda i, j: (i, 0, j, 0)),
                  pl.BlockSpec((1, N_MOD, d), lambda i, j: (i, 0, 0)),
                  const((1, d)), const((1, d)),
                  const((d_mix, d)), const((d_mix, d)),
                  const((d, d_ff)), const((d, d_ff)), const((d_ff, d)), const((tm, tm))],
        out_specs=pl.BlockSpec((1, tm, d), lambda i, j: (i, j, 0)),
        out_shape=jax.ShapeDtypeStruct((b, t, d), F32),
        scratch_shapes=[pltpu.VMEM((tm, d), BF16), pltpu.VMEM((tm, d), F32)],
        compiler_params=pltpu.CompilerParams(
            dimension_semantics=("arbitrary", "arbitrary"), vmem_limit_bytes=VMEM_LIMIT),
        name="out_ffn",
    )(x, y_dn, y_ml, mod, norm2.reshape(1, d), final_norm.reshape(1, d),
      wo_dn, wo_ml, w_gate, w_up, w_down, perm)


def _gate_weight_rows(w_in, d_group):
    h = N_HEADS
    cols = []
    for mixer in range(2):
        base = mixer * (4 * d_group + 4 * h) + 4 * d_group
        for head in range(h):
            for slab in range(2):
                pair = [base + slab * 2 * h + head, base + slab * 2 * h + h + head]
                cols += pair * (GATE_ROWS // 2)
    return w_in[:, jnp.array(cols)].T


def _chunk_major_gates(gt, chunk, col_major):
    b, _, t = gt.shape
    nc = t // chunk
    g = gt.reshape(b, N_HEADS, 2, GATE_ROWS, nc, chunk)
    if col_major:
        return g.transpose(0, 1, 2, 5, 3, 4)
    return g.transpose(0, 1, 2, 4, 3, 5)


def kernel(x, c, ctx, c_ctx, w_mod, b_mod, norm1, w_in, dn_conv, dn_a_log, dn_dt_bias, dn_norm,
           ml_ig_bias, ml_fg_bias, ml_norm, w_out, norm2, w_ffn_in, w_ffn_out, final_norm):
    depth = w_mod.shape[0]
    assert depth == 1, "context outputs are only skipped for a single layer"
    b, t_lat, d = x.shape
    h = N_HEADS
    d_group = h * HEAD_DIM
    d_ff = w_ffn_out.shape[1]
    layer = 0

    pad_rows = -(b + 1) % 8
    cc = jnp.concatenate([c, c_ctx[None, :], jnp.zeros((pad_rows, d), F32)], axis=0)
    mod = _modulation(cc, w_mod[layer], b_mod[layer])
    mod_lat = mod[:, :b].transpose(1, 0, 2)
    mod_ctx = jnp.broadcast_to(mod[:, b][None], (b, N_MOD, d))

    w = w_in[layer]
    dn_cols = 4 * d_group + 4 * h
    w_dn = w[:, :4 * d_group].astype(BF16)
    w_ml = w[:, dn_cols:dn_cols + 4 * d_group].astype(BF16)
    w_gate_t = _gate_weight_rows(w, d_group).astype(BF16)
    pdn_lat, pml_lat, gt_lat = _in_projection(x, mod_lat, norm1[layer], w_dn, w_ml, w_gate_t,
                                              tm=512, col_major=True)
    pdn_ctx, pml_ctx, gt_ctx = _in_projection(ctx, mod_ctx, norm1[layer], w_dn, w_ml, w_gate_t,
                                              tm=ctx.shape[1], col_major=False)
    pml_lat = pml_lat.reshape(b, t_lat, 4 * d_group)

    dn_gate_rows = h * 2 * GATE_ROWS
    g_dn_lat = _chunk_major_gates(gt_lat[:, :dn_gate_rows], DN_CHUNK, False)
    g_dn_ctx = _chunk_major_gates(gt_ctx[:, :dn_gate_rows], DN_CHUNK, False)
    g_ml_lat = _chunk_major_gates(gt_lat[:, dn_gate_rows:], ML_CHUNK, True)
    g_ml_ctx = _chunk_major_gates(gt_ctx[:, dn_gate_rows:], ML_CHUNK, False)

    dn_scal = jnp.concatenate([dn_a_log[layer].T, dn_dt_bias[layer].T], axis=1)
    ml_scal = jnp.concatenate([ml_ig_bias[layer].T, ml_fg_bias[layer].T], axis=1)

    y_dn = _deltanet(pdn_lat, pdn_ctx, g_dn_lat, g_dn_ctx, dn_scal, dn_conv[layer], dn_norm[layer])
    y_ml = _mlstm(pml_lat, pml_ctx, g_ml_lat, g_ml_ctx, ml_scal, ml_norm[layer])
    y_ml = y_ml.reshape(b, GRID_W, t_lat // GRID_W, d_group)

    wo = w_out[layer].astype(BF16)
    wf = w_ffn_in[layer].astype(BF16)
    return _out_ffn(x, y_dn, y_ml, mod_lat, norm2[layer], final_norm,
                    wo[:d_group], wo[d_group:], wf[:, :d_ff], wf[:, d_ff:],
                    w_ffn_out[layer].astype(BF16), tm=512)
```

```python
import functools

import jax
import jax.numpy as jnp
from jax import lax
from jax.experimental import pallas as pl
from jax.experimental.pallas import tpu as pltpu

F32 = jnp.float32
BF16 = jnp.bfloat16

DN_CHUNK = 128
ML_CHUNK = 64
ML_UNROLL = 4
GRID_W = 64
HEAD_DIM = 128
N_HEADS = 4
CONV_K = 5
NORM_EPS = 1e-6
N_MOD = 6
GATE_ROWS = 8
NEG_BIG = -1e30
VMEM_LIMIT = 56 * 1024 * 1024

NT_DIMS = (((1,), (1,)), ((), ()))
TN_DIMS = (((0,), (0,)), ((), ()))


def _dot(a, b):
    return jnp.dot(a.astype(BF16), b.astype(BF16), preferred_element_type=F32)


def _dot_nt(a, b):
    return lax.dot_general(a.astype(BF16), b.astype(BF16), NT_DIMS, preferred_element_type=F32)


def _split3(x):
    hi = x.astype(BF16)
    r1 = x - hi.astype(F32)
    mid = r1.astype(BF16)
    lo = (r1 - mid.astype(F32)).astype(BF16)
    return hi, mid, lo


def _dot_exact_rhs(x, m_bf16):
    hi, mid, lo = _split3(x)
    f = lambda t: jnp.dot(t, m_bf16, preferred_element_type=F32)
    return f(hi) + f(mid) + f(lo)


def _softplus(x):
    return jnp.maximum(x, 0.0) + jnp.log(1.0 + jnp.exp(-jnp.abs(x)))


def _sigmoid(x):
    return 1.0 / (1.0 + jnp.exp(-x))


def _silu(x):
    return x * _sigmoid(x)


def _mod_kernel(c_ref, w_ref, b_ref, o_ref):
    sc = _silu(c_ref[...])
    o_ref[0] = jnp.dot(sc, w_ref[...], preferred_element_type=F32,
                       precision=lax.Precision.HIGHEST) + b_ref[0]


def _modulation(cc, w_mod, b_mod):
    rows, d = cc.shape
    return pl.pallas_call(
        _mod_kernel,
        grid=(N_MOD,),
        in_specs=[pl.BlockSpec((rows, d), lambda j: (0, 0)),
                  pl.BlockSpec((d, d), lambda j: (0, j)),
                  pl.BlockSpec((1, 1, d), lambda j: (j, 0, 0))],
        out_specs=pl.BlockSpec((1, rows, d), lambda j: (j, 0, 0)),
        out_shape=jax.ShapeDtypeStruct((N_MOD, rows, d), F32),
        compiler_params=pltpu.CompilerParams(vmem_limit_bytes=VMEM_LIMIT),
        name="modulation",
    )(cc, w_mod, b_mod.reshape(N_MOD, 1, d))


def _rms_mod(x, gain, scale, shift):
    ms = jnp.mean(x * x, axis=-1, keepdims=True)
    return (x * lax.rsqrt(ms + NORM_EPS) * gain) * (1.0 + scale) + shift


def _inproj_kernel(x_ref, mod_ref, n1_ref, wdn_ref, wml_ref, wg_ref, perm_ref, pdn_ref, pml_ref, gt_ref,
                   h_ref, hcm_ref, *, n_tile, col_major):
    h = _rms_mod(x_ref[0], n1_ref[...], mod_ref[0, 1:2, :], mod_ref[0, 0:1, :])
    h_ref[...] = h.astype(BF16)
    for j in range(wdn_ref.shape[1] // n_tile):
        cols = slice(j * n_tile, (j + 1) * n_tile)
        pdn_ref[0, :, cols] = jnp.dot(h_ref[...], wdn_ref[:, cols], preferred_element_type=F32)
    if col_major:
        hcm_ref[...] = jnp.dot(perm_ref[...], h_ref[...], preferred_element_type=F32).astype(BF16)
    lhs_ref = hcm_ref if col_major else h_ref
    for j in range(wml_ref.shape[1] // n_tile):
        cols = slice(j * n_tile, (j + 1) * n_tile)
        res = jnp.dot(lhs_ref[...], wml_ref[:, cols], preferred_element_type=F32)
        if col_major:
            pml_ref[0, :, :, cols] = res.reshape(pml_ref.shape[1], pml_ref.shape[2], n_tile)
        else:
            pml_ref[0, :, cols] = res
    gt_ref[0] = lax.dot_general(wg_ref[...], h_ref[...], NT_DIMS, preferred_element_type=F32)


def _grid_transpose_perm(tm, to_col_major):
    rows = tm // GRID_W
    idx = jnp.arange(tm)
    if to_col_major:
        src = (idx % rows) * GRID_W + idx // rows
    else:
        src = (idx % GRID_W) * rows + idx // GRID_W
    return (src[:, None] == idx[None, :]).astype(BF16)


def _in_projection(x, mod, norm1, w_dn, w_ml, w_gate_t, tm, col_major):
    b, t, d = x.shape
    n_dn, n_ml = w_dn.shape[1], w_ml.shape[1]
    n_gate = w_gate_t.shape[0]
    kern = functools.partial(_inproj_kernel, n_tile=512, col_major=col_major)
    if col_major:
        assert tm % GRID_W == 0 and (tm // GRID_W) % 8 == 0 and t % tm == 0
        rows = t // GRID_W
        ml_spec = pl.BlockSpec((1, GRID_W, tm // GRID_W, n_ml), lambda i, j: (i, 0, j, 0))
        ml_shape = jax.ShapeDtypeStruct((b, GRID_W, rows, n_ml), F32)
    else:
        ml_spec = pl.BlockSpec((1, tm, n_ml), lambda i, j: (i, j, 0))
        ml_shape = jax.ShapeDtypeStruct((b, t, n_ml), F32)
    return pl.pallas_call(
        kern,
        grid=(b, t // tm),
        in_specs=[pl.BlockSpec((1, tm, d), lambda i, j: (i, j, 0)),
                  pl.BlockSpec((1, N_MOD, d), lambda i, j: (i, 0, 0)),
                  pl.BlockSpec((1, d), lambda i, j: (0, 0)),
                  pl.BlockSpec((d, n_dn), lambda i, j: (0, 0)),
                  pl.BlockSpec((d, n_ml), lambda i, j: (0, 0)),
                  pl.BlockSpec((n_gate, d), lambda i, j: (0, 0)),
                  pl.BlockSpec((tm, tm), lambda i, j: (0, 0))],
        out_specs=[pl.BlockSpec((1, tm, n_dn), lambda i, j: (i, j, 0)),
                   ml_spec,
                   pl.BlockSpec((1, n_gate, tm), lambda i, j: (i, 0, j))],
        out_shape=[jax.ShapeDtypeStruct((b, t, n_dn), F32),
                   ml_shape,
                   jax.ShapeDtypeStruct((b, n_gate, t), F32)],
        scratch_shapes=[pltpu.VMEM((tm, d), BF16), pltpu.VMEM((tm, d), BF16)],
        compiler_params=pltpu.CompilerParams(
            dimension_semantics=("arbitrary", "arbitrary"), vmem_limit_bytes=VMEM_LIMIT),
        name="in_projection",
    )(x, mod, norm1.reshape(1, d), w_dn, w_ml, w_gate_t, _grid_transpose_perm(tm, True))


def _iota2(n):
    return (lax.broadcasted_iota(jnp.int32, (n, n), 0), lax.broadcasted_iota(jnp.int32, (n, n), 1))


def _to_columns(x):
    ii, jj = _iota2(x.shape[1])
    eye = (ii == jj).astype(BF16)
    hi, mid, lo = _split3(x)
    f = lambda t: lax.dot_general(eye, t, NT_DIMS, preferred_element_type=F32)
    return f(hi) + f(mid) + f(lo)


def _masks(fwd, n):
    ii, jj = _iota2(n)
    if fwd:
        return ii >= jj, ii > jj
    return ii <= jj, ii < jj


def _row_parity(shape):
    return lax.broadcasted_iota(jnp.int32, shape, len(shape) - 2) % 2


def _cumulate_gate_rows(logdecay):
    n_chunks, _, n = logdecay.shape
    ii, jj = _iota2(n)
    prefix = (ii <= jj).astype(BF16)
    suffix = (ii >= jj).astype(BF16)
    ones = jnp.ones((n, n), BF16)
    flat = logdecay.reshape(n_chunks * GATE_ROWS, n)
    par = _row_parity(flat.shape)
    cum = jnp.where(par == 0, _dot_exact_rhs(flat, prefix), _dot_exact_rhs(flat, suffix))
    tot = _dot_exact_rhs(flat, ones)
    row = lax.broadcasted_iota(jnp.int32, flat.shape, 0) % GATE_ROWS
    return cum, tot, row


def _conv_block(u_ref, w, t0, rows, total):
    main = u_ref[0, pl.ds(t0, rows), :]
    lo = jnp.maximum(t0 - 8, 0)
    hi = jnp.minimum(t0 + rows, total - 8)
    prev = jnp.where(t0 > 0, u_ref[0, pl.ds(pl.multiple_of(lo, 8), 8), :], 0.0)
    nxt = jnp.where(t0 + rows < total, u_ref[0, pl.ds(pl.multiple_of(hi, 8), 8), :], 0.0)
    ext = jnp.concatenate([prev, main, nxt], axis=0)
    acc = None
    for j in range(CONV_K):
        off = 8 + j - CONV_K // 2
        term = ext[off:off + rows, :] * w[j:j + 1, :]
        acc = term if acc is None else acc + term
    return _silu(acc)


def _conv_block_inner(u_ref, w, t0, rows):
    acc = None
    for j in range(CONV_K):
        term = u_ref[0, pl.ds(t0 + (j - CONV_K // 2), rows), :] * w[j:j + 1, :]
        acc = term if acc is None else acc + term
    return _silu(acc)


def _l2norm(x):
    return x * lax.rsqrt(jnp.sum(x * x, axis=-1, keepdims=True) + NORM_EPS)


def _unit_tri_inverses(a_list):
    n = a_list[0].shape[0]
    ii, jj = _iota2(n)
    eye = (ii == jj).astype(F32)
    pair = (ii >> 1) == (jj >> 1)
    ts = [eye - jnp.where(pair, a, 0.0) for a in a_list]
    for level in range(1, n.bit_length() - 1):
        same_big = (ii >> (level + 1)) == (jj >> (level + 1))
        same_small = (ii >> level) == (jj >> level)
        couple = same_big & jnp.logical_not(same_small)
        es = [jnp.where(couple, a, 0.0).astype(BF16) for a in a_list]
        tbs = [t.astype(BF16) for t in ts]
        tes = [jnp.dot(tb, e, preferred_element_type=F32) for tb, e in zip(tbs, es)]
        ts = [t - jnp.dot(te.astype(BF16), tb, preferred_element_type=F32)
              for t, te, tb in zip(ts, tes, tbs)]
    return ts


def _dn_intra_chunks(qkvx, with_out):
    n = qkvx[0][0].shape[0]
    cols = [_to_columns(x) for _, _, _, x in qkvx]
    kks = [_dot_nt(k, k) for _, k, _, _ in qkvx]
    qks = [_dot_nt(q, k) if with_out else None for q, k, _, _ in qkvx]
    parts = []
    for (q, k, v, x), col, kk in zip(qkvx, cols, kks):
        for d in range(2):
            beta_c = col[:, d:d + 1]
            g_c = col[:, 2 + d:3 + d]
            tot_c = col[:, 4 + d:5 + d]
            g_r = x[2 + d:3 + d, :]
            incl, strict = _masks(d == 0, n)
            decay = jnp.exp(jnp.where(incl, g_c - g_r, NEG_BIG))
            e_g = jnp.exp(g_c)
            a = jnp.where(strict, kk * beta_c * decay, 0.0)
            rhs = jnp.concatenate([k * (beta_c * e_g), v * beta_c], axis=1).astype(BF16)
            parts.append((a, rhs, decay, e_g, tot_c - g_c))
    ts = _unit_tri_inverses([p[0] for p in parts])
    wus = [jnp.dot(t.astype(BF16), p[1], preferred_element_type=F32).astype(BF16) for t, p in zip(ts, parts)]
    k_tail_ts = [(qkvx[i // 2][1] * jnp.exp(p[4])).T.astype(BF16) for i, p in enumerate(parts)]
    state_terms = [jnp.dot(kt, wu, preferred_element_type=F32) for kt, wu in zip(k_tail_ts, wus)]
    if with_out:
        scores = [(qks[i // 2] * p[2]).astype(BF16) for i, p in enumerate(parts)]
        out_terms = [jnp.dot(sc, wu, preferred_element_type=F32) for sc, wu in zip(scores, wus)]
    res = []
    for ci, (q, k, v, x) in enumerate(qkvx):
        per_dir = []
        for d in range(2):
            i = 2 * ci + d
            st = state_terms[i]
            if with_out:
                ot = out_terms[i]
                per_dir.append((-st[:, :HEAD_DIM], st[:, HEAD_DIM:],
                                q * parts[i][3] - ot[:, :HEAD_DIM], ot[:, HEAD_DIM:]))
            else:
                per_dir.append((-st[:, :HEAD_DIM], st[:, HEAD_DIM:], None, None))
        res.append(per_dir)
    return res


def _dn_kernel(sc_ref, ql_ref, kl_ref, vl_ref, gate_ref, qc_ref, kc_ref, vc_ref,
               gl_ref, gc_ref, wq_ref, wk_ref, wv_ref, nw_ref, y_ref,
               qs, ks, vs, rows_s, lhs_s, add_s, obuf, s_ref):
    head = pl.program_id(1)
    n = DN_CHUNK
    t_lat = ql_ref.shape[1]
    t_ctx = qc_ref.shape[1]
    nc_lat = t_lat // n
    nc_ctx = t_ctx // n
    half = nc_lat // 2
    conv_rows = 256
    ctx_group = 2 if nc_ctx % 2 == 0 else 1
    lat_group = 8 if nc_lat % 8 == 0 else 2

    a_log_f, a_log_b = sc_ref[head, 0], sc_ref[head, 1]
    dtb_f, dtb_b = sc_ref[head, 2], sc_ref[head, 3]

    def chunk_rows(c):
        return pl.ds(pl.multiple_of(c * n, n), n)

    def prep(src_refs, total):
        def block(t0, conv):
            q = _l2norm(conv(src_refs[0], wq_ref[...], t0))
            qs[pl.ds(t0, conv_rows), :] = q * (HEAD_DIM ** -0.5)
            ks[pl.ds(t0, conv_rows), :] = _l2norm(conv(src_refs[1], wk_ref[...], t0))
            vs[pl.ds(t0, conv_rows), :] = conv(src_refs[2], wv_ref[...], t0)

        edge = lambda u_ref, w, t0: _conv_block(u_ref, w, t0, conv_rows, total)
        inner = lambda u_ref, w, t0: _conv_block_inner(u_ref, w, t0, conv_rows)
        n_blocks = total // conv_rows
        block(0, edge)
        if n_blocks > 1:
            block((n_blocks - 1) * conv_rows, edge)

        def body(i, carry):
            block(pl.multiple_of(i * conv_rows, conv_rows), inner)
            return carry
        lax.fori_loop(1, n_blocks - 1, body, 0)

    def gate_rows(g_ref, n_chunks):
        beta_raw = g_ref[0, 0, 0]
        alpha_raw = g_ref[0, 0, 1]
        par = _row_parity(alpha_raw.shape)
        a_vec = jnp.exp(jnp.where(par == 0, a_log_f, a_log_b))
        dtb = jnp.where(par == 0, dtb_f, dtb_b)
        cum, tot, row = _cumulate_gate_rows(-a_vec * _softplus(alpha_raw + dtb))
        beta = _sigmoid(beta_raw).reshape(cum.shape)
        packed = jnp.where(row < 2, beta, jnp.where(row < 4, cum, tot))
        rows_s[0:n_chunks] = packed.reshape(n_chunks, GATE_ROWS, n)

    def intra(n_chunks, group, with_out):
        def body(i, carry):
            cs = [i * group + j for j in range(group)]
            rs = [chunk_rows(c) for c in cs]
            res = _dn_intra_chunks([(qs[r, :], ks[r, :], vs[r, :], rows_s[c]) for c, r in zip(cs, rs)], with_out)
            for c, per_dir in zip(cs, res):
                for d in range(2):
                    s_mul, s_add, o_mul, o_add = per_dir[d]
                    lhs_s[d, c, 0:HEAD_DIM, :] = s_mul.astype(BF16)
                    add_s[d, c, 0:HEAD_DIM, :] = s_add
                    if with_out:
                        lhs_s[d, c, HEAD_DIM:, :] = o_mul.astype(BF16)
                        add_s[d, c, HEAD_DIM:, :] = o_add
            return carry
        lax.fori_loop(0, n_chunks // group, body, 0)

    def state_steps(cf, cb, with_out):
        dc = ((0, cf), (1, cb))
        rows = slice(None) if with_out else slice(0, HEAD_DIM)
        ss = [s_ref[d] for d, _ in dc]
        rs = [jnp.dot(lhs_s[d, c, rows, :], s.astype(BF16), preferred_element_type=F32) + add_s[d, c, rows, :]
              for (d, c), s in zip(dc, ss)]
        for (d, c), s, r in zip(dc, ss, rs):
            s_ref[d] = s * jnp.exp(rows_s[c][4 + d:5 + d, :]) + r[:HEAD_DIM]
        return [r[HEAD_DIM:] if with_out else None for r in rs]

    def finalize(o, r):
        ms = jnp.mean(o * o, axis=-1, keepdims=True)
        return o * lax.rsqrt(ms + NORM_EPS) * nw_ref[...] * _silu(gate_ref[0, r, :])

    s_ref[...] = jnp.zeros_like(s_ref)
    prep((qc_ref, kc_ref, vc_ref), t_ctx)
    gate_rows(gc_ref, nc_ctx)
    intra(nc_ctx, ctx_group, False)

    def ctx_body(i, carry):
        state_steps(i, nc_ctx - 1 - i, False)
        return carry
    lax.fori_loop(0, nc_ctx, ctx_body, 0)

    prep((ql_ref, kl_ref, vl_ref), t_lat)
    gate_rows(gl_ref, nc_lat)
    intra(nc_lat, lat_group, True)

    def first_body(i, carry):
        cb = nc_lat - 1 - i
        o_f, o_b = state_steps(i, cb, True)
        obuf[chunk_rows(i), :] = o_f
        obuf[chunk_rows(cb), :] = o_b
        return carry

    def second_body(i, carry):
        cb = nc_lat - 1 - i
        o_f, o_b = state_steps(i, cb, True)
        for c, o in ((i, o_f), (cb, o_b)):
            r = chunk_rows(c)
            y_ref[0, r, :] = finalize(o + obuf[r, :], r)
        return carry

    lax.fori_loop(0, half, first_body, 0)
    lax.fori_loop(half, nc_lat, second_body, 0)


def _deltanet(p_lat, p_ctx, g_lat, g_ctx, scalars, dn_conv, dn_norm):
    b, t_lat, _ = p_lat.shape
    t_ctx = p_ctx.shape[1]
    n = DN_CHUNK
    nc_lat, nc_ctx = t_lat // n, t_ctx // n
    assert n == HEAD_DIM and nc_lat % 2 == 0 and t_lat % 256 == 0 and t_ctx % 256 == 0 and t_ctx <= t_lat
    h = N_HEADS
    col = lambda off: (lambda i, j: (i, 0, off + j))
    lat_spec = lambda off: pl.BlockSpec((1, t_lat, HEAD_DIM), col(off))
    ctx_spec = lambda off: pl.BlockSpec((1, t_ctx, HEAD_DIM), col(off))
    conv_spec = lambda off: pl.BlockSpec((CONV_K, HEAD_DIM), lambda i, j: (0, off + j))
    gate_spec = lambda nc: pl.BlockSpec((1, 1, 2, nc, GATE_ROWS, n), lambda i, j: (i, j, 0, 0, 0, 0))
    seq = lambda dt: pltpu.VMEM((t_lat, HEAD_DIM), dt)
    step_terms = lambda dt: pltpu.VMEM((2, nc_lat, HEAD_DIM + n, HEAD_DIM), dt)
    return pl.pallas_call(
        _dn_kernel,
        grid=(b, h),
        in_specs=[pl.BlockSpec(memory_space=pltpu.SMEM),
                  lat_spec(0), lat_spec(h), lat_spec(2 * h), lat_spec(3 * h),
                  ctx_spec(0), ctx_spec(h), ctx_spec(2 * h),
                  gate_spec(nc_lat), gate_spec(nc_ctx),
                  conv_spec(0), conv_spec(h), conv_spec(2 * h),
                  pl.BlockSpec((1, HEAD_DIM), lambda i, j: (0, 0))],
        out_specs=pl.BlockSpec((1, t_lat, HEAD_DIM), lambda i, j: (i, 0, j)),
        out_shape=jax.ShapeDtypeStruct((b, t_lat, h * HEAD_DIM), F32),
        scratch_shapes=[seq(F32), seq(F32), seq(F32),
                        pltpu.VMEM((nc_lat, GATE_ROWS, n), F32),
                        step_terms(BF16), step_terms(F32),
                        seq(F32),
                        pltpu.VMEM((2, HEAD_DIM, HEAD_DIM), F32)],
        compiler_params=pltpu.CompilerParams(
            dimension_semantics=("arbitrary", "arbitrary"), vmem_limit_bytes=VMEM_LIMIT),
        name="deltanet_scan",
    )(scalars, p_lat, p_lat, p_lat, p_lat, p_ctx, p_ctx, p_ctx, g_lat, g_ctx,
      dn_conv, dn_conv, dn_conv, dn_norm.reshape(1, HEAD_DIM))


def _ml_group(problems, states, with_out):
    n = problems[0][1].shape[0]
    ii, jj = _iota2(n)
    eye = ii == jj
    ones_blk = jnp.ones((n, HEAD_DIM), BF16)

    def lane_spread(rows):
        splits = [_split3(jnp.where(eye, row, 0.0)) for row in rows]
        return [sum(jnp.dot(t, ones_blk, preferred_element_type=F32) for t in sp) for sp in splits]

    a_rows = [x[d:d + 1, :] for d, _, _, _, x in problems]
    a_spreads = lane_spread(a_rows)
    b_spreads = lane_spread([x[2 + d:3 + d, :] for d, _, _, _, x in problems]) if with_out else None
    pre = []
    for (d, q, k, v, x), a_r, a_s in zip(problems, a_rows, a_spreads):
        amax = jnp.max(a_r, axis=1, keepdims=True)
        v_ext = jnp.concatenate([v.astype(BF16), ones_blk], axis=1)
        kw = (k * jnp.exp(a_s - amax)).astype(BF16)
        pre.append((amax, v_ext, kw))
    ups = [lax.dot_general(kw, v_ext, TN_DIMS, preferred_element_type=F32) for _, v_ext, kw in pre]
    if with_out:
        cm_rows = [jnp.max(jnp.where(_masks(d != 0, n)[0], a_s[:, :n], NEG_BIG), axis=0, keepdims=True)
                   for (d, _, _, _, _), a_s in zip(problems, a_spreads)]
        cm_spreads = lane_spread(cm_rows)
        qks = [_dot_nt(q, k) for _, q, k, _, _ in problems]
        scores = []
        for (d, _, _, _, _), a_r, cm_s, qk in zip(problems, a_rows, cm_spreads, qks):
            incl, _ = _masks(d == 0, n)
            expo = jnp.where(incl, jnp.broadcast_to(a_r, (n, n)) - cm_s[:, :n], NEG_BIG)
            scores.append((qk * jnp.exp(expo)).astype(BF16))
        intra = [jnp.dot(s, v_ext, preferred_element_type=F32) for s, (_, v_ext, _) in zip(scores, pre)]
    states = list(states)
    starts = []
    for (d, _, _, _, x), (amax, _, _), up in zip(problems, pre, ups):
        c_ext, m = states[d]
        starts.append((c_ext, m))
        mx = jnp.maximum(m, amax)
        states[d] = (jnp.exp(m - mx) * c_ext + jnp.exp(amax - mx) * up, x[4 + d:5 + d, 0:1] + mx)
    if not with_out:
        return states, [None] * len(problems)
    inter = [_dot(q, c_ext) for (_, q, _, _, _), (c_ext, _) in zip(problems, starts)]
    hs = []
    for (_, m), cm_s, b_s, qc, sv in zip(starts, cm_spreads, b_spreads, inter, intra):
        mm = jnp.maximum(m, cm_s)
        w_inter = jnp.exp(m - mm)
        w_intra = jnp.exp(cm_s - mm)
        num = w_inter * qc[:, :HEAD_DIM] + w_intra * sv[:, :HEAD_DIM]
        den = w_inter * qc[:, HEAD_DIM:] + w_intra * sv[:, HEAD_DIM:]
        hs.append(num / jnp.maximum(jnp.abs(den), jnp.exp(-(b_s + mm))))
    return states, hs


def _ml_kernel(sc_ref, ql_ref, kl_ref, vl_ref, og_ref, qc_ref, kc_ref, vc_ref,
               gl_ref, gc_ref, nw_ref, y_ref, rl, rc, obuf, c_ref, m_ref):
    head = pl.program_id(1)
    t_lat = ql_ref.shape[1]
    t_ctx = qc_ref.shape[1]
    nc_lat = t_lat // ML_CHUNK
    nc_ctx = t_ctx // ML_CHUNK
    half = nc_lat // 2
    k_scale = HEAD_DIM ** -0.5

    igb_f, igb_b = sc_ref[head, 0], sc_ref[head, 1]
    fgb_f, fgb_b = sc_ref[head, 2], sc_ref[head, 3]

    def gate_rows(g_ref, dst, n_chunks):
        ig_raw = g_ref[0, 0, 0]
        fg_raw = g_ref[0, 0, 1]
        par = _row_parity(fg_raw.shape)
        lf = -_softplus(-(fg_raw + jnp.where(par == 0, fgb_f, fgb_b)))
        cum, tot, row = _cumulate_gate_rows(lf)
        ic = (ig_raw + jnp.where(par == 0, igb_f, igb_b)).reshape(n_chunks * GATE_ROWS, ML_CHUNK)
        packed = jnp.where(row < 2, ic - cum, jnp.where(row < 4, cum, tot))
        dst[...] = packed.reshape(n_chunks, GATE_ROWS, ML_CHUNK)

    gate_rows(gl_ref, rl, nc_lat)
    gate_rows(gc_ref, rc, nc_ctx)

    c_ref[...] = jnp.zeros_like(c_ref)
    m_ref[...] = jnp.zeros_like(m_ref)

    def run_steps(first_step, n_steps, total, load, gates, with_out):
        problems = []
        for j in range(n_steps):
            for d, c in ((0, first_step + j), (1, total - 1 - first_step - j)):
                problems.append((d, load(0, c), load(1, c) * k_scale, load(2, c), gates[c]))
        states = [(c_ref[d], m_ref[d, 0:1, 0:1]) for d in range(2)]
        states, hs = _ml_group(problems, states, with_out)
        for d, (c_ext, m) in enumerate(states):
            c_ref[d] = c_ext
            m_ref[d] = jnp.broadcast_to(m, m_ref.shape[1:])
        return hs

    ctx_refs = (qc_ref, kc_ref, vc_ref)
    lat_refs = (ql_ref, kl_ref, vl_ref)

    def chunk_rows(c):
        return pl.ds(pl.multiple_of(c * ML_CHUNK, ML_CHUNK), ML_CHUNK)

    def ctx_load(which, c):
        return ctx_refs[which][0, chunk_rows(c), :]

    def lat_load(which, c):
        return lat_refs[which][0, chunk_rows(c), :]

    ctx_unroll = ML_UNROLL if nc_ctx % ML_UNROLL == 0 else 1

    def ctx_body(i, carry):
        run_steps(i * ctx_unroll, ctx_unroll, nc_ctx, ctx_load, rc, False)
        return carry

    lax.fori_loop(0, nc_ctx // ctx_unroll, ctx_body, 0)

    def finalize(hh, r):
        ms = jnp.mean(hh * hh, axis=-1, keepdims=True)
        y = hh * lax.rsqrt(ms + NORM_EPS) * nw_ref[...]
        return y * _sigmoid(og_ref[0, r, :])

    def lat_body(i, second):
        first_step = i * ML_UNROLL
        hs = run_steps(first_step, ML_UNROLL, nc_lat, lat_load, rl, True)
        for j in range(ML_UNROLL):
            for d, c in ((0, first_step + j), (1, nc_lat - 1 - first_step - j)):
                hh = hs[2 * j + d]
                r = chunk_rows(c)
                if second:
                    y_ref[0, r, :] = finalize(hh + obuf[r, :], r)
                else:
                    obuf[r, :] = hh

    def first_body(i, carry):
        lat_body(i, False)
        return carry

    def second_body(i, carry):
        lat_body(i, True)
        return carry

    lax.fori_loop(0, half // ML_UNROLL, first_body, 0)
    lax.fori_loop(half // ML_UNROLL, nc_lat // ML_UNROLL, second_body, 0)


def _mlstm(p_lat, p_ctx, g_lat, g_ctx, scalars, ml_norm):
    b, t_lat, _ = p_lat.shape
    t_ctx = p_ctx.shape[1]
    nc_lat, nc_ctx = t_lat // ML_CHUNK, t_ctx // ML_CHUNK
    assert nc_lat % (2 * ML_UNROLL) == 0 and t_lat == ML_CHUNK * GRID_W
    h = N_HEADS
    col = lambda off: (lambda i, j: (i, 0, off + j))
    lat_spec = lambda off: pl.BlockSpec((1, t_lat, HEAD_DIM), col(off))
    ctx_spec = lambda off: pl.BlockSpec((1, t_ctx, HEAD_DIM), col(off))
    gate_spec = lambda nc: pl.BlockSpec((1, 1, 2, nc, GATE_ROWS, ML_CHUNK), lambda i, j: (i, j, 0, 0, 0, 0))
    return pl.pallas_call(
        _ml_kernel,
        grid=(b, h),
        in_specs=[pl.BlockSpec(memory_space=pltpu.SMEM),
                  lat_spec(0), lat_spec(h), lat_spec(2 * h), lat_spec(3 * h),
                  ctx_spec(0), ctx_spec(h), ctx_spec(2 * h),
                  gate_spec(nc_lat), gate_spec(nc_ctx),
                  pl.BlockSpec((1, HEAD_DIM), lambda i, j: (0, j))],
        out_specs=pl.BlockSpec((1, t_lat, HEAD_DIM), lambda i, j: (i, 0, j)),
        out_shape=jax.ShapeDtypeStruct((b, t_lat, h * HEAD_DIM), F32),
        scratch_shapes=[pltpu.VMEM((nc_lat, GATE_ROWS, ML_CHUNK), F32), pltpu.VMEM((nc_ctx, GATE_ROWS, ML_CHUNK), F32),
                        pltpu.VMEM((t_lat, HEAD_DIM), F32), pltpu.VMEM((2, HEAD_DIM, 2 * HEAD_DIM), F32),
                        pltpu.VMEM((2, 8, HEAD_DIM), F32)],
        compiler_params=pltpu.CompilerParams(
            dimension_semantics=("arbitrary", "arbitrary"), vmem_limit_bytes=VMEM_LIMIT),
        name="mlstm_scan",
    )(scalars, p_lat, p_lat, p_lat, p_lat, p_ctx, p_ctx, p_ctx, g_lat, g_ctx,
      ml_norm.reshape(1, h * HEAD_DIM))


def _ffn_kernel(x_ref, ydn_ref, yml_ref, mod_ref, n2_ref, fn_ref, wo_dn_ref, wo_ml_ref,
                wg_ref, wu_ref, wd_ref, perm_ref, o_ref, h_ref, acc_ref, *, f_tile):
    tm = x_ref.shape[1]
    yml_cm = yml_ref[0].reshape(tm, yml_ref.shape[3]).astype(BF16)
    yml = jnp.dot(perm_ref[...], yml_cm, preferred_element_type=F32).astype(BF16)
    mix = _dot(ydn_ref[0], wo_dn_ref[...]) + jnp.dot(yml, wo_ml_ref[...], preferred_element_type=F32)
    x1 = x_ref[0] + mod_ref[0, 2:3, :] * mix
    h_ref[...] = _rms_mod(x1, n2_ref[...], mod_ref[0, 4:5, :], mod_ref[0, 3:4, :]).astype(BF16)
    acc_ref[...] = x1
    g2 = mod_ref[0, 5:6, :]
    d_ff = wg_ref.shape[1]
    for j in range(d_ff // f_tile):
        sl = slice(j * f_tile, (j + 1) * f_tile)
        gate = jnp.dot(h_ref[...], wg_ref[:, sl], preferred_element_type=F32)
        up = jnp.dot(h_ref[...], wu_ref[:, sl], preferred_element_type=F32)
        act = (_silu(gate) * up).astype(BF16)
        acc_ref[...] += g2 * jnp.dot(act, wd_ref[sl, :], preferred_element_type=F32)
    x2 = acc_ref[...]
    ms = jnp.mean(x2 * x2, axis=-1, keepdims=True)
    o_ref[0] = x2 * lax.rsqrt(ms + NORM_EPS) * fn_ref[...]


def _out_ffn(x, y_dn, y_ml, mod, norm2, final_norm, wo_dn, wo_ml, w_gate, w_up, w_down, tm):
    b, t, d = x.shape
    d_mix = y_dn.shape[2]
    d_ff = w_gate.shape[1]
    assert tm % GRID_W == 0 and (tm // GRID_W) % 8 == 0
    perm = _grid_transpose_perm(tm, False)
    const =lambda shape: pl.BlockSpec(shape, lambda i, j: (0,) * len(shape),
                                       pipeline_mode=pl.Buffered(1))
    kern = functools.partial(_ffn_kernel, f_tile=256)
    return pl.pallas_call(
        kern,
        grid=(b, t // tm),
        in_specs=[pl.BlockSpec((1, tm, d), lambda i, j: (i, j, 0)),
                  pl.BlockSpec((1, tm, d_mix), lambda i, j: (i, j, 0)),
                  pl.BlockSpec((1, GRID_W, tm // GRID_W, d_mix), lambda i, j: (i, 0, j, 0)),
                  pl.BlockSpec((1, N_MOD, d), lambda i, j: (i, 0, 0)),
                  const((1, d)), const((1, d)),
                  const((d_mix, d)), const((d_mix, d)),
                  const((d, d_ff)), const((d, d_ff)), const((d_ff, d)), const((tm, tm))],
        out_specs=pl.BlockSpec((1, tm, d), lambda i, j: (i, j, 0)),
        out_shape=jax.ShapeDtypeStruct((b, t, d), F32),
        scratch_shapes=[pltpu.VMEM((tm, d), BF16), pltpu.VMEM((tm, d), F32)],
        compiler_params=pltpu.CompilerParams(
            dimension_semantics=("arbitrary", "arbitrary"), vmem_limit_bytes=VMEM_LIMIT),
        name="out_ffn",
    )(x, y_dn, y_ml, mod, norm2.reshape(1, d), final_norm.reshape(1, d),
      wo_dn, wo_ml, w_gate, w_up, w_down, perm)


def _gate_weight_rows(w_in, d_group):
    h = N_HEADS
    cols = []
    for mixer in range(2):
        base = mixer * (4 * d_group + 4 * h) + 4 * d_group
        for head in range(h):
            for slab in range(2):
                pair = [base + slab * 2 * h + head, base + slab * 2 * h + h + head]
                cols += pair * (GATE_ROWS // 2)
    return w_in[:, jnp.array(cols)].T


def _chunk_major_gates(gt, chunk, col_major):
    b, _, t = gt.shape
    nc = t // chunk
    g = gt.reshape(b, N_HEADS, 2, GATE_ROWS, nc, chunk)
    if col_major:
        return g.transpose(0, 1, 2, 5, 3, 4)
    return g.transpose(0, 1, 2, 4, 3, 5)


def kernel(x, c, ctx, c_ctx, w_mod, b_mod, norm1, w_in, dn_conv, dn_a_log, dn_dt_bias, dn_norm,
           ml_ig_bias, ml_fg_bias, ml_norm, w_out, norm2, w_ffn_in, w_ffn_out, final_norm):
    depth = w_mod.shape[0]
    assert depth == 1, "context outputs are only skipped for a single layer"
    b, t_lat, d = x.shape
    h = N_HEADS
    d_group = h * HEAD_DIM
    d_ff = w_ffn_out.shape[1]
    layer = 0

    pad_rows = -(b + 1) % 8
    cc = jnp.concatenate([c, c_ctx[None, :], jnp.zeros((pad_rows, d), F32)], axis=0)
    mod = _modulation(cc, w_mod[layer], b_mod[layer])
    mod_lat = mod[:, :b].transpose(1, 0, 2)
    mod_ctx = jnp.broadcast_to(mod[:, b][None], (b, N_MOD, d))

    w = w_in[layer]
    dn_cols = 4 * d_group + 4 * h
    w_dn = w[:, :4 * d_group].astype(BF16)
    w_ml = w[:, dn_cols:dn_cols + 4 * d_group].astype(BF16)
    w_gate_t = _gate_weight_rows(w, d_group).astype(BF16)
    pdn_lat, pml_lat, gt_lat = _in_projection(x, mod_lat, norm1[layer], w_dn, w_ml, w_gate_t,
                                              tm=512, col_major=True)
    pdn_ctx, pml_ctx, gt_ctx = _in_projection(ctx, mod_ctx, norm1[layer], w_dn, w_ml, w_gate_t,
                                              tm=ctx.shape[1], col_major=False)
    pml_lat = pml_lat.reshape(b, t_lat, 4 * d_group)

    dn_gate_rows = h * 2 * GATE_ROWS
    g_dn_lat = _chunk_major_gates(gt_lat[:, :dn_gate_rows], DN_CHUNK, False)
    g_dn_ctx = _chunk_major_gates(gt_ctx[:, :dn_gate_rows], DN_CHUNK, False)
    g_ml_lat = _chunk_major_gates(gt_lat[:, dn_gate_rows:], ML_CHUNK, True)
    g_ml_ctx = _chunk_major_gates(gt_ctx[:, dn_gate_rows:], ML_CHUNK, False)

    dn_scal = jnp.concatenate([dn_a_log[layer].T, dn_dt_bias[layer].T], axis=1)
    ml_scal = jnp.concatenate([ml_ig_bias[layer].T, ml_fg_bias[layer].T], axis=1)

    y_dn = _deltanet(pdn_lat, pdn_ctx, g_dn_lat, g_dn_ctx, dn_scal, dn_conv[layer], dn_norm[layer])
    y_ml = _mlstm(pml_lat, pml_ctx, g_ml_lat, g_ml_ctx, ml_scal, ml_norm[layer])
    y_ml = y_ml.reshape(b, GRID_W, t_lat // GRID_W, d_group)

    wo = w_out[layer].astype(BF16)
    wf = w_ffn_in[layer].astype(BF16)
    return _out_ffn(x, y_dn, y_ml, mod_lat, norm2[layer], final_norm,
                    wo[:d_group], wo[d_group:], wf[:, :d_ff], wf[:, d_ff:],
                    w_ffn_out[layer].astype(BF16), tm=512)
```

```python
import functools

import jax
import jax.numpy as jnp
from jax import lax
from jax.experimental import pallas as pl
from jax.experimental.pallas import tpu as pltpu

F32 = jnp.float32
BF16 = jnp.bfloat16

DN_CHUNK = 128
ML_CHUNK = 64
ML_UNROLL = 4
GRID_W = 64
HEAD_DIM = 128
N_HEADS = 4
CONV_K = 5
NORM_EPS = 1e-6
N_MOD = 6
GATE_ROWS = 8
NEG_BIG = -1e30
VMEM_LIMIT = 56 * 1024 * 1024

NT_DIMS = (((1,), (1,)), ((), ()))
TN_DIMS = (((0,), (0,)), ((), ()))


def _dot(a, b):
    return jnp.dot(a.astype(BF16), b.astype(BF16), preferred_element_type=F32)


def _dot_nt(a, b):
    return lax.dot_general(a.astype(BF16), b.astype(BF16), NT_DIMS, preferred_element_type=F32)


def _split3(x):
    hi = x.astype(BF16)
    r1 = x - hi.astype(F32)
    mid = r1.astype(BF16)
    lo = (r1 - mid.astype(F32)).astype(BF16)
    return hi, mid, lo


def _split2(x):
    hi = x.astype(BF16)
    return hi, (x - hi.astype(F32)).astype(BF16)


def _dot_exact_rhs(x, m_bf16):
    hi, mid, lo = _split3(x)
    f = lambda t: jnp.dot(t, m_bf16, preferred_element_type=F32)
    return f(hi) + f(mid) + f(lo)


def _softplus(x):
    return jnp.maximum(x, 0.0) + jnp.log(1.0 + jnp.exp(-jnp.abs(x)))


def _sigmoid(x):
    return 1.0 / (1.0 + jnp.exp(-x))


def _silu(x):
    return x * _sigmoid(x)


def _mod_kernel(c_ref, w_ref, b_ref, o_ref):
    sc = _silu(c_ref[...])
    o_ref[0] = jnp.dot(sc, w_ref[...], preferred_element_type=F32,
                       precision=lax.Precision.HIGHEST) + b_ref[0]


def _modulation(cc, w_mod, b_mod):
    rows, d = cc.shape
    return pl.pallas_call(
        _mod_kernel,
        grid=(N_MOD,),
        in_specs=[pl.BlockSpec((rows, d), lambda j: (0, 0)),
                  pl.BlockSpec((d, d), lambda j: (0, j)),
                  pl.BlockSpec((1, 1, d), lambda j: (j, 0, 0))],
        out_specs=pl.BlockSpec((1, rows, d), lambda j: (j, 0, 0)),
        out_shape=jax.ShapeDtypeStruct((N_MOD, rows, d), F32),
        compiler_params=pltpu.CompilerParams(vmem_limit_bytes=VMEM_LIMIT),
        name="modulation",
    )(cc, w_mod, b_mod.reshape(N_MOD, 1, d))


def _rms_mod(x, gain, scale, shift):
    ms = jnp.mean(x * x, axis=-1, keepdims=True)
    return (x * lax.rsqrt(ms + NORM_EPS) * gain) * (1.0 + scale) + shift


def _inproj_kernel(x_ref, mod_ref, n1_ref, w_ref, wg_ref, p_ref, gt_ref, h_ref, *, n_tile, col_tile):
    gain, scale, shift = n1_ref[...], mod_ref[0, 1:2, :], mod_ref[0, 0:1, :]
    d = h_ref.shape[1]
    if col_tile:
        rows = x_ref.shape[1]
        for c in range(h_ref.shape[0] // rows):
            h_ref[c * rows:(c + 1) * rows, :] = _rms_mod(
                x_ref[0, :, c * d:(c + 1) * d], gain, scale, shift).astype(BF16)
    else:
        h_ref[...] = _rms_mod(x_ref[0], gain, scale, shift).astype(BF16)
    for j in range(w_ref.shape[1] // n_tile):
        cols = slice(j * n_tile, (j + 1) * n_tile)
        p_ref[0, :, cols] = jnp.dot(h_ref[...], w_ref[:, cols], preferred_element_type=F32).astype(p_ref.dtype)
    gt_ref[0] = lax.dot_general(wg_ref[...], h_ref[...], NT_DIMS, preferred_element_type=F32)


def _in_projection(x, mod, norm1, w, w_gate_t, tm, col_tile, out_dtype):
    b, t, d = x.shape
    n = w.shape[1]
    n_gate = w_gate_t.shape[0]
    kern = functools.partial(_inproj_kernel, n_tile=512, col_tile=col_tile)
    if col_tile:
        rows = t // GRID_W
        assert tm % rows == 0 and t % tm == 0
        x = x.reshape(b, rows, GRID_W * d)
        x_spec = pl.BlockSpec((1, rows, (tm // rows) * d), lambda i, j: (i, 0, j))
    else:
        x_spec = pl.BlockSpec((1, tm, d), lambda i, j: (i, j, 0))
    const = lambda shape: pl.BlockSpec(shape, lambda i, j: (0,) * len(shape), pipeline_mode=pl.Buffered(1))
    return pl.pallas_call(
        kern,
        grid=(b, t // tm),
        in_specs=[x_spec,
                  pl.BlockSpec((1, N_MOD, d), lambda i, j: (i, 0, 0)),
                  const((1, d)), const((d, n)), const((n_gate, d))],
        out_specs=[pl.BlockSpec((1, tm, n), lambda i, j: (i, j, 0)),
                   pl.BlockSpec((1, n_gate, tm), lambda i, j: (i, 0, j))],
        out_shape=[jax.ShapeDtypeStruct((b, t, n), out_dtype),
                   jax.ShapeDtypeStruct((b, n_gate, t), F32)],
        scratch_shapes=[pltpu.VMEM((tm, d), BF16)],
        compiler_params=pltpu.CompilerParams(
            dimension_semantics=("arbitrary", "arbitrary"), vmem_limit_bytes=VMEM_LIMIT),
        name="in_projection",
    )(x, mod, norm1.reshape(1, d), w, w_gate_t)


def _iota2(n):
    return (lax.broadcasted_iota(jnp.int32, (n, n), 0), lax.broadcasted_iota(jnp.int32, (n, n), 1))


def _to_columns(x):
    ii, jj = _iota2(x.shape[1])
    eye = (ii == jj).astype(BF16)
    hi, mid, lo = _split3(x)
    f = lambda t: lax.dot_general(eye, t, NT_DIMS, preferred_element_type=F32)
    return f(hi) + f(mid) + f(lo)


def _masks(fwd, n):
    ii, jj = _iota2(n)
    if fwd:
        return ii >= jj, ii > jj
    return ii <= jj, ii < jj


def _row_parity(shape):
    return lax.broadcasted_iota(jnp.int32, shape, len(shape) - 2) % 2


def _cumulate_gate_rows(logdecay):
    n_chunks, _, n = logdecay.shape
    ii, jj = _iota2(n)
    prefix = (ii <= jj).astype(BF16)
    suffix = (ii >= jj).astype(BF16)
    ones = jnp.ones((n, n), BF16)
    flat = logdecay.reshape(n_chunks * GATE_ROWS, n)
    par = _row_parity(flat.shape)
    cum = jnp.where(par == 0, _dot_exact_rhs(flat, prefix), _dot_exact_rhs(flat, suffix))
    tot = _dot_exact_rhs(flat, ones)
    row = lax.broadcasted_iota(jnp.int32, flat.shape, 0) % GATE_ROWS
    return cum, tot, row


def _conv_block(u_ref, w, t0, rows, total):
    main = u_ref[0, pl.ds(t0, rows), :]
    lo = jnp.maximum(t0 - 8, 0)
    hi = jnp.minimum(t0 + rows, total - 8)
    prev = jnp.where(t0 > 0, u_ref[0, pl.ds(pl.multiple_of(lo, 8), 8), :], 0.0)
    nxt = jnp.where(t0 + rows < total, u_ref[0, pl.ds(pl.multiple_of(hi, 8), 8), :], 0.0)
    ext = jnp.concatenate([prev, main, nxt], axis=0)
    acc = None
    for j in range(CONV_K):
        off = 8 + j - CONV_K // 2
        term = ext[off:off + rows, :] * w[j:j + 1, :]
        acc = term if acc is None else acc + term
    return _silu(acc)


def _conv_block_inner(u_ref, w, t0, rows):
    acc = None
    for j in range(CONV_K):
        term = u_ref[0, pl.ds(t0 + (j - CONV_K // 2), rows), :] * w[j:j + 1, :]
        acc = term if acc is None else acc + term
    return _silu(acc)


def _l2norm(x):
    return x * lax.rsqrt(jnp.sum(x * x, axis=-1, keepdims=True) + NORM_EPS)


def _unit_tri_inverses(a_list, between_levels=()):
    n = a_list[0].shape[0]
    ii, jj = _iota2(n)
    eye = (ii == jj).astype(F32)
    pair = (ii >> 1) == (jj >> 1)
    ts = [eye - jnp.where(pair, a, 0.0) for a in a_list]
    for level in range(1, n.bit_length() - 1):
        same_big = (ii >> (level + 1)) == (jj >> (level + 1))
        same_small = (ii >> level) == (jj >> level)
        couple = same_big & jnp.logical_not(same_small)
        es = [jnp.where(couple, a, 0.0).astype(BF16) for a in a_list]
        tbs = [t.astype(BF16) for t in ts]
        tes = [jnp.dot(tb, e, preferred_element_type=F32) for tb, e in zip(tbs, es)]
        ts = [t - jnp.dot(te.astype(BF16), tb, preferred_element_type=F32)
              for t, te, tb in zip(ts, tes, tbs)]
        if level <= len(between_levels):
            between_levels[level - 1]()
    return ts


def _dn_intra_chunks(qkvx, with_out, between_levels=()):
    n = qkvx[0][0].shape[0]
    cols = [_to_columns(x) for _, _, _, x in qkvx]
    kks = [_dot_nt(k, k) for _, k, _, _ in qkvx]
    qks = [_dot_nt(q, k) if with_out else None for q, k, _, _ in qkvx]
    parts = []
    for (q, k, v, x), col, kk in zip(qkvx, cols, kks):
        for d in range(2):
            beta_c = col[:, d:d + 1]
            g_c = col[:, 2 + d:3 + d]
            tot_c = col[:, 4 + d:5 + d]
            g_r = x[2 + d:3 + d, :]
            incl, strict = _masks(d == 0, n)
            decay = jnp.exp(jnp.where(incl, g_c - g_r, NEG_BIG))
            e_g = jnp.exp(g_c)
            a = jnp.where(strict, kk * beta_c * decay, 0.0)
            rhs = jnp.concatenate([k * (beta_c * e_g), v * beta_c], axis=1).astype(BF16)
            parts.append((a, rhs, decay, e_g, tot_c - g_c))
    ts = _unit_tri_inverses([p[0] for p in parts], between_levels)
    wus = [jnp.dot(t.astype(BF16), p[1], preferred_element_type=F32).astype(BF16) for t, p in zip(ts, parts)]
    k_tail_ts = [(qkvx[i // 2][1] * jnp.exp(p[4])).T.astype(BF16) for i, p in enumerate(parts)]
    state_terms = [jnp.dot(kt, wu, preferred_element_type=F32) for kt, wu in zip(k_tail_ts, wus)]
    if with_out:
        scores = [(qks[i // 2] * p[2]).astype(BF16) for i, p in enumerate(parts)]
        out_terms = [jnp.dot(sc, wu, preferred_element_type=F32) for sc, wu in zip(scores, wus)]
    res = []
    for ci, (q, k, v, x) in enumerate(qkvx):
        per_dir = []
        for d in range(2):
            i = 2 * ci + d
            st = state_terms[i]
            if with_out:
                ot = out_terms[i]
                per_dir.append((-st[:, :HEAD_DIM], st[:, HEAD_DIM:],
                                q * parts[i][3] - ot[:, :HEAD_DIM], ot[:, HEAD_DIM:]))
            else:
                per_dir.append((-st[:, :HEAD_DIM], st[:, HEAD_DIM:], None, None))
        res.append(per_dir)
    return res


def _dn_kernel(sc_ref, ql_ref, kl_ref, vl_ref, gate_ref, qc_ref, kc_ref, vc_ref,
               gl_ref, gc_ref, wq_ref, wk_ref, wv_ref, nw_ref, y_ref,
               qs, ks, vs, rows_s, lhs_s, add_s, obuf, s_ref):
    head = pl.program_id(1)
    n = DN_CHUNK
    t_lat = ql_ref.shape[1]
    t_ctx = qc_ref.shape[1]
    nc_lat = t_lat // n
    nc_ctx = t_ctx // n
    half = nc_lat // 2
    conv_rows = 256
    ctx_group = 2 if nc_ctx % 2 == 0 else 1
    lat_group = 8 if nc_lat % 16 == 0 else 2

    a_log_f, a_log_b = sc_ref[head, 0], sc_ref[head, 1]
    dtb_f, dtb_b = sc_ref[head, 2], sc_ref[head, 3]

    def chunk_rows(c):
        return pl.ds(pl.multiple_of(c * n, n), n)

    def prep(src_refs, total):
        def block(t0, conv):
            q = _l2norm(conv(src_refs[0], wq_ref[...], t0))
            qs[pl.ds(t0, conv_rows), :] = q * (HEAD_DIM ** -0.5)
            ks[pl.ds(t0, conv_rows), :] = _l2norm(conv(src_refs[1], wk_ref[...], t0))
            vs[pl.ds(t0, conv_rows), :] = conv(src_refs[2], wv_ref[...], t0)

        edge = lambda u_ref, w, t0: _conv_block(u_ref, w, t0, conv_rows, total)
        inner = lambda u_ref, w, t0: _conv_block_inner(u_ref, w, t0, conv_rows)
        n_blocks = total // conv_rows
        block(0, edge)
        if n_blocks > 1:
            block((n_blocks - 1) * conv_rows, edge)

        def body(i, carry):
            block(pl.multiple_of(i * conv_rows, conv_rows), inner)
            return carry
        lax.fori_loop(1, n_blocks - 1, body, 0)

    def gate_rows(g_ref, n_chunks):
        beta_raw = g_ref[0, 0, 0]
        alpha_raw = g_ref[0, 0, 1]
        par = _row_parity(alpha_raw.shape)
        a_vec = jnp.exp(jnp.where(par == 0, a_log_f, a_log_b))
        dtb = jnp.where(par == 0, dtb_f, dtb_b)
        cum, tot, row = _cumulate_gate_rows(-a_vec * _softplus(alpha_raw + dtb))
        beta = _sigmoid(beta_raw).reshape(cum.shape)
        packed = jnp.where(row < 2, beta, jnp.where(row < 4, cum, tot))
        rows_s[0:n_chunks] = packed.reshape(n_chunks, GATE_ROWS, n)

    def intra_group(cs, with_out, between_levels=()):
        res = _dn_intra_chunks([(qs[chunk_rows(c), :], ks[chunk_rows(c), :], vs[chunk_rows(c), :], rows_s[c])
                                for c in cs], with_out, between_levels)
        for c, per_dir in zip(cs, res):
            for d in range(2):
                s_mul, s_add, o_mul, o_add = per_dir[d]
                lhs_s[d, c, 0:HEAD_DIM, :] = s_mul.astype(BF16)
                add_s[d, c, 0:HEAD_DIM, :] = s_add
                if with_out:
                    lhs_s[d, c, HEAD_DIM:, :] = o_mul.astype(BF16)
                    add_s[d, c, HEAD_DIM:, :] = o_add

    def state_steps(cf, cb, with_out):
        dc = ((0, cf), (1, cb))
        rows = slice(None) if with_out else slice(0, HEAD_DIM)
        ss = [s_ref[d] for d, _ in dc]
        rs = [jnp.dot(lhs_s[d, c, rows, :], s.astype(BF16), preferred_element_type=F32) + add_s[d, c, rows, :]
              for (d, c), s in zip(dc, ss)]
        for (d, c), s, r in zip(dc, ss, rs):
            s_ref[d] = s * jnp.exp(rows_s[c][4 + d:5 + d, :]) + r[:HEAD_DIM]
        return [r[HEAD_DIM:] if with_out else None for r in rs]

    def finalize(o, r):
        ms = jnp.mean(o * o, axis=-1, keepdims=True)
        return o * lax.rsqrt(ms + NORM_EPS) * nw_ref[...] * _silu(gate_ref[0, r, :])

    s_ref[...] = jnp.zeros_like(s_ref)
    prep((qc_ref, kc_ref, vc_ref), t_ctx)
    gate_rows(gc_ref, nc_ctx)
    for g in range(nc_ctx // ctx_group):
        intra_group([g * ctx_group + j for j in range(ctx_group)], False)

    def ctx_body(i, carry):
        state_steps(i, nc_ctx - 1 - i, False)
        return carry
    lax.fori_loop(0, nc_ctx, ctx_body, 0)

    prep((ql_ref, kl_ref, vl_ref), t_lat)
    gate_rows(gl_ref, nc_lat)
    def first_visit(i):
        cb = nc_lat - 1 - i
        o_f, o_b = state_steps(i, cb, True)
        obuf[chunk_rows(i), :] = o_f
        obuf[chunk_rows(cb), :] = o_b

    side = lat_group // 2
    n_groups = half // side

    def group_chunks(g):
        return [g * side + j for j in range(side)] + [nc_lat - 1 - g * side - j for j in range(side)]

    def group_steps(g):
        return [functools.partial(first_visit, g * side + j) for j in range(side)]

    intra_group(group_chunks(0), True)

    def group_body(g, carry):
        intra_group(group_chunks(g), True, group_steps(g - 1))
        return carry
    lax.fori_loop(1, n_groups, group_body, 0)
    for step in group_steps(n_groups - 1):
        step()

    def second_body(i, carry):
        cb = nc_lat - 1 - i
        o_f, o_b = state_steps(i, cb, True)
        for c, o in ((i, o_f), (cb, o_b)):
            r = chunk_rows(c)
            y_ref[0, r, :] = finalize(o + obuf[r, :], r).astype(y_ref.dtype)
        return carry

    lax.fori_loop(half, nc_lat, second_body, 0)


def _deltanet(p_lat, p_ctx, g_lat, g_ctx, scalars, dn_conv, dn_norm):
    b, t_lat, _ = p_lat.shape
    t_ctx = p_ctx.shape[1]
    n = DN_CHUNK
    nc_lat, nc_ctx = t_lat // n, t_ctx // n
    assert n == HEAD_DIM and nc_lat % 2 == 0 and t_lat % 256 == 0 and t_ctx % 256 == 0 and t_ctx <= t_lat
    h = N_HEADS
    col = lambda off: (lambda i, j: (i, 0, off + j))
    lat_spec = lambda off: pl.BlockSpec((1, t_lat, HEAD_DIM), col(off))
    ctx_spec = lambda off: pl.BlockSpec((1, t_ctx, HEAD_DIM), col(off))
    conv_spec = lambda off: pl.BlockSpec((CONV_K, HEAD_DIM), lambda i, j: (0, off + j))
    gate_spec = lambda nc: pl.BlockSpec((1, 1, 2, nc, GATE_ROWS, n), lambda i, j: (i, j, 0, 0, 0, 0))
    seq = lambda dt: pltpu.VMEM((t_lat, HEAD_DIM), dt)
    step_terms = lambda dt: pltpu.VMEM((2, nc_lat, HEAD_DIM + n, HEAD_DIM), dt)
    return pl.pallas_call(
        _dn_kernel,
        grid=(b, h),
        in_specs=[pl.BlockSpec(memory_space=pltpu.SMEM),
                  lat_spec(0), lat_spec(h), lat_spec(2 * h), lat_spec(3 * h),
                  ctx_spec(0), ctx_spec(h), ctx_spec(2 * h),
                  gate_spec(nc_lat), gate_spec(nc_ctx),
                  conv_spec(0), conv_spec(h), conv_spec(2 * h),
                  pl.BlockSpec((1, HEAD_DIM), lambda i, j: (0, 0))],
        out_specs=pl.BlockSpec((1, t_lat, HEAD_DIM), lambda i, j: (i, 0, j)),
        out_shape=jax.ShapeDtypeStruct((b, t_lat, h * HEAD_DIM), BF16),
        scratch_shapes=[seq(F32), seq(F32), seq(F32),
                        pltpu.VMEM((nc_lat, GATE_ROWS, n), F32),
                        step_terms(BF16), step_terms(F32),
                        seq(F32),
                        pltpu.VMEM((2, HEAD_DIM, HEAD_DIM), F32)],
        compiler_params=pltpu.CompilerParams(
            dimension_semantics=("arbitrary", "arbitrary"), vmem_limit_bytes=VMEM_LIMIT),
        name="deltanet_scan",
    )(scalars, p_lat, p_lat, p_lat, p_lat, p_ctx, p_ctx, p_ctx, g_lat, g_ctx,
      dn_conv, dn_conv, dn_conv, dn_norm.reshape(1, HEAD_DIM))


def _ml_group(problems, states, with_out):
    n = problems[0][1].shape[0]
    ii, jj = _iota2(n)
    eye = ii == jj
    ones_blk = jnp.ones((n, HEAD_DIM), BF16)

    def lane_spread(rows):
        splits = [_split2(jnp.where(eye, row, 0.0)) for row in rows]
        return [sum(jnp.dot(t, ones_blk, preferred_element_type=F32) for t in sp) for sp in splits]

    a_rows = [x[d:d + 1, :] for d, _, _, _, x in problems]
    a_spreads = lane_spread(a_rows)
    b_spreads = lane_spread([x[2 + d:3 + d, :] for d, _, _, _, x in problems]) if with_out else None
    pre = []
    for (d, q, k, v, x), a_r, a_s in zip(problems, a_rows, a_spreads):
        amax = jnp.max(a_r, axis=1, keepdims=True)
        v_ext = jnp.concatenate([v.astype(BF16), ones_blk], axis=1)
        kw = (k * jnp.exp(a_s - amax)).astype(BF16)
        pre.append((amax, v_ext, kw))
    ups = [lax.dot_general(kw, v_ext, TN_DIMS, preferred_element_type=F32) for _, v_ext, kw in pre]
    if with_out:
        cm_rows = [jnp.max(jnp.where(_masks(d != 0, n)[0], a_s[:, :n], NEG_BIG), axis=0, keepdims=True)
                   for (d, _, _, _, _), a_s in zip(problems, a_spreads)]
        cm_spreads = lane_spread(cm_rows)
        qks = [_dot_nt(q, k) for _, q, k, _, _ in problems]
        scores = []
        for (d, _, _, _, _), a_r, cm_s, qk in zip(problems, a_rows, cm_spreads, qks):
            incl, _ = _masks(d == 0, n)
            expo = jnp.where(incl, jnp.broadcast_to(a_r, (n, n)) - cm_s[:, :n], NEG_BIG)
            scores.append((qk * jnp.exp(expo)).astype(BF16))
        intra = [jnp.dot(s, v_ext, preferred_element_type=F32) for s, (_, v_ext, _) in zip(scores, pre)]
    states = list(states)
    starts = []
    for (d, _, _, _, x), (amax, _, _), up in zip(problems, pre, ups):
        c_ext, m = states[d]
        starts.append((c_ext, m))
        mx = jnp.maximum(m, amax)
        states[d] = (jnp.exp(m - mx) * c_ext + jnp.exp(amax - mx) * up, x[4 + d:5 + d, 0:1] + mx)
    if not with_out:
        return states, [None] * len(problems)
    inter = [_dot(q, c_ext) for (_, q, _, _, _), (c_ext, _) in zip(problems, starts)]
    hs = []
    for (_, m), cm_s, b_s, qc, sv in zip(starts, cm_spreads, b_spreads, inter, intra):
        mm = jnp.maximum(m, cm_s)
        w_inter = jnp.exp(m - mm)
        w_intra = jnp.exp(cm_s - mm)
        num = w_inter * qc[:, :HEAD_DIM] + w_intra * sv[:, :HEAD_DIM]
        den = w_inter * qc[:, HEAD_DIM:] + w_intra * sv[:, HEAD_DIM:]
        hs.append(num / jnp.maximum(jnp.abs(den), jnp.exp(-(b_s + mm))))
    return states, hs


def _ml_kernel(sc_ref, ql_ref, kl_ref, vl_ref, og_ref, qc_ref, kc_ref, vc_ref,
               gl_ref, gc_ref, nw_ref, y_ref, rl, rc, obuf, c_ref, m_ref):
    head = pl.program_id(1)
    t_lat = ql_ref.shape[1]
    t_ctx = qc_ref.shape[1]
    nc_lat = t_lat // ML_CHUNK
    nc_ctx = t_ctx // ML_CHUNK
    half = nc_lat // 2
    k_scale = HEAD_DIM ** -0.5

    igb_f, igb_b = sc_ref[head, 0], sc_ref[head, 1]
    fgb_f, fgb_b = sc_ref[head, 2], sc_ref[head, 3]

    def gate_rows(g_ref, dst, n_chunks):
        ig_raw = g_ref[0, 0, 0]
        fg_raw = g_ref[0, 0, 1]
        par = _row_parity(fg_raw.shape)
        lf = -_softplus(-(fg_raw + jnp.where(par == 0, fgb_f, fgb_b)))
        cum, tot, row = _cumulate_gate_rows(lf)
        ic = (ig_raw + jnp.where(par == 0, igb_f, igb_b)).reshape(n_chunks * GATE_ROWS, ML_CHUNK)
        packed = jnp.where(row < 2, ic - cum, jnp.where(row < 4, cum, tot))
        dst[...] = packed.reshape(n_chunks, GATE_ROWS, ML_CHUNK)

    gate_rows(gl_ref, rl, nc_lat)
    gate_rows(gc_ref, rc, nc_ctx)

    c_ref[...] = jnp.zeros_like(c_ref)
    m_ref[...] = jnp.zeros_like(m_ref)

    def run_steps(first_step, n_steps, total, load, gates, with_out):
        problems = []
        for j in range(n_steps):
            for d, c in ((0, first_step + j), (1, total - 1 - first_step - j)):
                problems.append((d, load(0, c), load(1, c).astype(F32) * k_scale, load(2, c), gates[c]))
        states = [(c_ref[d], m_ref[d, 0:1, 0:1]) for d in range(2)]
        states, hs = _ml_group(problems, states, with_out)
        for d, (c_ext, m) in enumerate(states):
            c_ref[d] = c_ext
            m_ref[d] = jnp.broadcast_to(m, m_ref.shape[1:])
        return hs

    ctx_refs = (qc_ref, kc_ref, vc_ref)
    lat_refs = (ql_ref, kl_ref, vl_ref)

    def chunk_rows(c):
        return pl.ds(pl.multiple_of(c * ML_CHUNK, ML_CHUNK), ML_CHUNK)

    def ctx_load(which, c):
        return ctx_refs[which][0, chunk_rows(c), :]

    def lat_load(which, c):
        return lat_refs[which][0, chunk_rows(c), :]

    ctx_unroll = ML_UNROLL if nc_ctx % ML_UNROLL == 0 else 1

    def ctx_body(i, carry):
        run_steps(i * ctx_unroll, ctx_unroll, nc_ctx, ctx_load, rc, False)
        return carry

    lax.fori_loop(0, nc_ctx // ctx_unroll, ctx_body, 0)

    def finalize(hh, r):
        ms = jnp.mean(hh * hh, axis=-1, keepdims=True)
        y = hh * lax.rsqrt(ms + NORM_EPS) * nw_ref[...]
        return y * _sigmoid(og_ref[0, r, :].astype(F32))

    def lat_body(i, second):
        first_step = i * ML_UNROLL
        hs = run_steps(first_step, ML_UNROLL, nc_lat, lat_load, rl, True)
        for j in range(ML_UNROLL):
            for d, c in ((0, first_step + j), (1, nc_lat - 1 - first_step - j)):
                hh = hs[2 * j + d]
                r = chunk_rows(c)
                if second:
                    y_ref[0, r, :] = finalize(hh + obuf[r, :], r).astype(y_ref.dtype)
                else:
                    obuf[r, :] = hh

    def first_body(i, carry):
        lat_body(i, False)
        return carry

    def second_body(i, carry):
        lat_body(i, True)
        return carry

    lax.fori_loop(0, half // ML_UNROLL, first_body, 0)
    lax.fori_loop(half // ML_UNROLL, nc_lat // ML_UNROLL, second_body, 0)


def _mlstm(p_lat, p_ctx, g_lat, g_ctx, scalars, ml_norm):
    b, t_lat, _ = p_lat.shape
    t_ctx = p_ctx.shape[1]
    nc_lat, nc_ctx = t_lat // ML_CHUNK, t_ctx // ML_CHUNK
    assert nc_lat % (2 * ML_UNROLL) == 0 and t_lat == ML_CHUNK * GRID_W
    h = N_HEADS
    lat_spec = lambda off: pl.BlockSpec((1, t_lat, HEAD_DIM), lambda i, j: (i, 0, off + j))
    ctx_spec = lambda off: pl.BlockSpec((1, t_ctx, HEAD_DIM), lambda i, j: (i, 0, 4 * h + off + j))
    gate_spec = lambda nc: pl.BlockSpec((1, 1, 2, nc, GATE_ROWS, ML_CHUNK), lambda i, j: (i, j, 0, 0, 0, 0))
    return pl.pallas_call(
        _ml_kernel,
        grid=(b, h),
        in_specs=[pl.BlockSpec(memory_space=pltpu.SMEM),
                  lat_spec(0), lat_spec(h), lat_spec(2 * h), lat_spec(3 * h),
                  ctx_spec(0), ctx_spec(h), ctx_spec(2 * h),
                  gate_spec(nc_lat), gate_spec(nc_ctx),
                  pl.BlockSpec((1, HEAD_DIM), lambda i, j: (0, j))],
        out_specs=pl.BlockSpec((1, t_lat, HEAD_DIM), lambda i, j: (i, 0, j)),
        out_shape=jax.ShapeDtypeStruct((b, t_lat, h * HEAD_DIM), BF16),
        scratch_shapes=[pltpu.VMEM((nc_lat, GATE_ROWS, ML_CHUNK), F32), pltpu.VMEM((nc_ctx, GATE_ROWS, ML_CHUNK), F32),
                        pltpu.VMEM((t_lat, HEAD_DIM), F32), pltpu.VMEM((2, HEAD_DIM, 2 * HEAD_DIM), F32),
                        pltpu.VMEM((2, 8, HEAD_DIM), F32)],
        compiler_params=pltpu.CompilerParams(
            dimension_semantics=("arbitrary", "arbitrary"), vmem_limit_bytes=VMEM_LIMIT),
        name="mlstm_scan",
    )(scalars, p_lat, p_lat, p_lat, p_lat, p_ctx, p_ctx, p_ctx, g_lat, g_ctx,
      ml_norm.reshape(1, h * HEAD_DIM))


def _ffn_kernel(x_ref, ydn_ref, yml_ref, mod_ref, n2_ref, fn_ref, wo_dn_ref, wo_ml_ref,
                wg_ref, wu_ref, wd_ref, o_ref, h_ref, acc_ref, ydn_s, *, f_tile):
    rows = x_ref.shape[1]
    d = acc_ref.shape[1]
    dm = ydn_s.shape[1]
    n_cols = acc_ref.shape[0] // rows
    for c in range(n_cols):
        acc_ref[c * rows:(c + 1) * rows, :] = x_ref[0, :, c * d:(c + 1) * d]
        ydn_s[c * rows:(c + 1) * rows, :] = ydn_ref[0, :, c * dm:(c + 1) * dm]
    mix = (jnp.dot(ydn_s[...], wo_dn_ref[...], preferred_element_type=F32)
           + jnp.dot(yml_ref[0], wo_ml_ref[...], preferred_element_type=F32))
    x1 = acc_ref[...] + mod_ref[0, 2:3, :] * mix
    h_ref[...] = _rms_mod(x1, n2_ref[...], mod_ref[0, 4:5, :], mod_ref[0, 3:4, :]).astype(BF16)
    acc_ref[...] = x1
    g2 = mod_ref[0, 5:6, :]
    d_ff = wg_ref.shape[1]
    for j in range(d_ff // f_tile):
        sl = slice(j * f_tile, (j + 1) * f_tile)
        gate = jnp.dot(h_ref[...], wg_ref[:, sl], preferred_element_type=F32)
        up = jnp.dot(h_ref[...], wu_ref[:, sl], preferred_element_type=F32)
        act = (_silu(gate) * up).astype(BF16)
        acc_ref[...] += g2 * jnp.dot(act, wd_ref[sl, :], preferred_element_type=F32)
    for c in range(n_cols):
        x2 = acc_ref[c * rows:(c + 1) * rows, :]
        ms = jnp.mean(x2 * x2, axis=-1, keepdims=True)
        o_ref[0, :, c * d:(c + 1) * d] = x2 * lax.rsqrt(ms + NORM_EPS) * fn_ref[...]


def _out_ffn(x, y_dn, y_ml, mod, norm2, final_norm, wo_dn, wo_ml, w_gate, w_up, w_down, tm):
    b, t, d = x.shape
    d_mix = y_dn.shape[2]
    d_ff = w_gate.shape[1]
    rows = t // GRID_W
    n_cols = tm // rows
    assert tm % rows == 0 and t % tm == 0
    const = lambda shape: pl.BlockSpec(shape, lambda i, j: (0,) * len(shape),
                                       pipeline_mode=pl.Buffered(1))
    grid_cols = lambda width: pl.BlockSpec((1, rows, n_cols * width), lambda i, j: (i, 0, j))
    kern = functools.partial(_ffn_kernel, f_tile=256)
    out = pl.pallas_call(
        kern,
        grid=(b, t // tm),
        in_specs=[grid_cols(d), grid_cols(d_mix),
                  pl.BlockSpec((1, tm, d_mix), lambda i, j: (i, j, 0)),
                  pl.BlockSpec((1, N_MOD, d), lambda i, j: (i, 0, 0)),
                  const((1, d)), const((1, d)),
                  const((d_mix, d)), const((d_mix, d)),
                  const((d, d_ff)), const((d, d_ff)), const((d_ff, d))],
        out_specs=grid_cols(d),
        out_shape=jax.ShapeDtypeStruct((b, rows, GRID_W * d), F32),
        scratch_shapes=[pltpu.VMEM((tm, d), BF16), pltpu.VMEM((tm, d), F32), pltpu.VMEM((tm, d_mix), BF16)],
        compiler_params=pltpu.CompilerParams(
            dimension_semantics=("arbitrary", "arbitrary"), vmem_limit_bytes=VMEM_LIMIT),
        name="out_ffn",
    )(x.reshape(b, rows, GRID_W * d), y_dn.reshape(b, rows, GRID_W * d_mix), y_ml, mod,
      norm2.reshape(1, d), final_norm.reshape(1, d), wo_dn, wo_ml, w_gate, w_up, w_down)
    return out.reshape(b, t, d)


def _gate_weight_rows(w_in, d_group):
    h = N_HEADS
    cols = []
    for mixer in range(2):
        base = mixer * (4 * d_group + 4 * h) + 4 * d_group
        for head in range(h):
            for slab in range(2):
                pair = [base + slab * 2 * h + head, base + slab * 2 * h + h + head]
                cols += pair * (GATE_ROWS // 2)
    return w_in[:, jnp.array(cols)].T


def _chunk_major_gates(gt, chunk):
    b, _, t = gt.shape
    g = gt.reshape(b, N_HEADS, 2, GATE_ROWS, t // chunk, chunk)
    return g.transpose(0, 1, 2, 4, 3, 5)


def kernel(x, c, ctx, c_ctx, w_mod, b_mod, norm1, w_in, dn_conv, dn_a_log, dn_dt_bias, dn_norm,
           ml_ig_bias, ml_fg_bias, ml_norm, w_out, norm2, w_ffn_in, w_ffn_out, final_norm):
    depth = w_mod.shape[0]
    assert depth == 1, "context outputs are only skipped for a single layer"
    b, t_lat, d = x.shape
    h = N_HEADS
    d_group = h * HEAD_DIM
    d_ff = w_ffn_out.shape[1]
    layer = 0

    pad_rows = -(b + 1) % 8
    cc = jnp.concatenate([c, c_ctx[None, :], jnp.zeros((pad_rows, d), F32)], axis=0)
    mod = _modulation(cc, w_mod[layer], b_mod[layer])
    mod_lat = mod[:, :b].transpose(1, 0, 2)
    mod_ctx = jnp.broadcast_to(mod[:, b][None], (b, N_MOD, d))

    w = w_in[layer]
    dn_cols = 4 * d_group + 4 * h
    w_dn = w[:, :4 * d_group].astype(BF16)
    w_ml = w[:, dn_cols:dn_cols + 4 * d_group].astype(BF16)
    w_gate_t = _gate_weight_rows(w, d_group).astype(BF16)
    dn_gate_rows = h * 2 * GATE_ROWS
    pdn_lat, gdn_lat = _in_projection(x, mod_lat, norm1[layer], w_dn, w_gate_t[:dn_gate_rows],
                                      tm=512, col_tile=False, out_dtype=F32)
    pml_lat, gml_lat = _in_projection(x, mod_lat, norm1[layer], w_ml, w_gate_t[dn_gate_rows:],
                                      tm=512, col_tile=True, out_dtype=BF16)
    p_ctx, gt_ctx = _in_projection(ctx, mod_ctx, norm1[layer], jnp.concatenate([w_dn, w_ml], axis=1), w_gate_t,
                                   tm=ctx.shape[1], col_tile=False, out_dtype=F32)

    g_dn_lat = _chunk_major_gates(gdn_lat, DN_CHUNK)
    g_dn_ctx = _chunk_major_gates(gt_ctx[:, :dn_gate_rows], DN_CHUNK)
    g_ml_lat = _chunk_major_gates(gml_lat, ML_CHUNK)
    g_ml_ctx = _chunk_major_gates(gt_ctx[:, dn_gate_rows:], ML_CHUNK)

    dn_scal = jnp.concatenate([dn_a_log[layer].T, dn_dt_bias[layer].T], axis=1)
    ml_scal = jnp.concatenate([ml_ig_bias[layer].T, ml_fg_bias[layer].T], axis=1)

    y_dn = _deltanet(pdn_lat, p_ctx, g_dn_lat, g_dn_ctx, dn_scal, dn_conv[layer], dn_norm[layer])
    y_ml = _mlstm(pml_lat, p_ctx, g_ml_lat, g_ml_ctx, ml_scal, ml_norm[layer])

    wo = w_out[layer].astype(BF16)
    wf = w_ffn_in[layer].astype(BF16)
    return _out_ffn(x, y_dn, y_ml, mod_lat, norm2[layer], final_norm,
                    wo[:d_group], wo[d_group:], wf[:, :d_ff], wf[:, d_ff:],
                    w_ffn_out[layer].astype(BF16), tm=512)
```

```python
import functools

import jax
import jax.numpy as jnp
from jax import lax
from jax.experimental import pallas as pl
from jax.experimental.pallas import tpu as pltpu

F32 = jnp.float32
BF16 = jnp.bfloat16

DN_CHUNK = 128
ML_CHUNK = 64
ML_UNROLL = 4
GRID_W = 64
HEAD_DIM = 128
N_HEADS = 4
CONV_K = 5
NORM_EPS = 1e-6
N_MOD = 6
GATE_ROWS = 8
NEG_BIG = -1e30
VMEM_LIMIT = 56 * 1024 * 1024

NT_DIMS = (((1,), (1,)), ((), ()))
TN_DIMS = (((0,), (0,)), ((), ()))


def _dot(a, b):
    return jnp.dot(a.astype(BF16), b.astype(BF16), preferred_element_type=F32)


def _dot_nt(a, b):
    return lax.dot_general(a.astype(BF16), b.astype(BF16), NT_DIMS, preferred_element_type=F32)


def _split3(x):
    hi = x.astype(BF16)
    r1 = x - hi.astype(F32)
    mid = r1.astype(BF16)
    lo = (r1 - mid.astype(F32)).astype(BF16)
    return hi, mid, lo


def _split2(x):
    hi = x.astype(BF16)
    return hi, (x - hi.astype(F32)).astype(BF16)


def _dot_exact_rhs(x, m_bf16):
    hi, mid, lo = _split3(x)
    f = lambda t: jnp.dot(t, m_bf16, preferred_element_type=F32)
    return f(hi) + f(mid) + f(lo)


def _softplus(x):
    return jnp.maximum(x, 0.0) + jnp.log(1.0 + jnp.exp(-jnp.abs(x)))


def _sigmoid(x):
    return 1.0 / (1.0 + jnp.exp(-x))


def _silu(x):
    return x * _sigmoid(x)


def _mod_kernel(c_ref, w_ref, b_ref, o_ref):
    sc = _silu(c_ref[...])
    o_ref[0] = jnp.dot(sc, w_ref[...], preferred_element_type=F32,
                       precision=lax.Precision.HIGHEST) + b_ref[0]


def _modulation(cc, w_mod, b_mod):
    rows, d = cc.shape
    return pl.pallas_call(
        _mod_kernel,
        grid=(N_MOD,),
        in_specs=[pl.BlockSpec((rows, d), lambda j: (0, 0)),
                  pl.BlockSpec((d, d), lambda j: (0, j)),
                  pl.BlockSpec((1, 1, d), lambda j: (j, 0, 0))],
        out_specs=pl.BlockSpec((1, rows, d), lambda j: (j, 0, 0)),
        out_shape=jax.ShapeDtypeStruct((N_MOD, rows, d), F32),
        compiler_params=pltpu.CompilerParams(vmem_limit_bytes=VMEM_LIMIT),
        name="modulation",
    )(cc, w_mod, b_mod.reshape(N_MOD, 1, d))


def _rms_mod(x, gain, scale, shift):
    ms = jnp.mean(x * x, axis=-1, keepdims=True)
    return (x * lax.rsqrt(ms + NORM_EPS) * gain) * (1.0 + scale) + shift


def _inproj_kernel(x_ref, mod_ref, n1_ref, wdn_ref, wml_ref, wg_ref, pdn_ref, pml_ref, gt_ref,
                   h_ref, hcm_ref, *, n_tile, col_major):
    h = _rms_mod(x_ref[0], n1_ref[...], mod_ref[0, 1:2, :], mod_ref[0, 0:1, :])
    h_ref[...] = h.astype(BF16)
    for j in range(wdn_ref.shape[1] // n_tile):
        cols = slice(j * n_tile, (j + 1) * n_tile)
        pdn_ref[0, :, cols] = jnp.dot(h_ref[...], wdn_ref[:, cols], preferred_element_type=F32)
    if col_major:
        tm, d = h.shape
        hcm_ref[...] = jnp.swapaxes(h.reshape(tm // GRID_W, GRID_W, d), 0, 1).reshape(tm, d).astype(BF16)
    lhs_ref = hcm_ref if col_major else h_ref
    for j in range(wml_ref.shape[1] // n_tile):
        cols = slice(j * n_tile, (j + 1) * n_tile)
        res = jnp.dot(lhs_ref[...], wml_ref[:, cols], preferred_element_type=F32)
        if col_major:
            pml_ref[0, :, :, cols] = res.reshape(pml_ref.shape[1], pml_ref.shape[2], n_tile)
        else:
            pml_ref[0, :, cols] = res
    gt_ref[0] = lax.dot_general(wg_ref[...], h_ref[...], NT_DIMS, preferred_element_type=F32)


def _in_projection(x, mod, norm1, w_dn, w_ml, w_gate_t, tm, col_major):
    b, t, d = x.shape
    n_dn, n_ml = w_dn.shape[1], w_ml.shape[1]
    n_gate = w_gate_t.shape[0]
    kern = functools.partial(_inproj_kernel, n_tile=512, col_major=col_major)
    if col_major:
        assert tm % GRID_W == 0 and (tm // GRID_W) % 8 == 0 and t % tm == 0
        rows = t // GRID_W
        ml_spec = pl.BlockSpec((1, GRID_W, tm // GRID_W, n_ml), lambda i, j: (i, 0, j, 0))
        ml_shape = jax.ShapeDtypeStruct((b, GRID_W, rows, n_ml), F32)
    else:
        ml_spec = pl.BlockSpec((1, tm, n_ml), lambda i, j: (i, j, 0))
        ml_shape = jax.ShapeDtypeStruct((b, t, n_ml), F32)
    const = lambda shape: pl.BlockSpec(shape, lambda i, j: (0,) * len(shape), pipeline_mode=pl.Buffered(1))
    return pl.pallas_call(
        kern,
        grid=(b, t // tm),
        in_specs=[pl.BlockSpec((1, tm, d), lambda i, j: (i, j, 0)),
                  pl.BlockSpec((1, N_MOD, d), lambda i, j: (i, 0, 0)),
                  const((1, d)), const((d, n_dn)), const((d, n_ml)), const((n_gate, d))],
        out_specs=[pl.BlockSpec((1, tm, n_dn), lambda i, j: (i, j, 0)),
                   ml_spec,
                   pl.BlockSpec((1, n_gate, tm), lambda i, j: (i, 0, j))],
        out_shape=[jax.ShapeDtypeStruct((b, t, n_dn), F32),
                   ml_shape,
                   jax.ShapeDtypeStruct((b, n_gate, t), F32)],
        scratch_shapes=[pltpu.VMEM((tm, d), BF16), pltpu.VMEM((tm, d), BF16)],
        compiler_params=pltpu.CompilerParams(
            dimension_semantics=("arbitrary", "arbitrary"), vmem_limit_bytes=VMEM_LIMIT),
        name="in_projection",
    )(x, mod, norm1.reshape(1, d), w_dn, w_ml, w_gate_t)


def _iota2(n):
    return (lax.broadcasted_iota(jnp.int32, (n, n), 0), lax.broadcasted_iota(jnp.int32, (n, n), 1))


def _to_columns(x):
    ii, jj = _iota2(x.shape[1])
    eye = (ii == jj).astype(BF16)
    hi, mid, lo = _split3(x)
    f = lambda t: lax.dot_general(eye, t, NT_DIMS, preferred_element_type=F32)
    return f(hi) + f(mid) + f(lo)


def _masks(fwd, n):
    ii, jj = _iota2(n)
    if fwd:
        return ii >= jj, ii > jj
    return ii <= jj, ii < jj


def _row_parity(shape):
    return lax.broadcasted_iota(jnp.int32, shape, len(shape) - 2) % 2


def _cumulate_gate_rows(logdecay):
    n_chunks, _, n = logdecay.shape
    ii, jj = _iota2(n)
    prefix = (ii <= jj).astype(BF16)
    suffix = (ii >= jj).astype(BF16)
    ones = jnp.ones((n, n), BF16)
    flat = logdecay.reshape(n_chunks * GATE_ROWS, n)
    par = _row_parity(flat.shape)
    cum = jnp.where(par == 0, _dot_exact_rhs(flat, prefix), _dot_exact_rhs(flat, suffix))
    tot = _dot_exact_rhs(flat, ones)
    row = lax.broadcasted_iota(jnp.int32, flat.shape, 0) % GATE_ROWS
    return cum, tot, row


def _conv_block(u_ref, w, t0, rows, total):
    main = u_ref[0, pl.ds(t0, rows), :]
    lo = jnp.maximum(t0 - 8, 0)
    hi = jnp.minimum(t0 + rows, total - 8)
    prev = jnp.where(t0 > 0, u_ref[0, pl.ds(pl.multiple_of(lo, 8), 8), :], 0.0)
    nxt = jnp.where(t0 + rows < total, u_ref[0, pl.ds(pl.multiple_of(hi, 8), 8), :], 0.0)
    ext = jnp.concatenate([prev, main, nxt], axis=0)
    acc = None
    for j in range(CONV_K):
        off = 8 + j - CONV_K // 2
        term = ext[off:off + rows, :] * w[j:j + 1, :]
        acc = term if acc is None else acc + term
    return _silu(acc)


def _conv_block_inner(u_ref, w, t0, rows):
    acc = None
    for j in range(CONV_K):
        term = u_ref[0, pl.ds(t0 + (j - CONV_K // 2), rows), :] * w[j:j + 1, :]
        acc = term if acc is None else acc + term
    return _silu(acc)


def _l2norm(x):
    return x * lax.rsqrt(jnp.sum(x * x, axis=-1, keepdims=True) + NORM_EPS)


def _unit_tri_inverses(a_list, between_levels=()):
    n = a_list[0].shape[0]
    ii, jj = _iota2(n)
    eye = (ii == jj).astype(F32)
    pair = (ii >> 1) == (jj >> 1)
    ts = [eye - jnp.where(pair, a, 0.0) for a in a_list]
    for level in range(1, n.bit_length() - 1):
        same_big = (ii >> (level + 1)) == (jj >> (level + 1))
        same_small = (ii >> level) == (jj >> level)
        couple = same_big & jnp.logical_not(same_small)
        es = [jnp.where(couple, a, 0.0).astype(BF16) for a in a_list]
        tbs = [t.astype(BF16) for t in ts]
        tes = [jnp.dot(tb, e, preferred_element_type=F32) for tb, e in zip(tbs, es)]
        ts = [t - jnp.dot(te.astype(BF16), tb, preferred_element_type=F32)
              for t, te, tb in zip(ts, tes, tbs)]
        if level <= len(between_levels):
            between_levels[level - 1]()
    return ts


def _dn_intra_chunks(qkvx, with_out, between_levels=()):
    n = qkvx[0][0].shape[0]
    cols = [_to_columns(x) for _, _, _, x in qkvx]
    kks = [_dot_nt(k, k) for _, k, _, _ in qkvx]
    qks = [_dot_nt(q, k) if with_out else None for q, k, _, _ in qkvx]
    parts = []
    for (q, k, v, x), col, kk in zip(qkvx, cols, kks):
        for d in range(2):
            beta_c = col[:, d:d + 1]
            g_c = col[:, 2 + d:3 + d]
            tot_c = col[:, 4 + d:5 + d]
            g_r = x[2 + d:3 + d, :]
            incl, strict = _masks(d == 0, n)
            decay = jnp.exp(jnp.where(incl, g_c - g_r, NEG_BIG))
            e_g = jnp.exp(g_c)
            a = jnp.where(strict, kk * beta_c * decay, 0.0)
            rhs = jnp.concatenate([k * (beta_c * e_g), v * beta_c], axis=1).astype(BF16)
            parts.append((a, rhs, decay, e_g, tot_c - g_c))
    ts = _unit_tri_inverses([p[0] for p in parts], between_levels)
    wus = [jnp.dot(t.astype(BF16), p[1], preferred_element_type=F32).astype(BF16) for t, p in zip(ts, parts)]
    k_tail_ts = [(qkvx[i // 2][1] * jnp.exp(p[4])).T.astype(BF16) for i, p in enumerate(parts)]
    state_terms = [jnp.dot(kt, wu, preferred_element_type=F32) for kt, wu in zip(k_tail_ts, wus)]
    if with_out:
        scores = [(qks[i // 2] * p[2]).astype(BF16) for i, p in enumerate(parts)]
        out_terms = [jnp.dot(sc, wu, preferred_element_type=F32) for sc, wu in zip(scores, wus)]
    res = []
    for ci, (q, k, v, x) in enumerate(qkvx):
        per_dir = []
        for d in range(2):
            i = 2 * ci + d
            st = state_terms[i]
            if with_out:
                ot = out_terms[i]
                per_dir.append((-st[:, :HEAD_DIM], st[:, HEAD_DIM:],
                                q * parts[i][3] - ot[:, :HEAD_DIM], ot[:, HEAD_DIM:]))
            else:
                per_dir.append((-st[:, :HEAD_DIM], st[:, HEAD_DIM:], None, None))
        res.append(per_dir)
    return res


def _dn_kernel(sc_ref, ql_ref, kl_ref, vl_ref, gate_ref, qc_ref, kc_ref, vc_ref,
               gl_ref, gc_ref, wq_ref, wk_ref, wv_ref, nw_ref, y_ref,
               qs, ks, vs, rows_s, lhs_s, add_s, obuf, s_ref):
    head = pl.program_id(1)
    n = DN_CHUNK
    t_lat = ql_ref.shape[1]
    t_ctx = qc_ref.shape[1]
    nc_lat = t_lat // n
    nc_ctx = t_ctx // n
    half = nc_lat // 2
    conv_rows = 256
    ctx_group = 2 if nc_ctx % 2 == 0 else 1
    lat_group = 8 if nc_lat % 16 == 0 else 2

    a_log_f, a_log_b = sc_ref[head, 0], sc_ref[head, 1]
    dtb_f, dtb_b = sc_ref[head, 2], sc_ref[head, 3]

    def chunk_rows(c):
        return pl.ds(pl.multiple_of(c * n, n), n)

    def prep(src_refs, total):
        def block(t0, conv):
            q = _l2norm(conv(src_refs[0], wq_ref[...], t0))
            qs[pl.ds(t0, conv_rows), :] = q * (HEAD_DIM ** -0.5)
            ks[pl.ds(t0, conv_rows), :] = _l2norm(conv(src_refs[1], wk_ref[...], t0))
            vs[pl.ds(t0, conv_rows), :] = conv(src_refs[2], wv_ref[...], t0)

        edge = lambda u_ref, w, t0: _conv_block(u_ref, w, t0, conv_rows, total)
        inner = lambda u_ref, w, t0: _conv_block_inner(u_ref, w, t0, conv_rows)
        n_blocks = total // conv_rows
        block(0, edge)
        if n_blocks > 1:
            block((n_blocks - 1) * conv_rows, edge)

        def body(i, carry):
            block(pl.multiple_of(i * conv_rows, conv_rows), inner)
            return carry
        lax.fori_loop(1, n_blocks - 1, body, 0)

    def gate_rows(g_ref, n_chunks):
        beta_raw = g_ref[0, 0, 0]
        alpha_raw = g_ref[0, 0, 1]
        par = _row_parity(alpha_raw.shape)
        a_vec = jnp.exp(jnp.where(par == 0, a_log_f, a_log_b))
        dtb = jnp.where(par == 0, dtb_f, dtb_b)
        cum, tot, row = _cumulate_gate_rows(-a_vec * _softplus(alpha_raw + dtb))
        beta = _sigmoid(beta_raw).reshape(cum.shape)
        packed = jnp.where(row < 2, beta, jnp.where(row < 4, cum, tot))
        rows_s[0:n_chunks] = packed.reshape(n_chunks, GATE_ROWS, n)

    def intra_group(cs, with_out, between_levels=()):
        res = _dn_intra_chunks([(qs[chunk_rows(c), :], ks[chunk_rows(c), :], vs[chunk_rows(c), :], rows_s[c])
                                for c in cs], with_out, between_levels)
        for c, per_dir in zip(cs, res):
            for d in range(2):
                s_mul, s_add, o_mul, o_add = per_dir[d]
                lhs_s[d, c, 0:HEAD_DIM, :] = s_mul.astype(BF16)
                add_s[d, c, 0:HEAD_DIM, :] = s_add
                if with_out:
                    lhs_s[d, c, HEAD_DIM:, :] = o_mul.astype(BF16)
                    add_s[d, c, HEAD_DIM:, :] = o_add

    def state_steps(cf, cb, with_out):
        dc = ((0, cf), (1, cb))
        rows = slice(None) if with_out else slice(0, HEAD_DIM)
        ss = [s_ref[d] for d, _ in dc]
        rs = [jnp.dot(lhs_s[d, c, rows, :], s.astype(BF16), preferred_element_type=F32) + add_s[d, c, rows, :]
              for (d, c), s in zip(dc, ss)]
        for (d, c), s, r in zip(dc, ss, rs):
            s_ref[d] = s * jnp.exp(rows_s[c][4 + d:5 + d, :]) + r[:HEAD_DIM]
        return [r[HEAD_DIM:] if with_out else None for r in rs]

    def finalize(o, r):
        ms = jnp.mean(o * o, axis=-1, keepdims=True)
        return o * lax.rsqrt(ms + NORM_EPS) * nw_ref[...] * _silu(gate_ref[0, r, :])

    s_ref[...] = jnp.zeros_like(s_ref)
    prep((qc_ref, kc_ref, vc_ref), t_ctx)
    gate_rows(gc_ref, nc_ctx)
    for g in range(nc_ctx // ctx_group):
        intra_group([g * ctx_group + j for j in range(ctx_group)], False)

    def ctx_body(i, carry):
        state_steps(i, nc_ctx - 1 - i, False)
        return carry
    lax.fori_loop(0, nc_ctx, ctx_body, 0)

    prep((ql_ref, kl_ref, vl_ref), t_lat)
    gate_rows(gl_ref, nc_lat)
    def first_visit(i):
        cb = nc_lat - 1 - i
        o_f, o_b = state_steps(i, cb, True)
        obuf[chunk_rows(i), :] = o_f
        obuf[chunk_rows(cb), :] = o_b

    side = lat_group // 2
    n_groups = half // side

    def group_chunks(g):
        return [g * side + j for j in range(side)] + [nc_lat - 1 - g * side - j for j in range(side)]

    def group_steps(g):
        return [functools.partial(first_visit, g * side + j) for j in range(side)]

    intra_group(group_chunks(0), True)

    def group_body(g, carry):
        intra_group(group_chunks(g), True, group_steps(g - 1))
        return carry
    lax.fori_loop(1, n_groups, group_body, 0)
    for step in group_steps(n_groups - 1):
        step()

    def second_body(i, carry):
        cb = nc_lat - 1 - i
        o_f, o_b = state_steps(i, cb, True)
        for c, o in ((i, o_f), (cb, o_b)):
            r = chunk_rows(c)
            y_ref[0, r, :] = finalize(o + obuf[r, :], r).astype(y_ref.dtype)
        return carry

    lax.fori_loop(half, nc_lat, second_body, 0)


def _deltanet(p_lat, p_ctx, g_lat, g_ctx, scalars, dn_conv, dn_norm):
    b, t_lat, _ = p_lat.shape
    t_ctx = p_ctx.shape[1]
    n = DN_CHUNK
    nc_lat, nc_ctx = t_lat // n, t_ctx // n
    assert n == HEAD_DIM and nc_lat % 16 == 0 and t_lat % 256 == 0 and t_ctx % 256 == 0 and t_ctx <= t_lat
    h = N_HEADS
    col = lambda off: (lambda i, j: (i, 0, off + j))
    lat_spec = lambda off: pl.BlockSpec((1, t_lat, HEAD_DIM), col(off))
    ctx_spec = lambda off: pl.BlockSpec((1, t_ctx, HEAD_DIM), col(off))
    conv_spec = lambda off: pl.BlockSpec((CONV_K, HEAD_DIM), lambda i, j: (0, off + j))
    gate_spec = lambda nc: pl.BlockSpec((1, 1, 2, nc, GATE_ROWS, n), lambda i, j: (i, j, 0, 0, 0, 0))
    seq = lambda dt: pltpu.VMEM((t_lat, HEAD_DIM), dt)
    step_terms = lambda dt: pltpu.VMEM((2, nc_lat, HEAD_DIM + n, HEAD_DIM), dt)
    return pl.pallas_call(
        _dn_kernel,
        grid=(b, h),
        in_specs=[pl.BlockSpec(memory_space=pltpu.SMEM),
                  lat_spec(0), lat_spec(h), lat_spec(2 * h), lat_spec(3 * h),
                  ctx_spec(0), ctx_spec(h), ctx_spec(2 * h),
                  gate_spec(nc_lat), gate_spec(nc_ctx),
                  conv_spec(0), conv_spec(h), conv_spec(2 * h),
                  pl.BlockSpec((1, HEAD_DIM), lambda i, j: (0, 0))],
        out_specs=pl.BlockSpec((1, t_lat, HEAD_DIM), lambda i, j: (i, 0, j)),
        out_shape=jax.ShapeDtypeStruct((b, t_lat, h * HEAD_DIM), BF16),
        scratch_shapes=[seq(F32), seq(F32), seq(F32),
                        pltpu.VMEM((nc_lat, GATE_ROWS, n), F32),
                        step_terms(BF16), step_terms(F32),
                        seq(F32),
                        pltpu.VMEM((2, HEAD_DIM, HEAD_DIM), F32)],
        compiler_params=pltpu.CompilerParams(
            dimension_semantics=("arbitrary", "arbitrary"), vmem_limit_bytes=VMEM_LIMIT),
        name="deltanet_scan",
    )(scalars, p_lat, p_lat, p_lat, p_lat, p_ctx, p_ctx, p_ctx, g_lat, g_ctx,
      dn_conv, dn_conv, dn_conv, dn_norm.reshape(1, HEAD_DIM))


def _ml_group(problems, states, with_out):
    n = problems[0][1].shape[0]
    ii, jj = _iota2(n)
    eye = ii == jj
    ones_blk = jnp.ones((n, HEAD_DIM), BF16)

    def lane_spread(rows):
        splits = [_split2(jnp.where(eye, row, 0.0)) for row in rows]
        return [sum(jnp.dot(t, ones_blk, preferred_element_type=F32) for t in sp) for sp in splits]

    a_rows = [x[d:d + 1, :] for d, _, _, _, x in problems]
    a_spreads = lane_spread(a_rows)
    b_spreads = lane_spread([x[2 + d:3 + d, :] for d, _, _, _, x in problems]) if with_out else None
    pre = []
    for (d, q, k, v, x), a_r, a_s in zip(problems, a_rows, a_spreads):
        amax = jnp.max(a_r, axis=1, keepdims=True)
        v_ext = jnp.concatenate([v.astype(BF16), ones_blk], axis=1)
        kw = (k * jnp.exp(a_s - amax)).astype(BF16)
        pre.append((amax, v_ext, kw))
    ups = [lax.dot_general(kw, v_ext, TN_DIMS, preferred_element_type=F32) for _, v_ext, kw in pre]
    if with_out:
        cm_rows = [jnp.max(jnp.where(_masks(d != 0, n)[0], a_s[:, :n], NEG_BIG), axis=0, keepdims=True)
                   for (d, _, _, _, _), a_s in zip(problems, a_spreads)]
        cm_spreads = lane_spread(cm_rows)
        qks = [_dot_nt(q, k) for _, q, k, _, _ in problems]
        scores = []
        for (d, _, _, _, _), a_r, cm_s, qk in zip(problems, a_rows, cm_spreads, qks):
            incl, _ = _masks(d == 0, n)
            expo = jnp.where(incl, jnp.broadcast_to(a_r, (n, n)) - cm_s[:, :n], NEG_BIG)
            scores.append((qk * jnp.exp(expo)).astype(BF16))
        intra = [jnp.dot(s, v_ext, preferred_element_type=F32) for s, (_, v_ext, _) in zip(scores, pre)]
    states = list(states)
    starts = []
    for (d, _, _, _, x), (amax, _, _), up in zip(problems, pre, ups):
        c_ext, m = states[d]
        starts.append((c_ext, m))
        mx = jnp.maximum(m, amax)
        states[d] = (jnp.exp(m - mx) * c_ext + jnp.exp(amax - mx) * up, x[4 + d:5 + d, 0:1] + mx)
    if not with_out:
        return states, [None] * len(problems)
    inter = [_dot(q, c_ext) for (_, q, _, _, _), (c_ext, _) in zip(problems, starts)]
    hs = []
    for (_, m), cm_s, b_s, qc, sv in zip(starts, cm_spreads, b_spreads, inter, intra):
        mm = jnp.maximum(m, cm_s)
        w_inter = jnp.exp(m - mm)
        w_intra = jnp.exp(cm_s - mm)
        num = w_inter * qc[:, :HEAD_DIM] + w_intra * sv[:, :HEAD_DIM]
        den = w_inter * qc[:, HEAD_DIM:] + w_intra * sv[:, HEAD_DIM:]
        hs.append(num / jnp.maximum(jnp.abs(den), jnp.exp(-(b_s + mm))))
    return states, hs


def _ml_kernel(sc_ref, ql_ref, kl_ref, vl_ref, og_ref, qc_ref, kc_ref, vc_ref,
               gl_ref, gc_ref, nw_ref, y_ref, rl, rc, obuf, c_ref, m_ref):
    head = pl.program_id(1)
    t_lat = ql_ref.shape[1]
    t_ctx = qc_ref.shape[1]
    nc_lat = t_lat // ML_CHUNK
    nc_ctx = t_ctx // ML_CHUNK
    half = nc_lat // 2
    k_scale = HEAD_DIM ** -0.5

    igb_f, igb_b = sc_ref[head, 0], sc_ref[head, 1]
    fgb_f, fgb_b = sc_ref[head, 2], sc_ref[head, 3]

    def gate_rows(g_ref, dst, n_chunks):
        ig_raw = g_ref[0, 0, 0]
        fg_raw = g_ref[0, 0, 1]
        par = _row_parity(fg_raw.shape)
        lf = -_softplus(-(fg_raw + jnp.where(par == 0, fgb_f, fgb_b)))
        cum, tot, row = _cumulate_gate_rows(lf)
        ic = (ig_raw + jnp.where(par == 0, igb_f, igb_b)).reshape(n_chunks * GATE_ROWS, ML_CHUNK)
        packed = jnp.where(row < 2, ic - cum, jnp.where(row < 4, cum, tot))
        dst[...] = packed.reshape(n_chunks, GATE_ROWS, ML_CHUNK)

    gate_rows(gl_ref, rl, nc_lat)
    gate_rows(gc_ref, rc, nc_ctx)

    c_ref[...] = jnp.zeros_like(c_ref)
    m_ref[...] = jnp.zeros_like(m_ref)

    def run_steps(first_step, n_steps, total, load, gates, with_out):
        problems = []
        for j in range(n_steps):
            for d, c in ((0, first_step + j), (1, total - 1 - first_step - j)):
                problems.append((d, load(0, c), load(1, c) * k_scale, load(2, c), gates[c]))
        states = [(c_ref[d], m_ref[d, 0:1, 0:1]) for d in range(2)]
        states, hs = _ml_group(problems, states, with_out)
        for d, (c_ext, m) in enumerate(states):
            c_ref[d] = c_ext
            m_ref[d] = jnp.broadcast_to(m, m_ref.shape[1:])
        return hs

    ctx_refs = (qc_ref, kc_ref, vc_ref)
    lat_refs = (ql_ref, kl_ref, vl_ref)

    def chunk_rows(c):
        return pl.ds(pl.multiple_of(c * ML_CHUNK, ML_CHUNK), ML_CHUNK)

    def ctx_load(which, c):
        return ctx_refs[which][0, chunk_rows(c), :]

    def lat_load(which, c):
        return lat_refs[which][0, chunk_rows(c), :]

    ctx_unroll = ML_UNROLL if nc_ctx % ML_UNROLL == 0 else 1

    def ctx_body(i, carry):
        run_steps(i * ctx_unroll, ctx_unroll, nc_ctx, ctx_load, rc, False)
        return carry

    lax.fori_loop(0, nc_ctx // ctx_unroll, ctx_body, 0)

    def finalize(hh, r):
        ms = jnp.mean(hh * hh, axis=-1, keepdims=True)
        y = hh * lax.rsqrt(ms + NORM_EPS) * nw_ref[...]
        return y * _sigmoid(og_ref[0, r, :])

    def lat_body(i, second):
        first_step = i * ML_UNROLL
        hs = run_steps(first_step, ML_UNROLL, nc_lat, lat_load, rl, True)
        for j in range(ML_UNROLL):
            for d, c in ((0, first_step + j), (1, nc_lat - 1 - first_step - j)):
                hh = hs[2 * j + d]
                r = chunk_rows(c)
                if second:
                    y_ref[0, r, :] = finalize(hh + obuf[r, :], r).astype(y_ref.dtype)
                else:
                    obuf[r, :] = hh

    def first_body(i, carry):
        lat_body(i, False)
        return carry

    def second_body(i, carry):
        lat_body(i, True)
        return carry

    lax.fori_loop(0, half // ML_UNROLL, first_body, 0)
    lax.fori_loop(half // ML_UNROLL, nc_lat // ML_UNROLL, second_body, 0)


def _mlstm(p_lat, p_ctx, g_lat, g_ctx, scalars, ml_norm):
    b, t_lat, _ = p_lat.shape
    t_ctx = p_ctx.shape[1]
    nc_lat, nc_ctx = t_lat // ML_CHUNK, t_ctx // ML_CHUNK
    assert nc_lat % (2 * ML_UNROLL) == 0 and t_lat == ML_CHUNK * GRID_W
    h = N_HEADS
    lat_spec = lambda off: pl.BlockSpec((1, t_lat, HEAD_DIM), lambda i, j: (i, 0, off + j))
    ctx_spec = lambda off: pl.BlockSpec((1, t_ctx, HEAD_DIM), lambda i, j: (i, 0, off + j))
    gate_spec = lambda nc: pl.BlockSpec((1, 1, 2, nc, GATE_ROWS, ML_CHUNK), lambda i, j: (i, j, 0, 0, 0, 0))
    return pl.pallas_call(
        _ml_kernel,
        grid=(b, h),
        in_specs=[pl.BlockSpec(memory_space=pltpu.SMEM),
                  lat_spec(0), lat_spec(h), lat_spec(2 * h), lat_spec(3 * h),
                  ctx_spec(0), ctx_spec(h), ctx_spec(2 * h),
                  gate_spec(nc_lat), gate_spec(nc_ctx),
                  pl.BlockSpec((1, HEAD_DIM), lambda i, j: (0, j))],
        out_specs=pl.BlockSpec((1, t_lat, HEAD_DIM), lambda i, j: (i, 0, j)),
        out_shape=jax.ShapeDtypeStruct((b, t_lat, h * HEAD_DIM), F32),
        scratch_shapes=[pltpu.VMEM((nc_lat, GATE_ROWS, ML_CHUNK), F32), pltpu.VMEM((nc_ctx, GATE_ROWS, ML_CHUNK), F32),
                        pltpu.VMEM((t_lat, HEAD_DIM), F32), pltpu.VMEM((2, HEAD_DIM, 2 * HEAD_DIM), F32),
                        pltpu.VMEM((2, 8, HEAD_DIM), F32)],
        compiler_params=pltpu.CompilerParams(
            dimension_semantics=("arbitrary", "arbitrary"), vmem_limit_bytes=VMEM_LIMIT),
        name="mlstm_scan",
    )(scalars, p_lat, p_lat, p_lat, p_lat, p_ctx, p_ctx, p_ctx, g_lat, g_ctx,
      ml_norm.reshape(1, h * HEAD_DIM))


def _ffn_kernel(x_ref, ydn_ref, yml_ref, mod_ref, n2_ref, fn_ref, wo_dn_ref, wo_ml_ref,
                wg_ref, wu_ref, wd_ref, o_ref, h_ref, acc_ref, *, f_tile):
    yml = jnp.swapaxes(yml_ref[0], 0, 1).reshape(x_ref.shape[1], yml_ref.shape[3]).astype(BF16)
    mix = (jnp.dot(ydn_ref[0], wo_dn_ref[...], preferred_element_type=F32)
           + jnp.dot(yml, wo_ml_ref[...], preferred_element_type=F32))
    x1 = x_ref[0] + mod_ref[0, 2:3, :] * mix
    h_ref[...] = _rms_mod(x1, n2_ref[...], mod_ref[0, 4:5, :], mod_ref[0, 3:4, :]).astype(BF16)
    acc_ref[...] = x1
    g2 = mod_ref[0, 5:6, :]
    d_ff = wg_ref.shape[1]
    for j in range(d_ff // f_tile):
        sl = slice(j * f_tile, (j + 1) * f_tile)
        gate = jnp.dot(h_ref[...], wg_ref[:, sl], preferred_element_type=F32)
        up = jnp.dot(h_ref[...], wu_ref[:, sl], preferred_element_type=F32)
        act = (_silu(gate) * up).astype(BF16)
        acc_ref[...] += g2 * jnp.dot(act, wd_ref[sl, :], preferred_element_type=F32)
    x2 = acc_ref[...]
    ms = jnp.mean(x2 * x2, axis=-1, keepdims=True)
    o_ref[0] = x2 * lax.rsqrt(ms + NORM_EPS) * fn_ref[...]


def _out_ffn(x, y_dn, y_ml, mod, norm2, final_norm, wo_dn, wo_ml, w_gate, w_up, w_down, tm):
    b, t, d = x.shape
    d_mix = y_dn.shape[2]
    d_ff = w_gate.shape[1]
    assert tm % GRID_W == 0 and (tm // GRID_W) % 8 == 0
    const = lambda shape: pl.BlockSpec(shape, lambda i, j: (0,) * len(shape),
                                       pipeline_mode=pl.Buffered(1))
    kern = functools.partial(_ffn_kernel, f_tile=256)
    return pl.pallas_call(
        kern,
        grid=(b, t // tm),
        in_specs=[pl.BlockSpec((1, tm, d), lambda i, j: (i, j, 0)),
                  pl.BlockSpec((1, tm, d_mix), lambda i, j: (i, j, 0)),
                  pl.BlockSpec((1, GRID_W, tm // GRID_W, d_mix), lambda i, j: (i, 0, j, 0)),
                  pl.BlockSpec((1, N_MOD, d), lambda i, j: (i, 0, 0)),
                  const((1, d)), const((1, d)),
                  const((d_mix, d)), const((d_mix, d)),
                  const((d, d_ff)), const((d, d_ff)), const((d_ff, d))],
        out_specs=pl.BlockSpec((1, tm, d), lambda i, j: (i, j, 0)),
        out_shape=jax.ShapeDtypeStruct((b, t, d), F32),
        scratch_shapes=[pltpu.VMEM((tm, d), BF16), pltpu.VMEM((tm, d), F32)],
        compiler_params=pltpu.CompilerParams(
            dimension_semantics=("arbitrary", "arbitrary"), vmem_limit_bytes=VMEM_LIMIT),
        name="out_ffn",
    )(x, y_dn, y_ml, mod, norm2.reshape(1, d), final_norm.reshape(1, d),
      wo_dn, wo_ml, w_gate, w_up, w_down)


def _gate_weight_rows(w_in, d_group):
    h = N_HEADS
    cols = []
    for mixer in range(2):
        base = mixer * (4 * d_group + 4 * h) + 4 * d_group
        for head in range(h):
            for slab in range(2):
                pair = [base + slab * 2 * h + head, base + slab * 2 * h + h + head]
                cols += pair * (GATE_ROWS // 2)
    return w_in[:, jnp.array(cols)].T


def _chunk_major_gates(gt, chunk, col_major):
    b, _, t = gt.shape
    g = gt.reshape(b, N_HEADS, 2, GATE_ROWS, t // chunk, chunk)
    if col_major:
        return g.transpose(0, 1, 2, 5, 3, 4)
    return g.transpose(0, 1, 2, 4, 3, 5)


def kernel(x, c, ctx, c_ctx, w_mod, b_mod, norm1, w_in, dn_conv, dn_a_log, dn_dt_bias, dn_norm,
           ml_ig_bias, ml_fg_bias, ml_norm, w_out, norm2, w_ffn_in, w_ffn_out, final_norm):
    depth = w_mod.shape[0]
    assert depth == 1, "context outputs are only skipped for a single layer"
    b, t_lat, d = x.shape
    h = N_HEADS
    d_group = h * HEAD_DIM
    d_ff = w_ffn_out.shape[1]
    layer = 0

    pad_rows = -(b + 1) % 8
    cc = jnp.concatenate([c, c_ctx[None, :], jnp.zeros((pad_rows, d), F32)], axis=0)
    mod = _modulation(cc, w_mod[layer], b_mod[layer])
    mod_lat = mod[:, :b].transpose(1, 0, 2)
    mod_ctx = jnp.broadcast_to(mod[:, b][None], (b, N_MOD, d))

    w = w_in[layer]
    dn_cols = 4 * d_group + 4 * h
    w_dn = w[:, :4 * d_group].astype(BF16)
    w_ml = w[:, dn_cols:dn_cols + 4 * d_group].astype(BF16)
    w_gate_t = _gate_weight_rows(w, d_group).astype(BF16)
    pdn_lat, pml_lat, gt_lat = _in_projection(x, mod_lat, norm1[layer], w_dn, w_ml, w_gate_t,
                                              tm=512, col_major=True)
    pdn_ctx, pml_ctx, gt_ctx = _in_projection(ctx, mod_ctx, norm1[layer], w_dn, w_ml, w_gate_t,
                                              tm=ctx.shape[1], col_major=False)
    pml_lat = pml_lat.reshape(b, t_lat, 4 * d_group)

    dn_gate_rows = h * 2 * GATE_ROWS
    g_dn_lat = _chunk_major_gates(gt_lat[:, :dn_gate_rows], DN_CHUNK, False)
    g_dn_ctx = _chunk_major_gates(gt_ctx[:, :dn_gate_rows], DN_CHUNK, False)
    g_ml_lat = _chunk_major_gates(gt_lat[:, dn_gate_rows:], ML_CHUNK, True)
    g_ml_ctx = _chunk_major_gates(gt_ctx[:, dn_gate_rows:], ML_CHUNK, False)

    dn_scal = jnp.concatenate([dn_a_log[layer].T, dn_dt_bias[layer].T], axis=1)
    ml_scal = jnp.concatenate([ml_ig_bias[layer].T, ml_fg_bias[layer].T], axis=1)

    y_dn = _deltanet(pdn_lat, pdn_ctx, g_dn_lat, g_dn_ctx, dn_scal, dn_conv[layer], dn_norm[layer])
    y_ml = _mlstm(pml_lat, pml_ctx, g_ml_lat, g_ml_ctx, ml_scal, ml_norm[layer])
    y_ml = y_ml.reshape(b, GRID_W, t_lat // GRID_W, d_group)

    wo = w_out[layer].astype(BF16)
    wf = w_ffn_in[layer].astype(BF16)
    return _out_ffn(x, y_dn, y_ml, mod_lat, norm2[layer], final_norm,
                    wo[:d_group], wo[d_group:], wf[:, :d_ff], wf[:, d_ff:],
                    w_ffn_out[layer].astype(BF16), tm=512)
```

```python
import functools

import jax
import jax.numpy as jnp
from jax import lax
from jax.experimental import pallas as pl
from jax.experimental.pallas import tpu as pltpu

F32 = jnp.float32
BF16 = jnp.bfloat16

DN_CHUNK = 128
ML_CHUNK = 128
ML_UNROLL = 4
GRID_W = 64
HEAD_DIM = 128
N_HEADS = 4
CONV_K = 5
NORM_EPS = 1e-6
N_MOD = 6
GATE_ROWS = 8
NEG_BIG = -1e30
VMEM_LIMIT = 56 * 1024 * 1024

NT_DIMS = (((1,), (1,)), ((), ()))
TN_DIMS = (((0,), (0,)), ((), ()))


def _dot(a, b):
    return jnp.dot(a.astype(BF16), b.astype(BF16), preferred_element_type=F32)


def _dot_nt(a, b):
    return lax.dot_general(a.astype(BF16), b.astype(BF16), NT_DIMS, preferred_element_type=F32)


def _split3(x):
    hi = x.astype(BF16)
    r1 = x - hi.astype(F32)
    mid = r1.astype(BF16)
    lo = (r1 - mid.astype(F32)).astype(BF16)
    return hi, mid, lo


def _split2(x):
    hi = x.astype(BF16)
    return hi, (x - hi.astype(F32)).astype(BF16)


def _dot_exact_rhs(x, m_bf16):
    hi, mid, lo = _split3(x)
    f = lambda t: jnp.dot(t, m_bf16, preferred_element_type=F32)
    return f(hi) + f(mid) + f(lo)


def _softplus(x):
    return jnp.maximum(x, 0.0) + jnp.log(1.0 + jnp.exp(-jnp.abs(x)))


def _sigmoid(x):
    return 1.0 / (1.0 + jnp.exp(-x))


def _silu(x):
    return x * _sigmoid(x)


def _mod_kernel(c_ref, w_ref, b_ref, o_ref):
    sc = _silu(c_ref[...])
    o_ref[0] = jnp.dot(sc, w_ref[...], preferred_element_type=F32,
                       precision=lax.Precision.HIGHEST) + b_ref[0]


def _modulation(cc, w_mod, b_mod):
    rows, d = cc.shape
    return pl.pallas_call(
        _mod_kernel,
        grid=(N_MOD,),
        in_specs=[pl.BlockSpec((rows, d), lambda j: (0, 0)),
                  pl.BlockSpec((d, d), lambda j: (0, j)),
                  pl.BlockSpec((1, 1, d), lambda j: (j, 0, 0))],
        out_specs=pl.BlockSpec((1, rows, d), lambda j: (j, 0, 0)),
        out_shape=jax.ShapeDtypeStruct((N_MOD, rows, d), F32),
        compiler_params=pltpu.CompilerParams(vmem_limit_bytes=VMEM_LIMIT),
        name="modulation",
    )(cc, w_mod, b_mod.reshape(N_MOD, 1, d))


def _rms_mod(x, gain, scale, shift):
    ms = jnp.mean(x * x, axis=-1, keepdims=True)
    return (x * lax.rsqrt(ms + NORM_EPS) * gain) * (1.0 + scale) + shift


def _inproj_kernel(x_ref, mod_ref, n1_ref, wdn_ref, wml_ref, wg_ref, pdn_ref, pml_ref, gt_ref,
                   h_ref, hcm_ref, *, n_tile, col_major):
    h = _rms_mod(x_ref[0], n1_ref[...], mod_ref[0, 1:2, :], mod_ref[0, 0:1, :])
    h_ref[...] = h.astype(BF16)
    for j in range(wdn_ref.shape[1] // n_tile):
        cols = slice(j * n_tile, (j + 1) * n_tile)
        pdn_ref[0, :, cols] = jnp.dot(h_ref[...], wdn_ref[:, cols], preferred_element_type=F32)
    if col_major:
        tm, d = h.shape
        hcm_ref[...] = jnp.swapaxes(h.reshape(tm // GRID_W, GRID_W, d), 0, 1).reshape(tm, d).astype(BF16)
    lhs_ref = hcm_ref if col_major else h_ref
    for j in range(wml_ref.shape[1] // n_tile):
        cols = slice(j * n_tile, (j + 1) * n_tile)
        res = jnp.dot(lhs_ref[...], wml_ref[:, cols], preferred_element_type=F32)
        if col_major:
            pml_ref[0, :, :, cols] = res.reshape(pml_ref.shape[1], pml_ref.shape[2], n_tile)
        else:
            pml_ref[0, :, cols] = res
    gt_ref[0] = lax.dot_general(wg_ref[...], h_ref[...], NT_DIMS, preferred_element_type=F32)


def _in_projection(x, mod, norm1, w_dn, w_ml, w_gate_t, tm, col_major):
    b, t, d = x.shape
    n_dn, n_ml = w_dn.shape[1], w_ml.shape[1]
    n_gate = w_gate_t.shape[0]
    kern = functools.partial(_inproj_kernel, n_tile=512, col_major=col_major)
    if col_major:
        assert tm % GRID_W == 0 and (tm // GRID_W) % 8 == 0 and t % tm == 0
        rows = t // GRID_W
        ml_spec = pl.BlockSpec((1, GRID_W, tm // GRID_W, n_ml), lambda i, j: (i, 0, j, 0))
        ml_shape = jax.ShapeDtypeStruct((b, GRID_W, rows, n_ml), F32)
    else:
        ml_spec = pl.BlockSpec((1, tm, n_ml), lambda i, j: (i, j, 0))
        ml_shape = jax.ShapeDtypeStruct((b, t, n_ml), F32)
    const = lambda shape: pl.BlockSpec(shape, lambda i, j: (0,) * len(shape), pipeline_mode=pl.Buffered(1))
    return pl.pallas_call(
        kern,
        grid=(b, t // tm),
        in_specs=[pl.BlockSpec((1, tm, d), lambda i, j: (i, j, 0)),
                  pl.BlockSpec((1, N_MOD, d), lambda i, j: (i, 0, 0)),
                  const((1, d)), const((d, n_dn)), const((d, n_ml)), const((n_gate, d))],
        out_specs=[pl.BlockSpec((1, tm, n_dn), lambda i, j: (i, j, 0)),
                   ml_spec,
                   pl.BlockSpec((1, n_gate, tm), lambda i, j: (i, 0, j))],
        out_shape=[jax.ShapeDtypeStruct((b, t, n_dn), F32),
                   ml_shape,
                   jax.ShapeDtypeStruct((b, n_gate, t), F32)],
        scratch_shapes=[pltpu.VMEM((tm, d), BF16), pltpu.VMEM((tm, d), BF16)],
        compiler_params=pltpu.CompilerParams(
            dimension_semantics=("arbitrary", "arbitrary"), vmem_limit_bytes=VMEM_LIMIT),
        name="in_projection",
    )(x, mod, norm1.reshape(1, d), w_dn, w_ml, w_gate_t)


def _iota2(n):
    return (lax.broadcasted_iota(jnp.int32, (n, n), 0), lax.broadcasted_iota(jnp.int32, (n, n), 1))


def _to_columns(x):
    ii, jj = _iota2(x.shape[1])
    eye = (ii == jj).astype(BF16)
    hi, mid, lo = _split3(x)
    f = lambda t: lax.dot_general(eye, t, NT_DIMS, preferred_element_type=F32)
    return f(hi) + f(mid) + f(lo)


def _masks(fwd, n):
    ii, jj = _iota2(n)
    if fwd:
        return ii >= jj, ii > jj
    return ii <= jj, ii < jj


def _row_parity(shape):
    return lax.broadcasted_iota(jnp.int32, shape, len(shape) - 2) % 2


def _cumulate_gate_rows(logdecay):
    n_chunks, _, n = logdecay.shape
    ii, jj = _iota2(n)
    prefix = (ii <= jj).astype(BF16)
    suffix = (ii >= jj).astype(BF16)
    ones = jnp.ones((n, n), BF16)
    flat = logdecay.reshape(n_chunks * GATE_ROWS, n)
    par = _row_parity(flat.shape)
    cum = jnp.where(par == 0, _dot_exact_rhs(flat, prefix), _dot_exact_rhs(flat, suffix))
    tot = _dot_exact_rhs(flat, ones)
    row = lax.broadcasted_iota(jnp.int32, flat.shape, 0) % GATE_ROWS
    return cum, tot, row


def _conv_block(u_ref, w, t0, rows, total):
    main = u_ref[0, pl.ds(t0, rows), :]
    lo = jnp.maximum(t0 - 8, 0)
    hi = jnp.minimum(t0 + rows, total - 8)
    prev = jnp.where(t0 > 0, u_ref[0, pl.ds(pl.multiple_of(lo, 8), 8), :], 0.0)
    nxt = jnp.where(t0 + rows < total, u_ref[0, pl.ds(pl.multiple_of(hi, 8), 8), :], 0.0)
    ext = jnp.concatenate([prev, main, nxt], axis=0)
    acc = None
    for j in range(CONV_K):
        off = 8 + j - CONV_K // 2
        term = ext[off:off + rows, :] * w[j:j + 1, :]
        acc = term if acc is None else acc + term
    return _silu(acc)


def _conv_block_inner(u_ref, w, t0, rows):
    acc = None
    for j in range(CONV_K):
        term = u_ref[0, pl.ds(t0 + (j - CONV_K // 2), rows), :] * w[j:j + 1, :]
        acc = term if acc is None else acc + term
    return _silu(acc)


def _l2norm(x):
    return x * lax.rsqrt(jnp.sum(x * x, axis=-1, keepdims=True) + NORM_EPS)


def _unit_tri_inverses(a_list, between_levels=()):
    n = a_list[0].shape[0]
    ii, jj = _iota2(n)
    eye = (ii == jj).astype(F32)
    pair = (ii >> 1) == (jj >> 1)
    ts = [eye - jnp.where(pair, a, 0.0) for a in a_list]
    for level in range(1, n.bit_length() - 1):
        same_big = (ii >> (level + 1)) == (jj >> (level + 1))
        same_small = (ii >> level) == (jj >> level)
        couple = same_big & jnp.logical_not(same_small)
        es = [jnp.where(couple, a, 0.0).astype(BF16) for a in a_list]
        tbs = [t.astype(BF16) for t in ts]
        tes = [jnp.dot(tb, e, preferred_element_type=F32) for tb, e in zip(tbs, es)]
        ts = [t - jnp.dot(te.astype(BF16), tb, preferred_element_type=F32)
              for t, te, tb in zip(ts, tes, tbs)]
        if level <= len(between_levels):
            between_levels[level - 1]()
    return ts


def _dn_intra_chunks(qkvx, with_out, between_levels=()):
    n = qkvx[0][0].shape[0]
    cols = [_to_columns(x) for _, _, _, x in qkvx]
    kks = [_dot_nt(k, k) for _, k, _, _ in qkvx]
    qks = [_dot_nt(q, k) if with_out else None for q, k, _, _ in qkvx]
    parts = []
    for (q, k, v, x), col, kk in zip(qkvx, cols, kks):
        for d in range(2):
            beta_c = col[:, d:d + 1]
            g_c = col[:, 2 + d:3 + d]
            tot_c = col[:, 4 + d:5 + d]
            g_r = x[2 + d:3 + d, :]
            incl, strict = _masks(d == 0, n)
            decay = jnp.exp(jnp.where(incl, g_c - g_r, NEG_BIG))
            e_g = jnp.exp(g_c)
            a = jnp.where(strict, kk * beta_c * decay, 0.0)
            rhs = jnp.concatenate([k * (beta_c * e_g), v * beta_c], axis=1).astype(BF16)
            parts.append((a, rhs, decay, e_g, tot_c - g_c))
    ts = _unit_tri_inverses([p[0] for p in parts], between_levels)
    wus = [jnp.dot(t.astype(BF16), p[1], preferred_element_type=F32).astype(BF16) for t, p in zip(ts, parts)]
    k_tail_ts = [(qkvx[i // 2][1] * jnp.exp(p[4])).T.astype(BF16) for i, p in enumerate(parts)]
    state_terms = [jnp.dot(kt, wu, preferred_element_type=F32) for kt, wu in zip(k_tail_ts, wus)]
    if with_out:
        scores = [(qks[i // 2] * p[2]).astype(BF16) for i, p in enumerate(parts)]
        out_terms = [jnp.dot(sc, wu, preferred_element_type=F32) for sc, wu in zip(scores, wus)]
    res = []
    for ci, (q, k, v, x) in enumerate(qkvx):
        per_dir = []
        for d in range(2):
            i = 2 * ci + d
            st = state_terms[i]
            if with_out:
                ot = out_terms[i]
                per_dir.append((-st[:, :HEAD_DIM], st[:, HEAD_DIM:],
                                q * parts[i][3] - ot[:, :HEAD_DIM], ot[:, HEAD_DIM:]))
            else:
                per_dir.append((-st[:, :HEAD_DIM], st[:, HEAD_DIM:], None, None))
        res.append(per_dir)
    return res


def _dn_kernel(sc_ref, ql_ref, kl_ref, vl_ref, gate_ref, qc_ref, kc_ref, vc_ref,
               gl_ref, gc_ref, wq_ref, wk_ref, wv_ref, nw_ref, y_ref,
               qs, ks, vs, rows_s, lhs_s, add_s, obuf, s_ref):
    head = pl.program_id(1)
    n = DN_CHUNK
    t_lat = ql_ref.shape[1]
    t_ctx = qc_ref.shape[1]
    nc_lat = t_lat // n
    nc_ctx = t_ctx // n
    half = nc_lat // 2
    conv_rows = 256
    ctx_group = 2 if nc_ctx % 2 == 0 else 1
    lat_group = 8 if nc_lat % 16 == 0 else 2

    a_log_f, a_log_b = sc_ref[head, 0], sc_ref[head, 1]
    dtb_f, dtb_b = sc_ref[head, 2], sc_ref[head, 3]

    def chunk_rows(c):
        return pl.ds(pl.multiple_of(c * n, n), n)

    def prep(src_refs, total):
        def block(t0, conv):
            q = _l2norm(conv(src_refs[0], wq_ref[...], t0))
            qs[pl.ds(t0, conv_rows), :] = q * (HEAD_DIM ** -0.5)
            ks[pl.ds(t0, conv_rows), :] = _l2norm(conv(src_refs[1], wk_ref[...], t0))
            vs[pl.ds(t0, conv_rows), :] = conv(src_refs[2], wv_ref[...], t0)

        edge = lambda u_ref, w, t0: _conv_block(u_ref, w, t0, conv_rows, total)
        inner = lambda u_ref, w, t0: _conv_block_inner(u_ref, w, t0, conv_rows)
        n_blocks = total // conv_rows
        block(0, edge)
        if n_blocks > 1:
            block((n_blocks - 1) * conv_rows, edge)

        def body(i, carry):
            block(pl.multiple_of(i * conv_rows, conv_rows), inner)
            return carry
        lax.fori_loop(1, n_blocks - 1, body, 0)

    def gate_rows(g_ref, n_chunks):
        beta_raw = g_ref[0, 0, 0]
        alpha_raw = g_ref[0, 0, 1]
        par = _row_parity(alpha_raw.shape)
        a_vec = jnp.exp(jnp.where(par == 0, a_log_f, a_log_b))
        dtb = jnp.where(par == 0, dtb_f, dtb_b)
        cum, tot, row = _cumulate_gate_rows(-a_vec * _softplus(alpha_raw + dtb))
        beta = _sigmoid(beta_raw).reshape(cum.shape)
        packed = jnp.where(row < 2, beta, jnp.where(row < 4, cum, tot))
        rows_s[0:n_chunks] = packed.reshape(n_chunks, GATE_ROWS, n)

    def intra_group(cs, with_out, between_levels=()):
        res = _dn_intra_chunks([(qs[chunk_rows(c), :], ks[chunk_rows(c), :], vs[chunk_rows(c), :], rows_s[c])
                                for c in cs], with_out, between_levels)
        for c, per_dir in zip(cs, res):
            for d in range(2):
                s_mul, s_add, o_mul, o_add = per_dir[d]
                lhs_s[d, c, 0:HEAD_DIM, :] = s_mul.astype(BF16)
                add_s[d, c, 0:HEAD_DIM, :] = s_add
                if with_out:
                    lhs_s[d, c, HEAD_DIM:, :] = o_mul.astype(BF16)
                    add_s[d, c, HEAD_DIM:, :] = o_add

    def state_steps(cf, cb, with_out):
        dc = ((0, cf), (1, cb))
        rows = slice(None) if with_out else slice(0, HEAD_DIM)
        ss = [s_ref[d] for d, _ in dc]
        rs = [jnp.dot(lhs_s[d, c, rows, :], s.astype(BF16), preferred_element_type=F32) + add_s[d, c, rows, :]
              for (d, c), s in zip(dc, ss)]
        for (d, c), s, r in zip(dc, ss, rs):
            s_ref[d] = s * jnp.exp(rows_s[c][4 + d:5 + d, :]) + r[:HEAD_DIM]
        return [r[HEAD_DIM:] if with_out else None for r in rs]

    def finalize(o, r):
        ms = jnp.mean(o * o, axis=-1, keepdims=True)
        return o * lax.rsqrt(ms + NORM_EPS) * nw_ref[...] * _silu(gate_ref[0, r, :])

    s_ref[...] = jnp.zeros_like(s_ref)
    prep((qc_ref, kc_ref, vc_ref), t_ctx)
    gate_rows(gc_ref, nc_ctx)
    for g in range(nc_ctx // ctx_group):
        intra_group([g * ctx_group + j for j in range(ctx_group)], False)

    def ctx_body(i, carry):
        state_steps(i, nc_ctx - 1 - i, False)
        return carry
    lax.fori_loop(0, nc_ctx, ctx_body, 0)

    prep((ql_ref, kl_ref, vl_ref), t_lat)
    gate_rows(gl_ref, nc_lat)
    def first_visit(i):
        cb = nc_lat - 1 - i
        o_f, o_b = state_steps(i, cb, True)
        obuf[chunk_rows(i), :] = o_f
        obuf[chunk_rows(cb), :] = o_b

    side = lat_group // 2
    n_groups = half // side

    def group_chunks(g):
        return [g * side + j for j in range(side)] + [nc_lat - 1 - g * side - j for j in range(side)]

    def group_steps(g):
        return [functools.partial(first_visit, g * side + j) for j in range(side)]

    intra_group(group_chunks(0), True)

    def group_body(g, carry):
        intra_group(group_chunks(g), True, group_steps(g - 1))
        return carry
    lax.fori_loop(1, n_groups, group_body, 0)
    for step in group_steps(n_groups - 1):
        step()

    def second_body(i, carry):
        cb = nc_lat - 1 - i
        o_f, o_b = state_steps(i, cb, True)
        for c, o in ((i, o_f), (cb, o_b)):
            r = chunk_rows(c)
            y_ref[0, r, :] = finalize(o + obuf[r, :], r).astype(y_ref.dtype)
        return carry

    lax.fori_loop(half, nc_lat, second_body, 0)


def _deltanet(p_lat, p_ctx, g_lat, g_ctx, scalars, dn_conv, dn_norm):
    b, t_lat, _ = p_lat.shape
    t_ctx = p_ctx.shape[1]
    n = DN_CHUNK
    nc_lat, nc_ctx = t_lat // n, t_ctx // n
    assert n == HEAD_DIM and nc_lat % 16 == 0 and t_lat % 256 == 0 and t_ctx % 256 == 0 and t_ctx <= t_lat
    h = N_HEADS
    col = lambda off: (lambda i, j: (i, 0, off + j))
    lat_spec = lambda off: pl.BlockSpec((1, t_lat, HEAD_DIM), col(off))
    ctx_spec = lambda off: pl.BlockSpec((1, t_ctx, HEAD_DIM), col(off))
    conv_spec = lambda off: pl.BlockSpec((CONV_K, HEAD_DIM), lambda i, j: (0, off + j))
    gate_spec = lambda nc: pl.BlockSpec((1, 1, 2, nc, GATE_ROWS, n), lambda i, j: (i, j, 0, 0, 0, 0))
    seq = lambda dt: pltpu.VMEM((t_lat, HEAD_DIM), dt)
    step_terms = lambda dt: pltpu.VMEM((2, nc_lat, HEAD_DIM + n, HEAD_DIM), dt)
    return pl.pallas_call(
        _dn_kernel,
        grid=(b, h),
        in_specs=[pl.BlockSpec(memory_space=pltpu.SMEM),
                  lat_spec(0), lat_spec(h), lat_spec(2 * h), lat_spec(3 * h),
                  ctx_spec(0), ctx_spec(h), ctx_spec(2 * h),
                  gate_spec(nc_lat), gate_spec(nc_ctx),
                  conv_spec(0), conv_spec(h), conv_spec(2 * h),
                  pl.BlockSpec((1, HEAD_DIM), lambda i, j: (0, 0))],
        out_specs=pl.BlockSpec((1, t_lat, HEAD_DIM), lambda i, j: (i, 0, j)),
        out_shape=jax.ShapeDtypeStruct((b, t_lat, h * HEAD_DIM), BF16),
        scratch_shapes=[seq(F32), seq(F32), seq(F32),
                        pltpu.VMEM((nc_lat, GATE_ROWS, n), F32),
                        step_terms(BF16), step_terms(F32),
                        seq(F32),
                        pltpu.VMEM((2, HEAD_DIM, HEAD_DIM), F32)],
        compiler_params=pltpu.CompilerParams(
            dimension_semantics=("arbitrary", "arbitrary"), vmem_limit_bytes=VMEM_LIMIT),
        name="deltanet_scan",
    )(scalars, p_lat, p_lat, p_lat, p_lat, p_ctx, p_ctx, p_ctx, g_lat, g_ctx,
      dn_conv, dn_conv, dn_conv, dn_norm.reshape(1, HEAD_DIM))


def _ml_group(problems, states, with_out):
    n = problems[0][1].shape[0]
    ii, jj = _iota2(n)
    eye = ii == jj
    ones_blk = jnp.ones((n, HEAD_DIM), BF16)

    def lane_spread(rows):
        splits = [_split2(jnp.where(eye, row, 0.0)) for row in rows]
        return [sum(jnp.dot(t, ones_blk, preferred_element_type=F32) for t in sp) for sp in splits]

    a_rows = [x[d:d + 1, :] for d, _, _, _, x in problems]
    a_spreads = lane_spread(a_rows)
    b_spreads = lane_spread([x[2 + d:3 + d, :] for d, _, _, _, x in problems]) if with_out else None
    ms = [m for _, m in states]
    chain = []
    for (d, _, _, _, x), a_r in zip(problems, a_rows):
        mx = jnp.maximum(ms[d], jnp.max(a_r, axis=1, keepdims=True))
        chain.append((ms[d], mx))
        ms[d] = x[4 + d:5 + d, 0:1] + mx
    pre = []
    for (d, q, k, v, x), a_s, (_, mx) in zip(problems, a_spreads, chain):
        v_ext = jnp.concatenate([v.astype(BF16), ones_blk], axis=1)
        pre.append((v_ext, (k * jnp.exp(a_s - mx)).astype(BF16)))
    ups = [lax.dot_general(kw, v_ext, TN_DIMS, preferred_element_type=F32) for v_ext, kw in pre]
    if with_out:
        cm_rows = [jnp.max(jnp.where(_masks(d != 0, n)[0], a_s[:, :n], NEG_BIG), axis=0, keepdims=True)
                   for (d, _, _, _, _), a_s in zip(problems, a_spreads)]
        cm_spreads = lane_spread(cm_rows)
        qks = [_dot_nt(q, k) for _, q, k, _, _ in problems]
        scores = []
        for (d, _, _, _, _), a_r, cm_s, qk in zip(problems, a_rows, cm_spreads, qks):
            incl, _ = _masks(d == 0, n)
            expo = jnp.where(incl, jnp.broadcast_to(a_r, (n, n)) - cm_s[:, :n], NEG_BIG)
            scores.append((qk * jnp.exp(expo)).astype(BF16))
        intra = [jnp.dot(s, v_ext, preferred_element_type=F32) for s, (v_ext, _) in zip(scores, pre)]
    cs = [c_ext for c_ext, _ in states]
    starts = []
    for (d, _, _, _, _), (m, mx), up in zip(problems, chain, ups):
        starts.append((cs[d], m))
        cs[d] = jnp.exp(m - mx) * cs[d] + up
    states = list(zip(cs, ms))
    if not with_out:
        return states, [None] * len(problems)
    inter = [_dot(q, c_ext) for (_, q, _, _, _), (c_ext, _) in zip(problems, starts)]
    hs = []
    for (_, m), cm_s, b_s, qc, sv in zip(starts, cm_spreads, b_spreads, inter, intra):
        mm = jnp.maximum(m, cm_s)
        w_inter = jnp.exp(m - mm)
        w_intra = jnp.exp(cm_s - mm)
        num = w_inter * qc[:, :HEAD_DIM] + w_intra * sv[:, :HEAD_DIM]
        den = w_inter * qc[:, HEAD_DIM:] + w_intra * sv[:, HEAD_DIM:]
        hs.append(num / jnp.maximum(jnp.abs(den), jnp.exp(-(b_s + mm))))
    return states, hs


def _ml_kernel(sc_ref, ql_ref, kl_ref, vl_ref, og_ref, qc_ref, kc_ref, vc_ref,
               gl_ref, gc_ref, nw_ref, y_ref, rl, rc, obuf, c_ref, m_ref):
    head = pl.program_id(1)
    t_lat = ql_ref.shape[1]
    t_ctx = qc_ref.shape[1]
    nc_lat = t_lat // ML_CHUNK
    nc_ctx = t_ctx // ML_CHUNK
    half = nc_lat // 2
    k_scale = HEAD_DIM ** -0.5

    igb_f, igb_b = sc_ref[head, 0], sc_ref[head, 1]
    fgb_f, fgb_b = sc_ref[head, 2], sc_ref[head, 3]

    def gate_rows(g_ref, dst, n_chunks):
        ig_raw = g_ref[0, 0, 0]
        fg_raw = g_ref[0, 0, 1]
        par = _row_parity(fg_raw.shape)
        lf = -_softplus(-(fg_raw + jnp.where(par == 0, fgb_f, fgb_b)))
        cum, tot, row = _cumulate_gate_rows(lf)
        ic = (ig_raw + jnp.where(par == 0, igb_f, igb_b)).reshape(n_chunks * GATE_ROWS, ML_CHUNK)
        packed = jnp.where(row < 2, ic - cum, jnp.where(row < 4, cum, tot))
        dst[...] = packed.reshape(n_chunks, GATE_ROWS, ML_CHUNK)

    gate_rows(gl_ref, rl, nc_lat)
    gate_rows(gc_ref, rc, nc_ctx)

    c_ref[...] = jnp.zeros_like(c_ref)
    m_ref[...] = jnp.zeros_like(m_ref)

    def run_steps(first_step, n_steps, total, load, gates, with_out):
        problems = []
        for j in range(n_steps):
            for d, c in ((0, first_step + j), (1, total - 1 - first_step - j)):
                problems.append((d, load(0, c), load(1, c) * k_scale, load(2, c), gates[c]))
        states = [(c_ref[d], m_ref[d, 0:1, 0:1]) for d in range(2)]
        states, hs = _ml_group(problems, states, with_out)
        for d, (c_ext, m) in enumerate(states):
            c_ref[d] = c_ext
            m_ref[d] = jnp.broadcast_to(m, m_ref.shape[1:])
        return hs

    ctx_refs = (qc_ref, kc_ref, vc_ref)
    lat_refs = (ql_ref, kl_ref, vl_ref)

    def chunk_rows(c):
        return pl.ds(pl.multiple_of(c * ML_CHUNK, ML_CHUNK), ML_CHUNK)

    def ctx_load(which, c):
        return ctx_refs[which][0, chunk_rows(c), :]

    def lat_load(which, c):
        return lat_refs[which][0, chunk_rows(c), :]

    ctx_unroll = ML_UNROLL if nc_ctx % ML_UNROLL == 0 else 1

    def ctx_body(i, carry):
        run_steps(i * ctx_unroll, ctx_unroll, nc_ctx, ctx_load, rc, False)
        return carry

    lax.fori_loop(0, nc_ctx // ctx_unroll, ctx_body, 0)

    def finalize(hh, r):
        ms = jnp.mean(hh * hh, axis=-1, keepdims=True)
        y = hh * lax.rsqrt(ms + NORM_EPS) * nw_ref[...]
        return y * _sigmoid(og_ref[0, r, :])

    def lat_body(i, second):
        first_step = i * ML_UNROLL
        hs = run_steps(first_step, ML_UNROLL, nc_lat, lat_load, rl, True)
        for j in range(ML_UNROLL):
            for d, c in ((0, first_step + j), (1, nc_lat - 1 - first_step - j)):
                hh = hs[2 * j + d]
                r = chunk_rows(c)
                if second:
                    y_ref[0, r, :] = finalize(hh + obuf[r, :], r).astype(y_ref.dtype)
                else:
                    obuf[r, :] = hh

    def first_body(i, carry):
        lat_body(i, False)
        return carry

    def second_body(i, carry):
        lat_body(i, True)
        return carry

    lax.fori_loop(0, half // ML_UNROLL, first_body, 0)
    lax.fori_loop(half // ML_UNROLL, nc_lat // ML_UNROLL, second_body, 0)


def _mlstm(p_lat, p_ctx, g_lat, g_ctx, scalars, ml_norm):
    b, t_lat, _ = p_lat.shape
    t_ctx = p_ctx.shape[1]
    nc_lat, nc_ctx = t_lat // ML_CHUNK, t_ctx // ML_CHUNK
    assert nc_lat % (2 * ML_UNROLL) == 0 and ML_CHUNK % (t_lat // GRID_W) == 0
    h = N_HEADS
    lat_spec = lambda off: pl.BlockSpec((1, t_lat, HEAD_DIM), lambda i, j: (i, 0, off + j))
    ctx_spec = lambda off: pl.BlockSpec((1, t_ctx, HEAD_DIM), lambda i, j: (i, 0, off + j))
    gate_spec = lambda nc: pl.BlockSpec((1, 1, 2, nc, GATE_ROWS, ML_CHUNK), lambda i, j: (i, j, 0, 0, 0, 0))
    return pl.pallas_call(
        _ml_kernel,
        grid=(b, h),
        in_specs=[pl.BlockSpec(memory_space=pltpu.SMEM),
                  lat_spec(0), lat_spec(h), lat_spec(2 * h), lat_spec(3 * h),
                  ctx_spec(0), ctx_spec(h), ctx_spec(2 * h),
                  gate_spec(nc_lat), gate_spec(nc_ctx),
                  pl.BlockSpec((1, HEAD_DIM), lambda i, j: (0, j))],
        out_specs=pl.BlockSpec((1, t_lat, HEAD_DIM), lambda i, j: (i, 0, j)),
        out_shape=jax.ShapeDtypeStruct((b, t_lat, h * HEAD_DIM), F32),
        scratch_shapes=[pltpu.VMEM((nc_lat, GATE_ROWS, ML_CHUNK), F32), pltpu.VMEM((nc_ctx, GATE_ROWS, ML_CHUNK), F32),
                        pltpu.VMEM((t_lat, HEAD_DIM), F32), pltpu.VMEM((2, HEAD_DIM, 2 * HEAD_DIM), F32),
                        pltpu.VMEM((2, 8, HEAD_DIM), F32)],
        compiler_params=pltpu.CompilerParams(
            dimension_semantics=("arbitrary", "arbitrary"), vmem_limit_bytes=VMEM_LIMIT),
        name="mlstm_scan",
    )(scalars, p_lat, p_lat, p_lat, p_lat, p_ctx, p_ctx, p_ctx, g_lat, g_ctx,
      ml_norm.reshape(1, h * HEAD_DIM))


def _ffn_kernel(x_ref, ydn_ref, yml_ref, mod_ref, n2_ref, fn_ref, wo_dn_ref, wo_ml_ref,
                wg_ref, wu_ref, wd_ref, o_ref, h_ref, acc_ref, *, f_tile):
    yml = jnp.swapaxes(yml_ref[0], 0, 1).reshape(x_ref.shape[1], yml_ref.shape[3]).astype(BF16)
    mix = (jnp.dot(ydn_ref[0], wo_dn_ref[...], preferred_element_type=F32)
           + jnp.dot(yml, wo_ml_ref[...], preferred_element_type=F32))
    x1 = x_ref[0] + mod_ref[0, 2:3, :] * mix
    h_ref[...] = _rms_mod(x1, n2_ref[...], mod_ref[0, 4:5, :], mod_ref[0, 3:4, :]).astype(BF16)
    acc_ref[...] = x1
    g2 = mod_ref[0, 5:6, :]
    d_ff = wg_ref.shape[1]
    for j in range(d_ff // f_tile):
        sl = slice(j * f_tile, (j + 1) * f_tile)
        gate = jnp.dot(h_ref[...], wg_ref[:, sl], preferred_element_type=F32)
        up = jnp.dot(h_ref[...], wu_ref[:, sl], preferred_element_type=F32)
        act = (_silu(gate) * up).astype(BF16)
        acc_ref[...] += g2 * jnp.dot(act, wd_ref[sl, :], preferred_element_type=F32)
    x2 = acc_ref[...]
    ms = jnp.mean(x2 * x2, axis=-1, keepdims=True)
    o_ref[0] = x2 * lax.rsqrt(ms + NORM_EPS) * fn_ref[...]


def _out_ffn(x, y_dn, y_ml, mod, norm2, final_norm, wo_dn, wo_ml, w_gate, w_up, w_down, tm):
    b, t, d = x.shape
    d_mix = y_dn.shape[2]
    d_ff = w_gate.shape[1]
    assert tm % GRID_W == 0 and (tm // GRID_W) % 8 == 0
    const = lambda shape: pl.BlockSpec(shape, lambda i, j: (0,) * len(shape),
                                       pipeline_mode=pl.Buffered(1))
    kern = functools.partial(_ffn_kernel, f_tile=256)
    return pl.pallas_call(
        kern,
        grid=(b, t // tm),
        in_specs=[pl.BlockSpec((1, tm, d), lambda i, j: (i, j, 0)),
                  pl.BlockSpec((1, tm, d_mix), lambda i, j: (i, j, 0)),
                  pl.BlockSpec((1, GRID_W, tm // GRID_W, d_mix), lambda i, j: (i, 0, j, 0)),
                  pl.BlockSpec((1, N_MOD, d), lambda i, j: (i, 0, 0)),
                  const((1, d)), const((1, d)),
                  const((d_mix, d)), const((d_mix, d)),
                  const((d, d_ff)), const((d, d_ff)), const((d_ff, d))],
        out_specs=pl.BlockSpec((1, tm, d), lambda i, j: (i, j, 0)),
        out_shape=jax.ShapeDtypeStruct((b, t, d), F32),
        scratch_shapes=[pltpu.VMEM((tm, d), BF16), pltpu.VMEM((tm, d), F32)],
        compiler_params=pltpu.CompilerParams(
            dimension_semantics=("arbitrary", "arbitrary"), vmem_limit_bytes=VMEM_LIMIT),
        name="out_ffn",
    )(x, y_dn, y_ml, mod, norm2.reshape(1, d), final_norm.reshape(1, d),
      wo_dn, wo_ml, w_gate, w_up, w_down)


def _gate_weight_rows(w_in, d_group):
    h = N_HEADS
    cols = []
    for mixer in range(2):
        base = mixer * (4 * d_group + 4 * h) + 4 * d_group
        for head in range(h):
            for slab in range(2):
                pair = [base + slab * 2 * h + head, base + slab * 2 * h + h + head]
                cols += pair * (GATE_ROWS // 2)
    return w_in[:, jnp.array(cols)].T


def _chunk_major_gates(gt, chunk, col_major):
    b, _, t = gt.shape
    if col_major:
        rows = t // GRID_W
        cols = chunk // rows
        g = gt.reshape(b, N_HEADS, 2, GATE_ROWS, rows, GRID_W // cols, cols)
        return g.transpose(0, 1, 2, 5, 3, 6, 4).reshape(b, N_HEADS, 2, t // chunk, GATE_ROWS, chunk)
    g = gt.reshape(b, N_HEADS, 2, GATE_ROWS, t // chunk, chunk)
    return g.transpose(0, 1, 2, 4, 3, 5)


def kernel(x, c, ctx, c_ctx, w_mod, b_mod, norm1, w_in, dn_conv, dn_a_log, dn_dt_bias, dn_norm,
           ml_ig_bias, ml_fg_bias, ml_norm, w_out, norm2, w_ffn_in, w_ffn_out, final_norm):
    depth = w_mod.shape[0]
    assert depth == 1, "context outputs are only skipped for a single layer"
    b, t_lat, d = x.shape
    h = N_HEADS
    d_group = h * HEAD_DIM
    d_ff = w_ffn_out.shape[1]
    layer = 0

    pad_rows = -(b + 1) % 8
    cc = jnp.concatenate([c, c_ctx[None, :], jnp.zeros((pad_rows, d), F32)], axis=0)
    mod = _modulation(cc, w_mod[layer], b_mod[layer])
    mod_lat = mod[:, :b].transpose(1, 0, 2)
    mod_ctx = jnp.broadcast_to(mod[:, b][None], (b, N_MOD, d))

    w = w_in[layer]
    dn_cols = 4 * d_group + 4 * h
    w_dn = w[:, :4 * d_group].astype(BF16)
    w_ml = w[:, dn_cols:dn_cols + 4 * d_group].astype(BF16)
    w_gate_t = _gate_weight_rows(w, d_group).astype(BF16)
    pdn_lat, pml_lat, gt_lat = _in_projection(x, mod_lat, norm1[layer], w_dn, w_ml, w_gate_t,
                                              tm=512, col_major=True)
    pdn_ctx, pml_ctx, gt_ctx = _in_projection(ctx, mod_ctx, norm1[layer], w_dn, w_ml, w_gate_t,
                                              tm=ctx.shape[1], col_major=False)
    pml_lat = pml_lat.reshape(b, t_lat, 4 * d_group)

    dn_gate_rows = h * 2 * GATE_ROWS
    g_dn_lat = _chunk_major_gates(gt_lat[:, :dn_gate_rows], DN_CHUNK, False)
    g_dn_ctx = _chunk_major_gates(gt_ctx[:, :dn_gate_rows], DN_CHUNK, False)
    g_ml_lat = _chunk_major_gates(gt_lat[:, dn_gate_rows:], ML_CHUNK, True)
    g_ml_ctx = _chunk_major_gates(gt_ctx[:, dn_gate_rows:], ML_CHUNK, False)

    dn_scal = jnp.concatenate([dn_a_log[layer].T, dn_dt_bias[layer].T], axis=1)
    ml_scal = jnp.concatenate([ml_ig_bias[layer].T, ml_fg_bias[layer].T], axis=1)

    y_dn = _deltanet(pdn_lat, pdn_ctx, g_dn_lat, g_dn_ctx, dn_scal, dn_conv[layer], dn_norm[layer])
    y_ml = _mlstm(pml_lat, pml_ctx, g_ml_lat, g_ml_ctx, ml_scal, ml_norm[layer])
    y_ml = y_ml.reshape(b, GRID_W, t_lat // GRID_W, d_group)

    wo = w_out[layer].astype(BF16)
    wf = w_ffn_in[layer].astype(BF16)
    return _out_ffn(x, y_dn, y_ml, mod_lat, norm2[layer], final_norm,
                    wo[:d_group], wo[d_group:], wf[:, :d_ff], wf[:, d_ff:],
                    w_ffn_out[layer].astype(BF16), tm=512)
```

```python
import functools

import jax
import jax.numpy as jnp
from jax import lax
from jax.experimental import pallas as pl
from jax.experimental.pallas import tpu as pltpu

F32 = jnp.float32
BF16 = jnp.bfloat16

DN_CHUNK = 128
ML_CHUNK = 128
ML_UNROLL = 4
GRID_W = 64
HEAD_DIM = 128
N_HEADS = 4
CONV_K = 5
NORM_EPS = 1e-6
N_MOD = 6
GATE_ROWS = 8
NEG_BIG = -1e30
VMEM_LIMIT = 56 * 1024 * 1024

NT_DIMS = (((1,), (1,)), ((), ()))
TN_DIMS = (((0,), (0,)), ((), ()))


def _dot(a, b):
    return jnp.dot(a.astype(BF16), b.astype(BF16), preferred_element_type=F32)


def _dot_nt(a, b):
    return lax.dot_general(a.astype(BF16), b.astype(BF16), NT_DIMS, preferred_element_type=F32)


def _split3(x):
    hi = x.astype(BF16)
    r1 = x - hi.astype(F32)
    mid = r1.astype(BF16)
    lo = (r1 - mid.astype(F32)).astype(BF16)
    return hi, mid, lo


def _split2(x):
    hi = x.astype(BF16)
    return hi, (x - hi.astype(F32)).astype(BF16)


def _dot_exact_rhs(x, m_bf16):
    hi, mid, lo = _split3(x)
    f = lambda t: jnp.dot(t, m_bf16, preferred_element_type=F32)
    return f(hi) + f(mid) + f(lo)


def _softplus(x):
    return jnp.maximum(x, 0.0) + jnp.log(1.0 + jnp.exp(-jnp.abs(x)))


def _sigmoid(x):
    return 1.0 / (1.0 + jnp.exp(-x))


def _silu(x):
    return x * _sigmoid(x)


def _mod_kernel(c_ref, w_ref, b_ref, o_ref):
    sc = _silu(c_ref[...])
    o_ref[0] = jnp.dot(sc, w_ref[...], preferred_element_type=F32,
                       precision=lax.Precision.HIGHEST) + b_ref[0]


def _modulation(cc, w_mod, b_mod):
    rows, d = cc.shape
    return pl.pallas_call(
        _mod_kernel,
        grid=(N_MOD,),
        in_specs=[pl.BlockSpec((rows, d), lambda j: (0, 0)),
                  pl.BlockSpec((d, d), lambda j: (0, j)),
                  pl.BlockSpec((1, 1, d), lambda j: (j, 0, 0))],
        out_specs=pl.BlockSpec((1, rows, d), lambda j: (j, 0, 0)),
        out_shape=jax.ShapeDtypeStruct((N_MOD, rows, d), F32),
        compiler_params=pltpu.CompilerParams(vmem_limit_bytes=VMEM_LIMIT),
        name="modulation",
    )(cc, w_mod, b_mod.reshape(N_MOD, 1, d))


def _rms_mod(x, gain, scale, shift):
    ms = jnp.mean(x * x, axis=-1, keepdims=True)
    return (x * lax.rsqrt(ms + NORM_EPS) * gain) * (1.0 + scale) + shift


def _inproj_kernel(x_ref, mod_ref, n1_ref, wdn_ref, wml_ref, wg_ref, pdn_ref, pml_ref, gt_ref,
                   h_ref, hcm_ref, *, n_tile, col_major):
    h = _rms_mod(x_ref[0], n1_ref[...], mod_ref[0, 1:2, :], mod_ref[0, 0:1, :])
    h_ref[...] = h.astype(BF16)
    for j in range(wdn_ref.shape[1] // n_tile):
        cols = slice(j * n_tile, (j + 1) * n_tile)
        pdn_ref[0, :, cols] = jnp.dot(h_ref[...], wdn_ref[:, cols], preferred_element_type=F32)
    if col_major:
        tm, d = h.shape
        hcm_ref[...] = jnp.swapaxes(h.reshape(tm // GRID_W, GRID_W, d), 0, 1).reshape(tm, d).astype(BF16)
    lhs_ref = hcm_ref if col_major else h_ref
    for j in range(wml_ref.shape[1] // n_tile):
        cols = slice(j * n_tile, (j + 1) * n_tile)
        res = jnp.dot(lhs_ref[...], wml_ref[:, cols], preferred_element_type=F32)
        if col_major:
            pml_ref[0, :, :, cols] = res.reshape(pml_ref.shape[1], pml_ref.shape[2], n_tile)
        else:
            pml_ref[0, :, cols] = res
    gt_ref[0] = lax.dot_general(wg_ref[...], h_ref[...], NT_DIMS, preferred_element_type=F32)


def _in_projection(x, mod, norm1, w_dn, w_ml, w_gate_t, tm, col_major):
    b, t, d = x.shape
    n_dn, n_ml = w_dn.shape[1], w_ml.shape[1]
    n_gate = w_gate_t.shape[0]
    kern = functools.partial(_inproj_kernel, n_tile=512, col_major=col_major)
    if col_major:
        assert tm % GRID_W == 0 and (tm // GRID_W) % 8 == 0 and t % tm == 0
        rows = t // GRID_W
        ml_spec = pl.BlockSpec((1, GRID_W, tm // GRID_W, n_ml), lambda i, j: (i, 0, j, 0))
        ml_shape = jax.ShapeDtypeStruct((b, GRID_W, rows, n_ml), F32)
    else:
        ml_spec = pl.BlockSpec((1, tm, n_ml), lambda i, j: (i, j, 0))
        ml_shape = jax.ShapeDtypeStruct((b, t, n_ml), F32)
    const = lambda shape: pl.BlockSpec(shape, lambda i, j: (0,) * len(shape), pipeline_mode=pl.Buffered(1))
    return pl.pallas_call(
        kern,
        grid=(b, t // tm),
        in_specs=[pl.BlockSpec((1, tm, d), lambda i, j: (i, j, 0)),
                  pl.BlockSpec((1, N_MOD, d), lambda i, j: (i, 0, 0)),
                  const((1, d)), const((d, n_dn)), const((d, n_ml)), const((n_gate, d))],
        out_specs=[pl.BlockSpec((1, tm, n_dn), lambda i, j: (i, j, 0)),
                   ml_spec,
                   pl.BlockSpec((1, n_gate, tm), lambda i, j: (i, 0, j))],
        out_shape=[jax.ShapeDtypeStruct((b, t, n_dn), F32),
                   ml_shape,
                   jax.ShapeDtypeStruct((b, n_gate, t), F32)],
        scratch_shapes=[pltpu.VMEM((tm, d), BF16), pltpu.VMEM((tm, d), BF16)],
        compiler_params=pltpu.CompilerParams(
            dimension_semantics=("arbitrary", "arbitrary"), vmem_limit_bytes=VMEM_LIMIT),
        name="in_projection",
    )(x, mod, norm1.reshape(1, d), w_dn, w_ml, w_gate_t)


def _iota2(n):
    return (lax.broadcasted_iota(jnp.int32, (n, n), 0), lax.broadcasted_iota(jnp.int32, (n, n), 1))


def _to_columns(x):
    ii, jj = _iota2(x.shape[1])
    eye = (ii == jj).astype(BF16)
    hi, mid, lo = _split3(x)
    f = lambda t: lax.dot_general(eye, t, NT_DIMS, preferred_element_type=F32)
    return f(hi) + f(mid) + f(lo)


def _masks(fwd, n):
    ii, jj = _iota2(n)
    if fwd:
        return ii >= jj, ii > jj
    return ii <= jj, ii < jj


def _row_parity(shape):
    return lax.broadcasted_iota(jnp.int32, shape, len(shape) - 2) % 2


def _cumulate_gate_rows(logdecay):
    n_chunks, _, n = logdecay.shape
    ii, jj = _iota2(n)
    prefix = (ii <= jj).astype(BF16)
    suffix = (ii >= jj).astype(BF16)
    ones = jnp.ones((n, n), BF16)
    flat = logdecay.reshape(n_chunks * GATE_ROWS, n)
    par = _row_parity(flat.shape)
    cum = jnp.where(par == 0, _dot_exact_rhs(flat, prefix), _dot_exact_rhs(flat, suffix))
    tot = _dot_exact_rhs(flat, ones)
    row = lax.broadcasted_iota(jnp.int32, flat.shape, 0) % GATE_ROWS
    return cum, tot, row


def _conv_block(u_ref, w, t0, rows, total):
    main = u_ref[0, pl.ds(t0, rows), :]
    lo = jnp.maximum(t0 - 8, 0)
    hi = jnp.minimum(t0 + rows, total - 8)
    prev = jnp.where(t0 > 0, u_ref[0, pl.ds(pl.multiple_of(lo, 8), 8), :], 0.0)
    nxt = jnp.where(t0 + rows < total, u_ref[0, pl.ds(pl.multiple_of(hi, 8), 8), :], 0.0)
    ext = jnp.concatenate([prev, main, nxt], axis=0)
    acc = None
    for j in range(CONV_K):
        off = 8 + j - CONV_K // 2
        term = ext[off:off + rows, :] * w[j:j + 1, :]
        acc = term if acc is None else acc + term
    return _silu(acc)


def _conv_block_inner(u_ref, w, t0, rows):
    acc = None
    for j in range(CONV_K):
        term = u_ref[0, pl.ds(t0 + (j - CONV_K // 2), rows), :] * w[j:j + 1, :]
        acc = term if acc is None else acc + term
    return _silu(acc)


def _l2norm(x):
    return x * lax.rsqrt(jnp.sum(x * x, axis=-1, keepdims=True) + NORM_EPS)


def _unit_tri_inverses(a_list, between_levels=()):
    n = a_list[0].shape[0]
    ii, jj = _iota2(n)
    eye = (ii == jj).astype(F32)
    pair = (ii >> 1) == (jj >> 1)
    ts = [eye - jnp.where(pair, a, 0.0) for a in a_list]
    for level in range(1, n.bit_length() - 1):
        same_big = (ii >> (level + 1)) == (jj >> (level + 1))
        same_small = (ii >> level) == (jj >> level)
        couple = same_big & jnp.logical_not(same_small)
        es = [jnp.where(couple, a, 0.0).astype(BF16) for a in a_list]
        tbs = [t.astype(BF16) for t in ts]
        tes = [jnp.dot(tb, e, preferred_element_type=F32) for tb, e in zip(tbs, es)]
        ts = [t - jnp.dot(te.astype(BF16), tb, preferred_element_type=F32)
              for t, te, tb in zip(ts, tes, tbs)]
        if level <= len(between_levels):
            between_levels[level - 1]()
    return ts


def _dn_intra_chunks(qkvx, with_out, between_levels=()):
    n = qkvx[0][0].shape[0]
    cols = [_to_columns(x) for _, _, _, x in qkvx]
    kks = [_dot_nt(k, k) for _, k, _, _ in qkvx]
    qks = [_dot_nt(q, k) if with_out else None for q, k, _, _ in qkvx]
    parts = []
    for (q, k, v, x), col, kk in zip(qkvx, cols, kks):
        for d in range(2):
            beta_c = col[:, d:d + 1]
            g_c = col[:, 2 + d:3 + d]
            tot_c = col[:, 4 + d:5 + d]
            g_r = x[2 + d:3 + d, :]
            incl, strict = _masks(d == 0, n)
            decay = jnp.exp(jnp.where(incl, g_c - g_r, NEG_BIG))
            e_g = jnp.exp(g_c)
            a = jnp.where(strict, kk * beta_c * decay, 0.0)
            rhs = jnp.concatenate([k * (beta_c * e_g), v * beta_c], axis=1).astype(BF16)
            parts.append((a, rhs, decay, e_g, tot_c - g_c))
    ts = _unit_tri_inverses([p[0] for p in parts], between_levels)
    wus = [jnp.dot(t.astype(BF16), p[1], preferred_element_type=F32).astype(BF16) for t, p in zip(ts, parts)]
    k_tail_ts = [(qkvx[i // 2][1] * jnp.exp(p[4])).T.astype(BF16) for i, p in enumerate(parts)]
    state_terms = [jnp.dot(kt, wu, preferred_element_type=F32) for kt, wu in zip(k_tail_ts, wus)]
    if with_out:
        scores = [(qks[i // 2] * p[2]).astype(BF16) for i, p in enumerate(parts)]
        out_terms = [jnp.dot(sc, wu, preferred_element_type=F32) for sc, wu in zip(scores, wus)]
    res = []
    for ci, (q, k, v, x) in enumerate(qkvx):
        per_dir = []
        for d in range(2):
            i = 2 * ci + d
            st = state_terms[i]
            if with_out:
                ot = out_terms[i]
                per_dir.append((-st[:, :HEAD_DIM], st[:, HEAD_DIM:],
                                q * parts[i][3] - ot[:, :HEAD_DIM], ot[:, HEAD_DIM:]))
            else:
                per_dir.append((-st[:, :HEAD_DIM], st[:, HEAD_DIM:], None, None))
        res.append(per_dir)
    return res


def _dn_kernel(sc_ref, ql_ref, kl_ref, vl_ref, gate_ref, qc_ref, kc_ref, vc_ref,
               gl_ref, gc_ref, wq_ref, wk_ref, wv_ref, nw_ref, y_ref,
               qs, ks, vs, rows_s, lhs_s, add_s, obuf, s_ref):
    head = pl.program_id(1)
    n = DN_CHUNK
    t_lat = ql_ref.shape[1]
    t_ctx = qc_ref.shape[1]
    nc_lat = t_lat // n
    nc_ctx = t_ctx // n
    half = nc_lat // 2
    conv_rows = 256
    ctx_group = 2 if nc_ctx % 2 == 0 else 1
    lat_group = 8 if nc_lat % 16 == 0 else 2

    a_log_f, a_log_b = sc_ref[head, 0], sc_ref[head, 1]
    dtb_f, dtb_b = sc_ref[head, 2], sc_ref[head, 3]

    def chunk_rows(c):
        return pl.ds(pl.multiple_of(c * n, n), n)

    def prep(src_refs, total):
        def block(t0, conv):
            q = _l2norm(conv(src_refs[0], wq_ref[...], t0))
            qs[pl.ds(t0, conv_rows), :] = q * (HEAD_DIM ** -0.5)
            ks[pl.ds(t0, conv_rows), :] = _l2norm(conv(src_refs[1], wk_ref[...], t0))
            vs[pl.ds(t0, conv_rows), :] = conv(src_refs[2], wv_ref[...], t0)

        edge = lambda u_ref, w, t0: _conv_block(u_ref, w, t0, conv_rows, total)
        inner = lambda u_ref, w, t0: _conv_block_inner(u_ref, w, t0, conv_rows)
        n_blocks = total // conv_rows
        block(0, edge)
        if n_blocks > 1:
            block((n_blocks - 1) * conv_rows, edge)

        def body(i, carry):
            block(pl.multiple_of(i * conv_rows, conv_rows), inner)
            return carry
        lax.fori_loop(1, n_blocks - 1, body, 0)

    def gate_rows(g_ref, n_chunks):
        beta_raw = g_ref[0, 0, 0]
        alpha_raw = g_ref[0, 0, 1]
        par = _row_parity(alpha_raw.shape)
        a_vec = jnp.exp(jnp.where(par == 0, a_log_f, a_log_b))
        dtb = jnp.where(par == 0, dtb_f, dtb_b)
        cum, tot, row = _cumulate_gate_rows(-a_vec * _softplus(alpha_raw + dtb))
        beta = _sigmoid(beta_raw).reshape(cum.shape)
        packed = jnp.where(row < 2, beta, jnp.where(row < 4, cum, tot))
        rows_s[0:n_chunks] = packed.reshape(n_chunks, GATE_ROWS, n)

    def intra_group(cs, with_out, between_levels=()):
        res = _dn_intra_chunks([(qs[chunk_rows(c), :], ks[chunk_rows(c), :], vs[chunk_rows(c), :], rows_s[c])
                                for c in cs], with_out, between_levels)
        for c, per_dir in zip(cs, res):
            for d in range(2):
                s_mul, s_add, o_mul, o_add = per_dir[d]
                lhs_s[d, c, 0:HEAD_DIM, :] = s_mul.astype(BF16)
                add_s[d, c, 0:HEAD_DIM, :] = s_add
                if with_out:
                    lhs_s[d, c, HEAD_DIM:, :] = o_mul.astype(BF16)
                    add_s[d, c, HEAD_DIM:, :] = o_add

    def state_steps(cf, cb, with_out):
        dc = ((0, cf), (1, cb))
        rows = slice(None) if with_out else slice(0, HEAD_DIM)
        ss = [s_ref[d] for d, _ in dc]
        rs = [jnp.dot(lhs_s[d, c, rows, :], s.astype(BF16), preferred_element_type=F32) + add_s[d, c, rows, :]
              for (d, c), s in zip(dc, ss)]
        for (d, c), s, r in zip(dc, ss, rs):
            s_ref[d] = s * jnp.exp(rows_s[c][4 + d:5 + d, :]) + r[:HEAD_DIM]
        return [r[HEAD_DIM:] if with_out else None for r in rs]

    def finalize(o, r):
        ms = jnp.mean(o * o, axis=-1, keepdims=True)
        return o * lax.rsqrt(ms + NORM_EPS) * nw_ref[...] * _silu(gate_ref[0, r, :])

    s_ref[...] = jnp.zeros_like(s_ref)
    prep((qc_ref, kc_ref, vc_ref), t_ctx)
    gate_rows(gc_ref, nc_ctx)
    for g in range(nc_ctx // ctx_group):
        intra_group([g * ctx_group + j for j in range(ctx_group)], False)

    def ctx_body(i, carry):
        state_steps(i, nc_ctx - 1 - i, False)
        return carry
    lax.fori_loop(0, nc_ctx, ctx_body, 0)

    prep((ql_ref, kl_ref, vl_ref), t_lat)
    gate_rows(gl_ref, nc_lat)
    def first_visit(i):
        cb = nc_lat - 1 - i
        o_f, o_b = state_steps(i, cb, True)
        obuf[chunk_rows(i), :] = o_f
        obuf[chunk_rows(cb), :] = o_b

    side = lat_group // 2
    n_groups = half // side

    def group_chunks(g):
        return [g * side + j for j in range(side)] + [nc_lat - 1 - g * side - j for j in range(side)]

    def group_steps(g):
        return [functools.partial(first_visit, g * side + j) for j in range(side)]

    intra_group(group_chunks(0), True)

    def group_body(g, carry):
        intra_group(group_chunks(g), True, group_steps(g - 1))
        return carry
    lax.fori_loop(1, n_groups, group_body, 0)
    for step in group_steps(n_groups - 1):
        step()

    def second_body(i, carry):
        cb = nc_lat - 1 - i
        o_f, o_b = state_steps(i, cb, True)
        for c, o in ((i, o_f), (cb, o_b)):
            r = chunk_rows(c)
            y_ref[0, r, :] = finalize(o + obuf[r, :], r).astype(y_ref.dtype)
        return carry

    lax.fori_loop(half, nc_lat, second_body, 0)


def _deltanet(p_lat, p_ctx, g_lat, g_ctx, scalars, dn_conv, dn_norm):
    b, t_lat, _ = p_lat.shape
    t_ctx = p_ctx.shape[1]
    n = DN_CHUNK
    nc_lat, nc_ctx = t_lat // n, t_ctx // n
    assert n == HEAD_DIM and nc_lat % 16 == 0 and t_lat % 256 == 0 and t_ctx % 256 == 0 and t_ctx <= t_lat
    h = N_HEADS
    col = lambda off: (lambda i, j: (i, 0, off + j))
    lat_spec = lambda off: pl.BlockSpec((1, t_lat, HEAD_DIM), col(off))
    ctx_spec = lambda off: pl.BlockSpec((1, t_ctx, HEAD_DIM), col(off))
    conv_spec = lambda off: pl.BlockSpec((CONV_K, HEAD_DIM), lambda i, j: (0, off + j))
    gate_spec = lambda nc: pl.BlockSpec((1, 1, 2, nc, GATE_ROWS, n), lambda i, j: (i, j, 0, 0, 0, 0))
    seq = lambda dt: pltpu.VMEM((t_lat, HEAD_DIM), dt)
    step_terms = lambda dt: pltpu.VMEM((2, nc_lat, HEAD_DIM + n, HEAD_DIM), dt)
    return pl.pallas_call(
        _dn_kernel,
        grid=(b, h),
        in_specs=[pl.BlockSpec(memory_space=pltpu.SMEM),
                  lat_spec(0), lat_spec(h), lat_spec(2 * h), lat_spec(3 * h),
                  ctx_spec(0), ctx_spec(h), ctx_spec(2 * h),
                  gate_spec(nc_lat), gate_spec(nc_ctx),
                  conv_spec(0), conv_spec(h), conv_spec(2 * h),
                  pl.BlockSpec((1, HEAD_DIM), lambda i, j: (0, 0))],
        out_specs=pl.BlockSpec((1, t_lat, HEAD_DIM), lambda i, j: (i, 0, j)),
        out_shape=jax.ShapeDtypeStruct((b, t_lat, h * HEAD_DIM), BF16),
        scratch_shapes=[seq(F32), seq(F32), seq(F32),
                        pltpu.VMEM((nc_lat, GATE_ROWS, n), F32),
                        step_terms(BF16), step_terms(F32),
                        seq(F32),
                        pltpu.VMEM((2, HEAD_DIM, HEAD_DIM), F32)],
        compiler_params=pltpu.CompilerParams(
            dimension_semantics=("arbitrary", "arbitrary"), vmem_limit_bytes=VMEM_LIMIT),
        name="deltanet_scan",
    )(scalars, p_lat, p_lat, p_lat, p_lat, p_ctx, p_ctx, p_ctx, g_lat, g_ctx,
      dn_conv, dn_conv, dn_conv, dn_norm.reshape(1, HEAD_DIM))


def _ml_group(problems, states, with_out):
    n = problems[0][1].shape[0]
    ii, jj = _iota2(n)
    eye = ii == jj
    ones_blk = jnp.ones((n, HEAD_DIM), BF16)

    def lane_spread(rows):
        splits = [_split2(jnp.where(eye, row, 0.0)) for row in rows]
        return [sum(jnp.dot(t, ones_blk, preferred_element_type=F32) for t in sp) for sp in splits]

    a_rows = [x[d:d + 1, :] for d, _, _, _, x in problems]
    a_spreads = lane_spread(a_rows)
    b_spreads = lane_spread([x[2 + d:3 + d, :] for d, _, _, _, x in problems]) if with_out else None
    ms = [m for _, m in states]
    chain = []
    for (d, _, _, _, x), a_r in zip(problems, a_rows):
        mx = jnp.maximum(ms[d], jnp.max(a_r, axis=1, keepdims=True))
        chain.append((ms[d], mx))
        ms[d] = x[4 + d:5 + d, 0:1] + mx
    pre = []
    for (d, q, k, v, x), a_s, (_, mx) in zip(problems, a_spreads, chain):
        v_ext = jnp.concatenate([v.astype(BF16), ones_blk], axis=1)
        pre.append((v_ext, (k * jnp.exp(a_s - mx)).astype(BF16)))
    ups = [lax.dot_general(kw, v_ext, TN_DIMS, preferred_element_type=F32) for v_ext, kw in pre]
    if with_out:
        cm_rows = [jnp.max(jnp.where(_masks(d != 0, n)[0], a_s[:, :n], NEG_BIG), axis=0, keepdims=True)
                   for (d, _, _, _, _), a_s in zip(problems, a_spreads)]
        cm_spreads = lane_spread(cm_rows)
        qks = [_dot_nt(q, k) for _, q, k, _, _ in problems]
        scores = []
        for (d, _, _, _, _), a_r, cm_s, qk in zip(problems, a_rows, cm_spreads, qks):
            incl, _ = _masks(d == 0, n)
            expo = jnp.where(incl, jnp.broadcast_to(a_r, (n, n)) - cm_s[:, :n], NEG_BIG)
            scores.append((qk * jnp.exp(expo)).astype(BF16))
        intra = [jnp.dot(s, v_ext, preferred_element_type=F32) for s, (v_ext, _) in zip(scores, pre)]
    cs = [c_ext for c_ext, _ in states]
    starts = []
    for (d, _, _, _, _), (m, mx), up in zip(problems, chain, ups):
        starts.append((cs[d], m))
        cs[d] = jnp.exp(m - mx) * cs[d] + up
    states = list(zip(cs, ms))
    if not with_out:
        return states, [None] * len(problems)
    inter = [_dot(q, c_ext) for (_, q, _, _, _), (c_ext, _) in zip(problems, starts)]
    hs = []
    for (_, m), cm_s, b_s, qc, sv in zip(starts, cm_spreads, b_spreads, inter, intra):
        mm = jnp.maximum(m, cm_s)
        w_inter = jnp.exp(m - mm)
        w_intra = jnp.exp(cm_s - mm)
        num = w_inter * qc[:, :HEAD_DIM] + w_intra * sv[:, :HEAD_DIM]
        den = w_inter * qc[:, HEAD_DIM:] + w_intra * sv[:, HEAD_DIM:]
        hs.append(num / jnp.maximum(jnp.abs(den), jnp.exp(-(b_s + mm))))
    return states, hs


def _ml_kernel(sc_ref, ql_ref, kl_ref, vl_ref, og_ref, qc_ref, kc_ref, vc_ref,
               gl_ref, gc_ref, nw_ref, y_ref, rl, rc, obuf, c_ref, m_ref):
    head = pl.program_id(1)
    t_lat = ql_ref.shape[1]
    t_ctx = qc_ref.shape[1]
    nc_lat = t_lat // ML_CHUNK
    nc_ctx = t_ctx // ML_CHUNK
    half = nc_lat // 2
    k_scale = HEAD_DIM ** -0.5

    igb_f, igb_b = sc_ref[head, 0], sc_ref[head, 1]
    fgb_f, fgb_b = sc_ref[head, 2], sc_ref[head, 3]

    def gate_rows(g_ref, dst, n_chunks):
        ig_raw = g_ref[0, 0, 0]
        fg_raw = g_ref[0, 0, 1]
        par = _row_parity(fg_raw.shape)
        lf = -_softplus(-(fg_raw + jnp.where(par == 0, fgb_f, fgb_b)))
        cum, tot, row = _cumulate_gate_rows(lf)
        ic = (ig_raw + jnp.where(par == 0, igb_f, igb_b)).reshape(n_chunks * GATE_ROWS, ML_CHUNK)
        packed = jnp.where(row < 2, ic - cum, jnp.where(row < 4, cum, tot))
        dst[...] = packed.reshape(n_chunks, GATE_ROWS, ML_CHUNK)

    gate_rows(gl_ref, rl, nc_lat)
    gate_rows(gc_ref, rc, nc_ctx)

    c_ref[...] = jnp.zeros_like(c_ref)
    m_ref[...] = jnp.zeros_like(m_ref)

    def run_steps(first_step, n_steps, total, load, gates, with_out):
        problems = []
        for j in range(n_steps):
            for d, c in ((0, first_step + j), (1, total - 1 - first_step - j)):
                problems.append((d, load(0, c), load(1, c) * k_scale, load(2, c), gates[c]))
        states = [(c_ref[d], m_ref[d, 0:1, 0:1]) for d in range(2)]
        states, hs = _ml_group(problems, states, with_out)
        for d, (c_ext, m) in enumerate(states):
            c_ref[d] = c_ext
            m_ref[d] = jnp.broadcast_to(m, m_ref.shape[1:])
        return hs

    ctx_refs = (qc_ref, kc_ref, vc_ref)
    lat_refs = (ql_ref, kl_ref, vl_ref)

    def chunk_rows(c):
        return pl.ds(pl.multiple_of(c * ML_CHUNK, ML_CHUNK), ML_CHUNK)

    def ctx_load(which, c):
        return ctx_refs[which][0, chunk_rows(c), :]

    def lat_load(which, c):
        return lat_refs[which][0, chunk_rows(c), :]

    ctx_unroll = ML_UNROLL if nc_ctx % ML_UNROLL == 0 else 1

    def ctx_body(i, carry):
        run_steps(i * ctx_unroll, ctx_unroll, nc_ctx, ctx_load, rc, False)
        return carry

    lax.fori_loop(0, nc_ctx // ctx_unroll, ctx_body, 0)

    def finalize(hh, r):
        ms = jnp.mean(hh * hh, axis=-1, keepdims=True)
        y = hh * lax.rsqrt(ms + NORM_EPS) * nw_ref[...]
        return y * _sigmoid(og_ref[0, r, :])

    def lat_body(i, second):
        first_step = i * ML_UNROLL
        hs = run_steps(first_step, ML_UNROLL, nc_lat, lat_load, rl, True)
        for j in range(ML_UNROLL):
            for d, c in ((0, first_step + j), (1, nc_lat - 1 - first_step - j)):
                hh = hs[2 * j + d]
                r = chunk_rows(c)
                if second:
                    y_ref[0, r, :] = finalize(hh + obuf[r, :], r).astype(y_ref.dtype)
                else:
                    obuf[r, :] = hh

    def first_body(i, carry):
        lat_body(i, False)
        return carry

    def second_body(i, carry):
        lat_body(i, True)
        return carry

    lax.fori_loop(0, half // ML_UNROLL, first_body, 0)
    lax.fori_loop(half // ML_UNROLL, nc_lat // ML_UNROLL, second_body, 0)


def _mlstm(p_lat, p_ctx, g_lat, g_ctx, scalars, ml_norm):
    b, t_lat, _ = p_lat.shape
    t_ctx = p_ctx.shape[1]
    nc_lat, nc_ctx = t_lat // ML_CHUNK, t_ctx // ML_CHUNK
    assert nc_lat % (2 * ML_UNROLL) == 0 and ML_CHUNK % (t_lat // GRID_W) == 0
    h = N_HEADS
    lat_spec = lambda off: pl.BlockSpec((1, t_lat, HEAD_DIM), lambda i, j: (i, 0, off + j))
    ctx_spec = lambda off: pl.BlockSpec((1, t_ctx, HEAD_DIM), lambda i, j: (i, 0, off + j))
    gate_spec = lambda nc: pl.BlockSpec((1, 1, 2, nc, GATE_ROWS, ML_CHUNK), lambda i, j: (i, j, 0, 0, 0, 0))
    return pl.pallas_call(
        _ml_kernel,
        grid=(b, h),
        in_specs=[pl.BlockSpec(memory_space=pltpu.SMEM),
                  lat_spec(0), lat_spec(h), lat_spec(2 * h), lat_spec(3 * h),
                  ctx_spec(0), ctx_spec(h), ctx_spec(2 * h),
                  gate_spec(nc_lat), gate_spec(nc_ctx),
                  pl.BlockSpec((1, HEAD_DIM), lambda i, j: (0, j))],
        out_specs=pl.BlockSpec((1, t_lat, HEAD_DIM), lambda i, j: (i, 0, j)),
        out_shape=jax.ShapeDtypeStruct((b, t_lat, h * HEAD_DIM), F32),
        scratch_shapes=[pltpu.VMEM((nc_lat, GATE_ROWS, ML_CHUNK), F32), pltpu.VMEM((nc_ctx, GATE_ROWS, ML_CHUNK), F32),
                        pltpu.VMEM((t_lat, HEAD_DIM), F32), pltpu.VMEM((2, HEAD_DIM, 2 * HEAD_DIM), F32),
                        pltpu.VMEM((2, 8, HEAD_DIM), F32)],
        compiler_params=pltpu.CompilerParams(
            dimension_semantics=("arbitrary", "arbitrary"), vmem_limit_bytes=VMEM_LIMIT),
        name="mlstm_scan",
    )(scalars, p_lat, p_lat, p_lat, p_lat, p_ctx, p_ctx, p_ctx, g_lat, g_ctx,
      ml_norm.reshape(1, h * HEAD_DIM))


def _ffn_kernel(x_ref, ydn_ref, yml_ref, mod_ref, n2_ref, fn_ref, wo_dn_ref, wo_ml_ref,
                wg_ref, wu_ref, wd_ref, o_ref, h_ref, acc_ref, *, f_tile):
    yml = jnp.swapaxes(yml_ref[0], 0, 1).reshape(x_ref.shape[1], yml_ref.shape[3]).astype(BF16)
    mix = (jnp.dot(ydn_ref[0], wo_dn_ref[...], preferred_element_type=F32)
           + jnp.dot(yml, wo_ml_ref[...], preferred_element_type=F32))
    x1 = x_ref[0] + mod_ref[0, 2:3, :] * mix
    h_ref[...] = _rms_mod(x1, n2_ref[...], mod_ref[0, 4:5, :], mod_ref[0, 3:4, :]).astype(BF16)
    acc_ref[...] = x1
    g2 = mod_ref[0, 5:6, :]
    d_ff = wg_ref.shape[1]
    for j in range(d_ff // f_tile):
        sl = slice(j * f_tile, (j + 1) * f_tile)
        gate = jnp.dot(h_ref[...], wg_ref[:, sl], preferred_element_type=F32)
        up = jnp.dot(h_ref[...], wu_ref[:, sl], preferred_element_type=F32)
        act = (_silu(gate) * up).astype(BF16)
        acc_ref[...] += g2 * jnp.dot(act, wd_ref[sl, :], preferred_element_type=F32)
    x2 = acc_ref[...]
    ms = jnp.mean(x2 * x2, axis=-1, keepdims=True)
    o_ref[0] = x2 * lax.rsqrt(ms + NORM_EPS) * fn_ref[...]


def _out_ffn(x, y_dn, y_ml, mod, norm2, final_norm, wo_dn, wo_ml, w_gate, w_up, w_down, tm):
    b, t, d = x.shape
    d_mix = y_dn.shape[2]
    d_ff = w_gate.shape[1]
    assert tm % GRID_W == 0 and (tm // GRID_W) % 8 == 0
    const = lambda shape: pl.BlockSpec(shape, lambda i, j: (0,) * len(shape),
                                       pipeline_mode=pl.Buffered(1))
    kern = functools.partial(_ffn_kernel, f_tile=256)
    return pl.pallas_call(
        kern,
        grid=(b, t // tm),
        in_specs=[pl.BlockSpec((1, tm, d), lambda i, j: (i, j, 0)),
                  pl.BlockSpec((1, tm, d_mix), lambda i, j: (i, j, 0)),
                  pl.BlockSpec((1, GRID_W, tm // GRID_W, d_mix), lambda i, j: (i, 0, j, 0)),
                  pl.BlockSpec((1, N_MOD, d), lambda i, j: (i, 0, 0)),
                  const((1, d)), const((1, d)),
                  const((d_mix, d)), const((d_mix, d)),
                  const((d, d_ff)), const((d, d_ff)), const((d_ff, d))],
        out_specs=pl.BlockSpec((1, tm, d), lambda i, j: (i, j, 0)),
        out_shape=jax.ShapeDtypeStruct((b, t, d), F32),
        scratch_shapes=[pltpu.VMEM((tm, d), BF16), pltpu.VMEM((tm, d), F32)],
        compiler_params=pltpu.CompilerParams(
            dimension_semantics=("arbitrary", "arbitrary"), vmem_limit_bytes=VMEM_LIMIT),
        name="out_ffn",
    )(x, y_dn, y_ml, mod, norm2.reshape(1, d), final_norm.reshape(1, d),
      wo_dn, wo_ml, w_gate, w_up, w_down)


def _gate_weight_rows(w_in, d_group):
    h = N_HEADS
    cols = []
    for mixer in range(2):
        base = mixer * (4 * d_group + 4 * h) + 4 * d_group
        for head in range(h):
            for slab in range(2):
                cols += [base + slab * 2 * h + head, base + slab * 2 * h + h + head]
    return w_in[:, jnp.array(cols)].T


def _chunk_major_gates(gt, chunk, col_major):
    b, _, t = gt.shape
    if col_major:
        rows = t // GRID_W
        cols = chunk // rows
        g = gt.reshape(b, N_HEADS, 2, 2, rows, GRID_W // cols, cols)
        g = g.transpose(0, 1, 2, 5, 3, 6, 4).reshape(b, N_HEADS, 2, t // chunk, 2, chunk)
    else:
        g = gt.reshape(b, N_HEADS, 2, 2, t // chunk, chunk).transpose(0, 1, 2, 4, 3, 5)
    return jnp.tile(g, (1, 1, 1, 1, GATE_ROWS // 2, 1))


def kernel(x, c, ctx, c_ctx, w_mod, b_mod, norm1, w_in, dn_conv, dn_a_log, dn_dt_bias, dn_norm,
           ml_ig_bias, ml_fg_bias, ml_norm, w_out, norm2, w_ffn_in, w_ffn_out, final_norm):
    depth = w_mod.shape[0]
    assert depth == 1, "context outputs are only skipped for a single layer"
    b, t_lat, d = x.shape
    h = N_HEADS
    d_group = h * HEAD_DIM
    d_ff = w_ffn_out.shape[1]
    layer = 0

    pad_rows = -(b + 1) % 8
    cc = jnp.concatenate([c, c_ctx[None, :], jnp.zeros((pad_rows, d), F32)], axis=0)
    mod = _modulation(cc, w_mod[layer], b_mod[layer])
    mod_lat = mod[:, :b].transpose(1, 0, 2)
    mod_ctx = jnp.broadcast_to(mod[:, b][None], (b, N_MOD, d))

    w = w_in[layer]
    dn_cols = 4 * d_group + 4 * h
    w_dn = w[:, :4 * d_group].astype(BF16)
    w_ml = w[:, dn_cols:dn_cols + 4 * d_group].astype(BF16)
    w_gate_t = _gate_weight_rows(w, d_group).astype(BF16)
    pdn_lat, pml_lat, gt_lat = _in_projection(x, mod_lat, norm1[layer], w_dn, w_ml, w_gate_t,
                                              tm=512, col_major=True)
    pdn_ctx, pml_ctx, gt_ctx = _in_projection(ctx, mod_ctx, norm1[layer], w_dn, w_ml, w_gate_t,
                                              tm=ctx.shape[1], col_major=False)
    pml_lat = pml_lat.reshape(b, t_lat, 4 * d_group)

    dn_gate_rows = h * 2 * 2
    g_dn_lat = _chunk_major_gates(gt_lat[:, :dn_gate_rows], DN_CHUNK, False)
    g_dn_ctx = _chunk_major_gates(gt_ctx[:, :dn_gate_rows], DN_CHUNK, False)
    g_ml_lat = _chunk_major_gates(gt_lat[:, dn_gate_rows:], ML_CHUNK, True)
    g_ml_ctx = _chunk_major_gates(gt_ctx[:, dn_gate_rows:], ML_CHUNK, False)

    dn_scal = jnp.concatenate([dn_a_log[layer].T, dn_dt_bias[layer].T], axis=1)
    ml_scal = jnp.concatenate([ml_ig_bias[layer].T, ml_fg_bias[layer].T], axis=1)

    y_dn = _deltanet(pdn_lat, pdn_ctx, g_dn_lat, g_dn_ctx, dn_scal, dn_conv[layer], dn_norm[layer])
    y_ml = _mlstm(pml_lat, pml_ctx, g_ml_lat, g_ml_ctx, ml_scal, ml_norm[layer])
    y_ml = y_ml.reshape(b, GRID_W, t_lat // GRID_W, d_group)

    wo = w_out[layer].astype(BF16)
    wf = w_ffn_in[layer].astype(BF16)
    return _out_ffn(x, y_dn, y_ml, mod_lat, norm2[layer], final_norm,
                    wo[:d_group], wo[d_group:], wf[:, :d_ff], wf[:, d_ff:],
                    w_ffn_out[layer].astype(BF16), tm=512)
```

```python
import functools

import jax
import jax.numpy as jnp
from jax import lax
from jax.experimental import pallas as pl
from jax.experimental.pallas import tpu as pltpu

F32 = jnp.float32
BF16 = jnp.bfloat16

DN_CHUNK = 128
ML_CHUNK = 128
ML_UNROLL = 8
GRID_W = 64
HEAD_DIM = 128
N_HEADS = 4
CONV_K = 5
NORM_EPS = 1e-6
N_MOD = 6
GATE_ROWS = 8
NEG_BIG = -1e30
VMEM_LIMIT = 56 * 1024 * 1024

NT_DIMS = (((1,), (1,)), ((), ()))
TN_DIMS = (((0,), (0,)), ((), ()))


def _dot(a, b):
    return jnp.dot(a.astype(BF16), b.astype(BF16), preferred_element_type=F32)


def _dot_nt(a, b):
    return lax.dot_general(a.astype(BF16), b.astype(BF16), NT_DIMS, preferred_element_type=F32)


def _split3(x):
    hi = x.astype(BF16)
    r1 = x - hi.astype(F32)
    mid = r1.astype(BF16)
    lo = (r1 - mid.astype(F32)).astype(BF16)
    return hi, mid, lo


def _split2(x):
    hi = x.astype(BF16)
    return hi, (x - hi.astype(F32)).astype(BF16)


def _dot_exact_rhs(x, m_bf16):
    hi, mid, lo = _split3(x)
    f = lambda t: jnp.dot(t, m_bf16, preferred_element_type=F32)
    return f(hi) + f(mid) + f(lo)


def _softplus(x):
    return jnp.maximum(x, 0.0) + jnp.log(1.0 + jnp.exp(-jnp.abs(x)))


def _sigmoid(x):
    return 1.0 / (1.0 + jnp.exp(-x))


def _silu(x):
    return x * _sigmoid(x)


def _mod_kernel(c_ref, w_ref, b_ref, o_ref):
    sc = _silu(c_ref[...])
    o_ref[0] = jnp.dot(sc, w_ref[...], preferred_element_type=F32,
                       precision=lax.Precision.HIGHEST) + b_ref[0]


def _modulation(cc, w_mod, b_mod):
    rows, d = cc.shape
    return pl.pallas_call(
        _mod_kernel,
        grid=(N_MOD,),
        in_specs=[pl.BlockSpec((rows, d), lambda j: (0, 0)),
                  pl.BlockSpec((d, d), lambda j: (0, j)),
                  pl.BlockSpec((1, 1, d), lambda j: (j, 0, 0))],
        out_specs=pl.BlockSpec((1, rows, d), lambda j: (j, 0, 0)),
        out_shape=jax.ShapeDtypeStruct((N_MOD, rows, d), F32),
        compiler_params=pltpu.CompilerParams(vmem_limit_bytes=VMEM_LIMIT),
        name="modulation",
    )(cc, w_mod, b_mod.reshape(N_MOD, 1, d))


def _rms_mod(x, gain, scale, shift):
    ms = jnp.mean(x * x, axis=-1, keepdims=True)
    return (x * lax.rsqrt(ms + NORM_EPS) * gain) * (1.0 + scale) + shift


def _inproj_kernel(x_ref, mod_ref, n1_ref, wdn_ref, wml_ref, wg_ref, pdn_ref, pml_ref, gt_ref,
                   h_ref, hcm_ref, *, n_tile, col_major):
    h = _rms_mod(x_ref[0], n1_ref[...], mod_ref[0, 1:2, :], mod_ref[0, 0:1, :])
    h_ref[...] = h.astype(BF16)
    for j in range(wdn_ref.shape[1] // n_tile):
        cols = slice(j * n_tile, (j + 1) * n_tile)
        pdn_ref[0, :, cols] = jnp.dot(h_ref[...], wdn_ref[:, cols], preferred_element_type=F32)
    if col_major:
        tm, d = h.shape
        hcm_ref[...] = jnp.swapaxes(h.reshape(tm // GRID_W, GRID_W, d), 0, 1).reshape(tm, d).astype(BF16)
    lhs_ref = hcm_ref if col_major else h_ref
    for j in range(wml_ref.shape[1] // n_tile):
        cols = slice(j * n_tile, (j + 1) * n_tile)
        res = jnp.dot(lhs_ref[...], wml_ref[:, cols], preferred_element_type=F32)
        if col_major:
            pml_ref[0, :, :, cols] = res.reshape(pml_ref.shape[1], pml_ref.shape[2], n_tile)
        else:
            pml_ref[0, :, cols] = res
    gt_ref[0] = lax.dot_general(wg_ref[...], h_ref[...], NT_DIMS, preferred_element_type=F32)


def _in_projection(x, mod, norm1, w_dn, w_ml, w_gate_t, tm, col_major):
    b, t, d = x.shape
    n_dn, n_ml = w_dn.shape[1], w_ml.shape[1]
    n_gate = w_gate_t.shape[0]
    kern = functools.partial(_inproj_kernel, n_tile=512, col_major=col_major)
    if col_major:
        assert tm % GRID_W == 0 and (tm // GRID_W) % 8 == 0 and t % tm == 0
        rows = t // GRID_W
        ml_spec = pl.BlockSpec((1, GRID_W, tm // GRID_W, n_ml), lambda i, j: (i, 0, j, 0))
        ml_shape = jax.ShapeDtypeStruct((b, GRID_W, rows, n_ml), F32)
    else:
        ml_spec = pl.BlockSpec((1, tm, n_ml), lambda i, j: (i, j, 0))
        ml_shape = jax.ShapeDtypeStruct((b, t, n_ml), F32)
    const = lambda shape: pl.BlockSpec(shape, lambda i, j: (0,) * len(shape), pipeline_mode=pl.Buffered(1))
    return pl.pallas_call(
        kern,
        grid=(b, t // tm),
        in_specs=[pl.BlockSpec((1, tm, d), lambda i, j: (i, j, 0)),
                  pl.BlockSpec((1, N_MOD, d), lambda i, j: (i, 0, 0)),
                  const((1, d)), const((d, n_dn)), const((d, n_ml)), const((n_gate, d))],
        out_specs=[pl.BlockSpec((1, tm, n_dn), lambda i, j: (i, j, 0)),
                   ml_spec,
                   pl.BlockSpec((1, n_gate, tm), lambda i, j: (i, 0, j))],
        out_shape=[jax.ShapeDtypeStruct((b, t, n_dn), F32),
                   ml_shape,
                   jax.ShapeDtypeStruct((b, n_gate, t), F32)],
        scratch_shapes=[pltpu.VMEM((tm, d), BF16), pltpu.VMEM((tm, d), BF16)],
        compiler_params=pltpu.CompilerParams(
            dimension_semantics=("arbitrary", "arbitrary"), vmem_limit_bytes=VMEM_LIMIT),
        name="in_projection",
    )(x, mod, norm1.reshape(1, d), w_dn, w_ml, w_gate_t)


def _iota2(n):
    return (lax.broadcasted_iota(jnp.int32, (n, n), 0), lax.broadcasted_iota(jnp.int32, (n, n), 1))


def _to_columns(x):
    ii, jj = _iota2(x.shape[1])
    eye = (ii == jj).astype(BF16)
    hi, mid, lo = _split3(x)
    f = lambda t: lax.dot_general(eye, t, NT_DIMS, preferred_element_type=F32)
    return f(hi) + f(mid) + f(lo)


def _masks(fwd, n):
    ii, jj = _iota2(n)
    if fwd:
        return ii >= jj, ii > jj
    return ii <= jj, ii < jj


def _row_parity(shape):
    return lax.broadcasted_iota(jnp.int32, shape, len(shape) - 2) % 2


def _cumulate_gate_rows(logdecay):
    n_chunks, _, n = logdecay.shape
    ii, jj = _iota2(n)
    prefix = (ii <= jj).astype(BF16)
    suffix = (ii >= jj).astype(BF16)
    ones = jnp.ones((n, n), BF16)
    flat = logdecay.reshape(n_chunks * GATE_ROWS, n)
    par = _row_parity(flat.shape)
    cum = jnp.where(par == 0, _dot_exact_rhs(flat, prefix), _dot_exact_rhs(flat, suffix))
    tot = _dot_exact_rhs(flat, ones)
    row = lax.broadcasted_iota(jnp.int32, flat.shape, 0) % GATE_ROWS
    return cum, tot, row


def _conv_block(u_ref, w, t0, rows, total):
    main = u_ref[0, pl.ds(t0, rows), :]
    lo = jnp.maximum(t0 - 8, 0)
    hi = jnp.minimum(t0 + rows, total - 8)
    prev = jnp.where(t0 > 0, u_ref[0, pl.ds(pl.multiple_of(lo, 8), 8), :], 0.0)
    nxt = jnp.where(t0 + rows < total, u_ref[0, pl.ds(pl.multiple_of(hi, 8), 8), :], 0.0)
    ext = jnp.concatenate([prev, main, nxt], axis=0)
    acc = None
    for j in range(CONV_K):
        off = 8 + j - CONV_K // 2
        term = ext[off:off + rows, :] * w[j:j + 1, :]
        acc = term if acc is None else acc + term
    return _silu(acc)


def _conv_block_inner(u_ref, w, t0, rows):
    acc = None
    for j in range(CONV_K):
        term = u_ref[0, pl.ds(t0 + (j - CONV_K // 2), rows), :] * w[j:j + 1, :]
        acc = term if acc is None else acc + term
    return _silu(acc)


def _l2norm(x):
    return x * lax.rsqrt(jnp.sum(x * x, axis=-1, keepdims=True) + NORM_EPS)


def _unit_tri_inverses(a_list, between_levels=()):
    n = a_list[0].shape[0]
    ii, jj = _iota2(n)
    eye = (ii == jj).astype(F32)
    pair = (ii >> 1) == (jj >> 1)
    ts = [eye - jnp.where(pair, a, 0.0) for a in a_list]
    for level in range(1, n.bit_length() - 1):
        same_big = (ii >> (level + 1)) == (jj >> (level + 1))
        same_small = (ii >> level) == (jj >> level)
        couple = same_big & jnp.logical_not(same_small)
        es = [jnp.where(couple, a, 0.0).astype(BF16) for a in a_list]
        tbs = [t.astype(BF16) for t in ts]
        tes = [jnp.dot(tb, e, preferred_element_type=F32) for tb, e in zip(tbs, es)]
        ts = [t - jnp.dot(te.astype(BF16), tb, preferred_element_type=F32)
              for t, te, tb in zip(ts, tes, tbs)]
        if level <= len(between_levels):
            between_levels[level - 1]()
    return ts


def _dn_intra_chunks(qkvx, with_out, between_levels=()):
    n = qkvx[0][0].shape[0]
    cols = [_to_columns(x) for _, _, _, x in qkvx]
    kks = [_dot_nt(k, k) for _, k, _, _ in qkvx]
    qks = [_dot_nt(q, k) if with_out else None for q, k, _, _ in qkvx]
    parts = []
    for (q, k, v, x), col, kk in zip(qkvx, cols, kks):
        for d in range(2):
            beta_c = col[:, d:d + 1]
            g_c = col[:, 2 + d:3 + d]
            tot_c = col[:, 4 + d:5 + d]
            g_r = x[2 + d:3 + d, :]
            incl, strict = _masks(d == 0, n)
            decay = jnp.exp(jnp.where(incl, g_c - g_r, NEG_BIG))
            e_g = jnp.exp(g_c)
            a = jnp.where(strict, kk * beta_c * decay, 0.0)
            rhs = jnp.concatenate([k * (beta_c * e_g), v * beta_c], axis=1).astype(BF16)
            parts.append((a, rhs, decay, e_g, tot_c - g_c))
    ts = _unit_tri_inverses([p[0] for p in parts], between_levels)
    wus = [jnp.dot(t.astype(BF16), p[1], preferred_element_type=F32).astype(BF16) for t, p in zip(ts, parts)]
    k_tail_ts = [(qkvx[i // 2][1] * jnp.exp(p[4])).T.astype(BF16) for i, p in enumerate(parts)]
    state_terms = [jnp.dot(kt, wu, preferred_element_type=F32) for kt, wu in zip(k_tail_ts, wus)]
    if with_out:
        scores = [(qks[i // 2] * p[2]).astype(BF16) for i, p in enumerate(parts)]
        out_terms = [jnp.dot(sc, wu, preferred_element_type=F32) for sc, wu in zip(scores, wus)]
    res = []
    for ci, (q, k, v, x) in enumerate(qkvx):
        per_dir = []
        for d in range(2):
            i = 2 * ci + d
            st = state_terms[i]
            if with_out:
                ot = out_terms[i]
                per_dir.append((-st[:, :HEAD_DIM], st[:, HEAD_DIM:],
                                q * parts[i][3] - ot[:, :HEAD_DIM], ot[:, HEAD_DIM:]))
            else:
                per_dir.append((-st[:, :HEAD_DIM], st[:, HEAD_DIM:], None, None))
        res.append(per_dir)
    return res


def _dn_kernel(sc_ref, ql_ref, kl_ref, vl_ref, gate_ref, qc_ref, kc_ref, vc_ref,
               gl_ref, gc_ref, wq_ref, wk_ref, wv_ref, nw_ref, y_ref,
               qs, ks, vs, rows_s, lhs_s, add_s, obuf, s_ref):
    head = pl.program_id(1)
    n = DN_CHUNK
    t_lat = ql_ref.shape[1]
    t_ctx = qc_ref.shape[1]
    nc_lat = t_lat // n
    nc_ctx = t_ctx // n
    half = nc_lat // 2
    conv_rows = 256
    ctx_group = 2 if nc_ctx % 2 == 0 else 1
    lat_group = 8 if nc_lat % 16 == 0 else 2

    a_log_f, a_log_b = sc_ref[head, 0], sc_ref[head, 1]
    dtb_f, dtb_b = sc_ref[head, 2], sc_ref[head, 3]

    def chunk_rows(c):
        return pl.ds(pl.multiple_of(c * n, n), n)

    def prep(src_refs, total):
        def block(t0, conv):
            q = _l2norm(conv(src_refs[0], wq_ref[...], t0))
            qs[pl.ds(t0, conv_rows), :] = q * (HEAD_DIM ** -0.5)
            ks[pl.ds(t0, conv_rows), :] = _l2norm(conv(src_refs[1], wk_ref[...], t0))
            vs[pl.ds(t0, conv_rows), :] = conv(src_refs[2], wv_ref[...], t0)

        edge = lambda u_ref, w, t0: _conv_block(u_ref, w, t0, conv_rows, total)
        inner = lambda u_ref, w, t0: _conv_block_inner(u_ref, w, t0, conv_rows)
        n_blocks = total // conv_rows
        block(0, edge)
        if n_blocks > 1:
            block((n_blocks - 1) * conv_rows, edge)

        def body(i, carry):
            block(pl.multiple_of(i * conv_rows, conv_rows), inner)
            return carry
        lax.fori_loop(1, n_blocks - 1, body, 0)

    def gate_rows(g_ref, n_chunks):
        beta_raw = g_ref[0, 0, 0]
        alpha_raw = g_ref[0, 0, 1]
        par = _row_parity(alpha_raw.shape)
        a_vec = jnp.exp(jnp.where(par == 0, a_log_f, a_log_b))
        dtb = jnp.where(par == 0, dtb_f, dtb_b)
        cum, tot, row = _cumulate_gate_rows(-a_vec * _softplus(alpha_raw + dtb))
        beta = _sigmoid(beta_raw).reshape(cum.shape)
        packed = jnp.where(row < 2, beta, jnp.where(row < 4, cum, tot))
        rows_s[0:n_chunks] = packed.reshape(n_chunks, GATE_ROWS, n)

    def intra_group(cs, with_out, between_levels=()):
        res = _dn_intra_chunks([(qs[chunk_rows(c), :], ks[chunk_rows(c), :], vs[chunk_rows(c), :], rows_s[c])
                                for c in cs], with_out, between_levels)
        for c, per_dir in zip(cs, res):
            for d in range(2):
                s_mul, s_add, o_mul, o_add = per_dir[d]
                lhs_s[d, c, 0:HEAD_DIM, :] = s_mul.astype(BF16)
                add_s[d, c, 0:HEAD_DIM, :] = s_add
                if with_out:
                    lhs_s[d, c, HEAD_DIM:, :] = o_mul.astype(BF16)
                    add_s[d, c, HEAD_DIM:, :] = o_add

    def state_steps(cf, cb, with_out):
        dc = ((0, cf), (1, cb))
        rows = slice(None) if with_out else slice(0, HEAD_DIM)
        ss = [s_ref[d] for d, _ in dc]
        rs = [jnp.dot(lhs_s[d, c, rows, :], s.astype(BF16), preferred_element_type=F32) + add_s[d, c, rows, :]
              for (d, c), s in zip(dc, ss)]
        for (d, c), s, r in zip(dc, ss, rs):
            s_ref[d] = s * jnp.exp(rows_s[c][4 + d:5 + d, :]) + r[:HEAD_DIM]
        return [r[HEAD_DIM:] if with_out else None for r in rs]

    def finalize(o, r):
        ms = jnp.mean(o * o, axis=-1, keepdims=True)
        return o * lax.rsqrt(ms + NORM_EPS) * nw_ref[...] * _silu(gate_ref[0, r, :])

    s_ref[...] = jnp.zeros_like(s_ref)
    prep((qc_ref, kc_ref, vc_ref), t_ctx)
    gate_rows(gc_ref, nc_ctx)
    for g in range(nc_ctx // ctx_group):
        intra_group([g * ctx_group + j for j in range(ctx_group)], False)

    def ctx_body(i, carry):
        state_steps(i, nc_ctx - 1 - i, False)
        return carry
    lax.fori_loop(0, nc_ctx, ctx_body, 0)

    prep((ql_ref, kl_ref, vl_ref), t_lat)
    gate_rows(gl_ref, nc_lat)
    def first_visit(i):
        cb = nc_lat - 1 - i
        o_f, o_b = state_steps(i, cb, True)
        obuf[chunk_rows(i), :] = o_f
        obuf[chunk_rows(cb), :] = o_b

    side = lat_group // 2
    n_groups = half // side

    def group_chunks(g):
        return [g * side + j for j in range(side)] + [nc_lat - 1 - g * side - j for j in range(side)]

    def group_steps(g):
        return [functools.partial(first_visit, g * side + j) for j in range(side)]

    intra_group(group_chunks(0), True)

    def group_body(g, carry):
        intra_group(group_chunks(g), True, group_steps(g - 1))
        return carry
    lax.fori_loop(1, n_groups, group_body, 0)
    for step in group_steps(n_groups - 1):
        step()

    def second_body(i, carry):
        cb = nc_lat - 1 - i
        o_f, o_b = state_steps(i, cb, True)
        for c, o in ((i, o_f), (cb, o_b)):
            r = chunk_rows(c)
            y_ref[0, r, :] = finalize(o + obuf[r, :], r).astype(y_ref.dtype)
        return carry

    lax.fori_loop(half, nc_lat, second_body, 0)


def _deltanet(p_lat, p_ctx, g_lat, g_ctx, scalars, dn_conv, dn_norm):
    b, t_lat, _ = p_lat.shape
    t_ctx = p_ctx.shape[1]
    n = DN_CHUNK
    nc_lat, nc_ctx = t_lat // n, t_ctx // n
    assert n == HEAD_DIM and nc_lat % 16 == 0 and t_lat % 256 == 0 and t_ctx % 256 == 0 and t_ctx <= t_lat
    h = N_HEADS
    col = lambda off: (lambda i, j: (i, 0, off + j))
    lat_spec = lambda off: pl.BlockSpec((1, t_lat, HEAD_DIM), col(off))
    ctx_spec = lambda off: pl.BlockSpec((1, t_ctx, HEAD_DIM), col(off))
    conv_spec = lambda off: pl.BlockSpec((CONV_K, HEAD_DIM), lambda i, j: (0, off + j))
    gate_spec = lambda nc: pl.BlockSpec((1, 1, 2, nc, GATE_ROWS, n), lambda i, j: (i, j, 0, 0, 0, 0))
    seq = lambda dt: pltpu.VMEM((t_lat, HEAD_DIM), dt)
    step_terms = lambda dt: pltpu.VMEM((2, nc_lat, HEAD_DIM + n, HEAD_DIM), dt)
    return pl.pallas_call(
        _dn_kernel,
        grid=(b, h),
        in_specs=[pl.BlockSpec(memory_space=pltpu.SMEM),
                  lat_spec(0), lat_spec(h), lat_spec(2 * h), lat_spec(3 * h),
                  ctx_spec(0), ctx_spec(h), ctx_spec(2 * h),
                  gate_spec(nc_lat), gate_spec(nc_ctx),
                  conv_spec(0), conv_spec(h), conv_spec(2 * h),
                  pl.BlockSpec((1, HEAD_DIM), lambda i, j: (0, 0))],
        out_specs=pl.BlockSpec((1, t_lat, HEAD_DIM), lambda i, j: (i, 0, j)),
        out_shape=jax.ShapeDtypeStruct((b, t_lat, h * HEAD_DIM), BF16),
        scratch_shapes=[seq(F32), seq(F32), seq(F32),
                        pltpu.VMEM((nc_lat, GATE_ROWS, n), F32),
                        step_terms(BF16), step_terms(F32),
                        seq(F32),
                        pltpu.VMEM((2, HEAD_DIM, HEAD_DIM), F32)],
        compiler_params=pltpu.CompilerParams(
            dimension_semantics=("arbitrary", "arbitrary"), vmem_limit_bytes=VMEM_LIMIT),
        name="deltanet_scan",
    )(scalars, p_lat, p_lat, p_lat, p_lat, p_ctx, p_ctx, p_ctx, g_lat, g_ctx,
      dn_conv, dn_conv, dn_conv, dn_norm.reshape(1, HEAD_DIM))


def _ml_group(problems, states, with_out):
    n = problems[0][1].shape[0]
    ones_blk = jnp.ones((n, HEAD_DIM), BF16)

    def lane_spread(rows):
        ones2 = jnp.ones((2, HEAD_DIM), BF16)
        return [lax.dot_general(jnp.concatenate(_split2(row), axis=0), ones2, TN_DIMS,
                                preferred_element_type=F32) for row in rows]

    a_rows = [x[d:d + 1, :] for d, _, _, _, x in problems]
    a_spreads = lane_spread(a_rows)
    b_spreads = lane_spread([x[2 + d:3 + d, :] for d, _, _, _, x in problems]) if with_out else None
    ms = [m for _, m in states]
    chain = []
    for (d, _, _, _, x), a_r in zip(problems, a_rows):
        mx = jnp.maximum(ms[d], jnp.max(a_r, axis=1, keepdims=True))
        chain.append((ms[d], mx))
        ms[d] = x[4 + d:5 + d, 0:1] + mx
    pre = []
    for (d, q, k, v, x), a_s, (_, mx) in zip(problems, a_spreads, chain):
        v_ext = jnp.concatenate([v.astype(BF16), ones_blk], axis=1)
        pre.append((v_ext, (k * jnp.exp(a_s - mx)).astype(BF16)))
    ups = [lax.dot_general(kw, v_ext, TN_DIMS, preferred_element_type=F32) for v_ext, kw in pre]
    if with_out:
        cm_rows = [jnp.max(jnp.where(_masks(d != 0, n)[0], a_s[:, :n], NEG_BIG), axis=0, keepdims=True)
                   for (d, _, _, _, _), a_s in zip(problems, a_spreads)]
        cm_spreads = lane_spread(cm_rows)
        qks = [_dot_nt(q, k) for _, q, k, _, _ in problems]
        scores = []
        for (d, _, _, _, _), a_r, cm_s, qk in zip(problems, a_rows, cm_spreads, qks):
            incl, _ = _masks(d == 0, n)
            expo = jnp.where(incl, jnp.broadcast_to(a_r, (n, n)) - cm_s[:, :n], NEG_BIG)
            scores.append((qk * jnp.exp(expo)).astype(BF16))
        intra = [jnp.dot(s, v_ext, preferred_element_type=F32) for s, (v_ext, _) in zip(scores, pre)]
    cs = [c_ext for c_ext, _ in states]
    starts = []
    for (d, _, _, _, _), (m, mx), up in zip(problems, chain, ups):
        starts.append((cs[d], m))
        cs[d] = jnp.exp(m - mx) * cs[d] + up
    states = list(zip(cs, ms))
    if not with_out:
        return states, [None] * len(problems)
    inter = [_dot(q, c_ext) for (_, q, _, _, _), (c_ext, _) in zip(problems, starts)]
    hs = []
    for (_, m), cm_s, b_s, qc, sv in zip(starts, cm_spreads, b_spreads, inter, intra):
        mm = jnp.maximum(m, cm_s)
        w_inter = jnp.exp(m - mm)
        w_intra = jnp.exp(cm_s - mm)
        num = w_inter * qc[:, :HEAD_DIM] + w_intra * sv[:, :HEAD_DIM]
        den = w_inter * qc[:, HEAD_DIM:] + w_intra * sv[:, HEAD_DIM:]
        hs.append(num / jnp.maximum(jnp.abs(den), jnp.exp(-(b_s + mm))))
    return states, hs


def _ml_kernel(sc_ref, ql_ref, kl_ref, vl_ref, og_ref, qc_ref, kc_ref, vc_ref,
               gl_ref, gc_ref, nw_ref, y_ref, rl, rc, obuf, c_ref, m_ref):
    head = pl.program_id(1)
    t_lat = ql_ref.shape[1]
    t_ctx = qc_ref.shape[1]
    nc_lat = t_lat // ML_CHUNK
    nc_ctx = t_ctx // ML_CHUNK
    half = nc_lat // 2
    k_scale = HEAD_DIM ** -0.5

    igb_f, igb_b = sc_ref[head, 0], sc_ref[head, 1]
    fgb_f, fgb_b = sc_ref[head, 2], sc_ref[head, 3]

    def gate_rows(g_ref, dst, n_chunks):
        ig_raw = g_ref[0, 0, 0]
        fg_raw = g_ref[0, 0, 1]
        par = _row_parity(fg_raw.shape)
        lf = -_softplus(-(fg_raw + jnp.where(par == 0, fgb_f, fgb_b)))
        cum, tot, row = _cumulate_gate_rows(lf)
        ic = (ig_raw + jnp.where(par == 0, igb_f, igb_b)).reshape(n_chunks * GATE_ROWS, ML_CHUNK)
        packed = jnp.where(row < 2, ic - cum, jnp.where(row < 4, cum, tot))
        dst[...] = packed.reshape(n_chunks, GATE_ROWS, ML_CHUNK)

    gate_rows(gl_ref, rl, nc_lat)
    gate_rows(gc_ref, rc, nc_ctx)

    c_ref[...] = jnp.zeros_like(c_ref)
    m_ref[...] = jnp.zeros_like(m_ref)

    def run_steps(first_step, n_steps, total, load, gates, with_out):
        problems = []
        for j in range(n_steps):
            for d, c in ((0, first_step + j), (1, total - 1 - first_step - j)):
                problems.append((d, load(0, c), load(1, c) * k_scale, load(2, c), gates[c]))
        states = [(c_ref[d], m_ref[d, 0:1, 0:1]) for d in range(2)]
        states, hs = _ml_group(problems, states, with_out)
        for d, (c_ext, m) in enumerate(states):
            c_ref[d] = c_ext
            m_ref[d] = jnp.broadcast_to(m, m_ref.shape[1:])
        return hs

    ctx_refs = (qc_ref, kc_ref, vc_ref)
    lat_refs = (ql_ref, kl_ref, vl_ref)

    def chunk_rows(c):
        return pl.ds(pl.multiple_of(c * ML_CHUNK, ML_CHUNK), ML_CHUNK)

    def ctx_load(which, c):
        return ctx_refs[which][0, chunk_rows(c), :]

    def lat_load(which, c):
        return lat_refs[which][0, chunk_rows(c), :]

    ctx_unroll = ML_UNROLL if nc_ctx % ML_UNROLL == 0 else 1

    def ctx_body(i, carry):
        run_steps(i * ctx_unroll, ctx_unroll, nc_ctx, ctx_load, rc, False)
        return carry

    lax.fori_loop(0, nc_ctx // ctx_unroll, ctx_body, 0)

    def finalize(hh, r):
        ms = jnp.mean(hh * hh, axis=-1, keepdims=True)
        y = hh * lax.rsqrt(ms + NORM_EPS) * nw_ref[...]
        return y * _sigmoid(og_ref[0, r, :])

    def lat_body(i, second):
        first_step = i * ML_UNROLL
        hs = run_steps(first_step, ML_UNROLL, nc_lat, lat_load, rl, True)
        for j in range(ML_UNROLL):
            for d, c in ((0, first_step + j), (1, nc_lat - 1 - first_step - j)):
                hh = hs[2 * j + d]
                r = chunk_rows(c)
                if second:
                    y_ref[0, r, :] = finalize(hh + obuf[r, :], r).astype(y_ref.dtype)
                else:
                    obuf[r, :] = hh

    def first_body(i, carry):
        lat_body(i, False)
        return carry

    def second_body(i, carry):
        lat_body(i, True)
        return carry

    lax.fori_loop(0, half // ML_UNROLL, first_body, 0)
    lax.fori_loop(half // ML_UNROLL, nc_lat // ML_UNROLL, second_body, 0)


def _mlstm(p_lat, p_ctx, g_lat, g_ctx, scalars, ml_norm):
    b, t_lat, _ = p_lat.shape
    t_ctx = p_ctx.shape[1]
    nc_lat, nc_ctx = t_lat // ML_CHUNK, t_ctx // ML_CHUNK
    assert nc_lat % (2 * ML_UNROLL) == 0 and ML_CHUNK % (t_lat // GRID_W) == 0
    h = N_HEADS
    lat_spec = lambda off: pl.BlockSpec((1, t_lat, HEAD_DIM), lambda i, j: (i, 0, off + j))
    ctx_spec = lambda off: pl.BlockSpec((1, t_ctx, HEAD_DIM), lambda i, j: (i, 0, off + j))
    gate_spec = lambda nc: pl.BlockSpec((1, 1, 2, nc, GATE_ROWS, ML_CHUNK), lambda i, j: (i, j, 0, 0, 0, 0))
    return pl.pallas_call(
        _ml_kernel,
        grid=(b, h),
        in_specs=[pl.BlockSpec(memory_space=pltpu.SMEM),
                  lat_spec(0), lat_spec(h), lat_spec(2 * h), lat_spec(3 * h),
                  ctx_spec(0), ctx_spec(h), ctx_spec(2 * h),
                  gate_spec(nc_lat), gate_spec(nc_ctx),
                  pl.BlockSpec((1, HEAD_DIM), lambda i, j: (0, j))],
        out_specs=pl.BlockSpec((1, t_lat, HEAD_DIM), lambda i, j: (i, 0, j)),
        out_shape=jax.ShapeDtypeStruct((b, t_lat, h * HEAD_DIM), F32),
        scratch_shapes=[pltpu.VMEM((nc_lat, GATE_ROWS, ML_CHUNK), F32), pltpu.VMEM((nc_ctx, GATE_ROWS, ML_CHUNK), F32),
                        pltpu.VMEM((t_lat, HEAD_DIM), F32), pltpu.VMEM((2, HEAD_DIM, 2 * HEAD_DIM), F32),
                        pltpu.VMEM((2, 8, HEAD_DIM), F32)],
        compiler_params=pltpu.CompilerParams(
            dimension_semantics=("arbitrary", "arbitrary"), vmem_limit_bytes=VMEM_LIMIT),
        name="mlstm_scan",
    )(scalars, p_lat, p_lat, p_lat, p_lat, p_ctx, p_ctx, p_ctx, g_lat, g_ctx,
      ml_norm.reshape(1, h * HEAD_DIM))


def _ffn_kernel(x_ref, ydn_ref, yml_ref, mod_ref, n2_ref, fn_ref, wo_dn_ref, wo_ml_ref,
                wg_ref, wu_ref, wd_ref, o_ref, h_ref, act_ref, *, f_tile):
    yml = jnp.swapaxes(yml_ref[0], 0, 1).reshape(x_ref.shape[1], yml_ref.shape[3]).astype(BF16)
    mix = (jnp.dot(ydn_ref[0], wo_dn_ref[...], preferred_element_type=F32)
           + jnp.dot(yml, wo_ml_ref[...], preferred_element_type=F32))
    x1 = x_ref[0] + mod_ref[0, 2:3, :] * mix
    h_ref[...] = _rms_mod(x1, n2_ref[...], mod_ref[0, 4:5, :], mod_ref[0, 3:4, :]).astype(BF16)
    d_ff = wg_ref.shape[1]
    for j in range(d_ff // f_tile):
        sl = slice(j * f_tile, (j + 1) * f_tile)
        gate = jnp.dot(h_ref[...], wg_ref[:, sl], preferred_element_type=F32)
        up = jnp.dot(h_ref[...], wu_ref[:, sl], preferred_element_type=F32)
        act_ref[:, sl] = (_silu(gate) * up).astype(BF16)
    x2 = x1 + mod_ref[0, 5:6, :] * jnp.dot(act_ref[...], wd_ref[...], preferred_element_type=F32)
    ms = jnp.mean(x2 * x2, axis=-1, keepdims=True)
    o_ref[0] = x2 * lax.rsqrt(ms + NORM_EPS) * fn_ref[...]


def _out_ffn(x, y_dn, y_ml, mod, norm2, final_norm, wo_dn, wo_ml, w_gate, w_up, w_down, tm):
    b, t, d = x.shape
    d_mix = y_dn.shape[2]
    d_ff = w_gate.shape[1]
    assert tm % GRID_W == 0 and (tm // GRID_W) % 8 == 0
    const = lambda shape: pl.BlockSpec(shape, lambda i, j: (0,) * len(shape),
                                       pipeline_mode=pl.Buffered(1))
    kern = functools.partial(_ffn_kernel, f_tile=256)
    return pl.pallas_call(
        kern,
        grid=(b, t // tm),
        in_specs=[pl.BlockSpec((1, tm, d), lambda i, j: (i, j, 0)),
                  pl.BlockSpec((1, tm, d_mix), lambda i, j: (i, j, 0)),
                  pl.BlockSpec((1, GRID_W, tm // GRID_W, d_mix), lambda i, j: (i, 0, j, 0)),
                  pl.BlockSpec((1, N_MOD, d), lambda i, j: (i, 0, 0)),
                  const((1, d)), const((1, d)),
                  const((d_mix, d)), const((d_mix, d)),
                  const((d, d_ff)), const((d, d_ff)), const((d_ff, d))],
        out_specs=pl.BlockSpec((1, tm, d), lambda i, j: (i, j, 0)),
        out_shape=jax.ShapeDtypeStruct((b, t, d), F32),
        scratch_shapes=[pltpu.VMEM((tm, d), BF16), pltpu.VMEM((tm, d_ff), BF16)],
        compiler_params=pltpu.CompilerParams(
            dimension_semantics=("arbitrary", "arbitrary"), vmem_limit_bytes=VMEM_LIMIT),
        name="out_ffn",
    )(x, y_dn, y_ml, mod, norm2.reshape(1, d), final_norm.reshape(1, d),
      wo_dn, wo_ml, w_gate, w_up, w_down)


def _gate_weight_rows(w_in, d_group):
    h = N_HEADS
    cols = []
    for mixer in range(2):
        base = mixer * (4 * d_group + 4 * h) + 4 * d_group
        for head in range(h):
            for slab in range(2):
                cols += [base + slab * 2 * h + head, base + slab * 2 * h + h + head]
    return w_in[:, jnp.array(cols)].T


def _chunk_major_gates(gt, chunk, col_major):
    b, _, t = gt.shape
    if col_major:
        rows = t // GRID_W
        cols = chunk // rows
        g = gt.reshape(b, N_HEADS, 2, 2, rows, GRID_W // cols, cols)
        g = g.transpose(0, 1, 2, 5, 3, 6, 4).reshape(b, N_HEADS, 2, t // chunk, 2, chunk)
    else:
        g = gt.reshape(b, N_HEADS, 2, 2, t // chunk, chunk).transpose(0, 1, 2, 4, 3, 5)
    return jnp.tile(g, (1, 1, 1, 1, GATE_ROWS // 2, 1))


def kernel(x, c, ctx, c_ctx, w_mod, b_mod, norm1, w_in, dn_conv, dn_a_log, dn_dt_bias, dn_norm,
           ml_ig_bias, ml_fg_bias, ml_norm, w_out, norm2, w_ffn_in, w_ffn_out, final_norm):
    depth = w_mod.shape[0]
    assert depth == 1, "context outputs are only skipped for a single layer"
    b, t_lat, d = x.shape
    h = N_HEADS
    d_group = h * HEAD_DIM
    d_ff = w_ffn_out.shape[1]
    layer = 0

    pad_rows = -(b + 1) % 8
    cc = jnp.concatenate([c, c_ctx[None, :], jnp.zeros((pad_rows, d), F32)], axis=0)
    mod = _modulation(cc, w_mod[layer], b_mod[layer])
    mod_lat = mod[:, :b].transpose(1, 0, 2)
    mod_ctx = jnp.broadcast_to(mod[:, b][None], (b, N_MOD, d))

    w = w_in[layer]
    dn_cols = 4 * d_group + 4 * h
    w_dn = w[:, :4 * d_group].astype(BF16)
    w_ml = w[:, dn_cols:dn_cols + 4 * d_group].astype(BF16)
    w_gate_t = _gate_weight_rows(w, d_group).astype(BF16)
    pdn_lat, pml_lat, gt_lat = _in_projection(x, mod_lat, norm1[layer], w_dn, w_ml, w_gate_t,
                                              tm=512, col_major=True)
    pdn_ctx, pml_ctx, gt_ctx = _in_projection(ctx, mod_ctx, norm1[layer], w_dn, w_ml, w_gate_t,
                                              tm=ctx.shape[1], col_major=False)
    pml_lat = pml_lat.reshape(b, t_lat, 4 * d_group)

    dn_gate_rows = h * 2 * 2
    g_dn_lat = _chunk_major_gates(gt_lat[:, :dn_gate_rows], DN_CHUNK, False)
    g_dn_ctx = _chunk_major_gates(gt_ctx[:, :dn_gate_rows], DN_CHUNK, False)
    g_ml_lat = _chunk_major_gates(gt_lat[:, dn_gate_rows:], ML_CHUNK, True)
    g_ml_ctx = _chunk_major_gates(gt_ctx[:, dn_gate_rows:], ML_CHUNK, False)

    dn_scal = jnp.concatenate([dn_a_log[layer].T, dn_dt_bias[layer].T], axis=1)
    ml_scal = jnp.concatenate([ml_ig_bias[layer].T, ml_fg_bias[layer].T], axis=1)

    y_dn = _deltanet(pdn_lat, pdn_ctx, g_dn_lat, g_dn_ctx, dn_scal, dn_conv[layer], dn_norm[layer])
    y_ml = _mlstm(pml_lat, pml_ctx, g_ml_lat, g_ml_ctx, ml_scal, ml_norm[layer])
    y_ml = y_ml.reshape(b, GRID_W, t_lat // GRID_W, d_group)

    wo = w_out[layer].astype(BF16)
    wf = w_ffn_in[layer].astype(BF16)
    return _out_ffn(x, y_dn, y_ml, mod_lat, norm2[layer], final_norm,
                    wo[:d_group], wo[d_group:], wf[:, :d_ff], wf[:, d_ff:],
                    w_ffn_out[layer].astype(BF16), tm=512)
```

```python
import functools

import jax
import jax.numpy as jnp
from jax import lax
from jax.experimental import pallas as pl
from jax.experimental.pallas import tpu as pltpu

F32 = jnp.float32
BF16 = jnp.bfloat16

DN_CHUNK = 128
ML_CHUNK = 128
ML_UNROLL = 8
GRID_W = 64
HEAD_DIM = 128
N_HEADS = 4
CONV_K = 5
NORM_EPS = 1e-6
N_MOD = 6
GATE_ROWS = 8
NEG_BIG = -1e30
VMEM_LIMIT = 56 * 1024 * 1024

NT_DIMS = (((1,), (1,)), ((), ()))
TN_DIMS = (((0,), (0,)), ((), ()))


def _dot(a, b):
    return jnp.dot(a.astype(BF16), b.astype(BF16), preferred_element_type=F32)


def _dot_nt(a, b):
    return lax.dot_general(a.astype(BF16), b.astype(BF16), NT_DIMS, preferred_element_type=F32)


def _split3(x):
    hi = x.astype(BF16)
    r1 = x - hi.astype(F32)
    mid = r1.astype(BF16)
    lo = (r1 - mid.astype(F32)).astype(BF16)
    return hi, mid, lo


def _split2(x):
    hi = x.astype(BF16)
    return hi, (x - hi.astype(F32)).astype(BF16)


def _dot_exact_rhs(x, m_bf16):
    hi, mid, lo = _split3(x)
    f = lambda t: jnp.dot(t, m_bf16, preferred_element_type=F32)
    return f(hi) + f(mid) + f(lo)


def _softplus(x):
    return jnp.maximum(x, 0.0) + jnp.log(1.0 + jnp.exp(-jnp.abs(x)))


def _sigmoid(x):
    return 1.0 / (1.0 + jnp.exp(-x))


def _silu(x):
    return x * _sigmoid(x)


def _mod_kernel(c_ref, w_ref, b_ref, o_ref):
    sc = _silu(c_ref[...])
    o_ref[0] = jnp.dot(sc, w_ref[...], preferred_element_type=F32,
                       precision=lax.Precision.HIGHEST) + b_ref[0]


def _modulation(cc, w_mod, b_mod):
    rows, d = cc.shape
    return pl.pallas_call(
        _mod_kernel,
        grid=(N_MOD,),
        in_specs=[pl.BlockSpec((rows, d), lambda j: (0, 0)),
                  pl.BlockSpec((d, d), lambda j: (0, j)),
                  pl.BlockSpec((1, 1, d), lambda j: (j, 0, 0))],
        out_specs=pl.BlockSpec((1, rows, d), lambda j: (j, 0, 0)),
        out_shape=jax.ShapeDtypeStruct((N_MOD, rows, d), F32),
        compiler_params=pltpu.CompilerParams(vmem_limit_bytes=VMEM_LIMIT),
        name="modulation",
    )(cc, w_mod, b_mod.reshape(N_MOD, 1, d))


def _rms_mod(x, gain, scale, shift):
    ms = jnp.mean(x * x, axis=-1, keepdims=True)
    return (x * lax.rsqrt(ms + NORM_EPS) * gain) * (1.0 + scale) + shift


def _inproj_kernel(x_ref, mod_ref, n1_ref, wdn_ref, wml_ref, wg_ref, pdn_ref, pml_ref, gt_ref,
                   h_ref, hcm_ref, *, n_tile, col_major):
    h = _rms_mod(x_ref[0], n1_ref[...], mod_ref[0, 1:2, :], mod_ref[0, 0:1, :])
    h_ref[...] = h.astype(BF16)
    for j in range(wdn_ref.shape[1] // n_tile):
        cols = slice(j * n_tile, (j + 1) * n_tile)
        pdn_ref[0, :, cols] = jnp.dot(h_ref[...], wdn_ref[:, cols], preferred_element_type=F32)
    if col_major:
        tm, d = h.shape
        hcm_ref[...] = jnp.swapaxes(h.reshape(tm // GRID_W, GRID_W, d), 0, 1).reshape(tm, d).astype(BF16)
    lhs_ref = hcm_ref if col_major else h_ref
    for j in range(wml_ref.shape[1] // n_tile):
        cols = slice(j * n_tile, (j + 1) * n_tile)
        res = jnp.dot(lhs_ref[...], wml_ref[:, cols], preferred_element_type=F32)
        if col_major:
            pml_ref[0, :, :, cols] = res.reshape(pml_ref.shape[1], pml_ref.shape[2], n_tile)
        else:
            pml_ref[0, :, cols] = res
    gt_ref[0] = lax.dot_general(wg_ref[...], h_ref[...], NT_DIMS, preferred_element_type=F32)


def _in_projection(x, mod, norm1, w_dn, w_ml, w_gate_t, tm, col_major):
    b, t, d = x.shape
    n_dn, n_ml = w_dn.shape[1], w_ml.shape[1]
    n_gate = w_gate_t.shape[0]
    kern = functools.partial(_inproj_kernel, n_tile=512, col_major=col_major)
    if col_major:
        assert tm % GRID_W == 0 and (tm // GRID_W) % 8 == 0 and t % tm == 0
        rows = t // GRID_W
        ml_spec = pl.BlockSpec((1, GRID_W, tm // GRID_W, n_ml), lambda i, j: (i, 0, j, 0))
        ml_shape = jax.ShapeDtypeStruct((b, GRID_W, rows, n_ml), F32)
    else:
        ml_spec = pl.BlockSpec((1, tm, n_ml), lambda i, j: (i, j, 0))
        ml_shape = jax.ShapeDtypeStruct((b, t, n_ml), F32)
    const = lambda shape: pl.BlockSpec(shape, lambda i, j: (0,) * len(shape), pipeline_mode=pl.Buffered(1))
    return pl.pallas_call(
        kern,
        grid=(b, t // tm),
        in_specs=[pl.BlockSpec((1, tm, d), lambda i, j: (i, j, 0)),
                  pl.BlockSpec((1, N_MOD, d), lambda i, j: (i, 0, 0)),
                  const((1, d)), const((d, n_dn)), const((d, n_ml)), const((n_gate, d))],
        out_specs=[pl.BlockSpec((1, tm, n_dn), lambda i, j: (i, j, 0)),
                   ml_spec,
                   pl.BlockSpec((1, n_gate, tm), lambda i, j: (i, 0, j))],
        out_shape=[jax.ShapeDtypeStruct((b, t, n_dn), F32),
                   ml_shape,
                   jax.ShapeDtypeStruct((b, n_gate, t), F32)],
        scratch_shapes=[pltpu.VMEM((tm, d), BF16), pltpu.VMEM((tm, d), BF16)],
        compiler_params=pltpu.CompilerParams(
            dimension_semantics=("arbitrary", "arbitrary"), vmem_limit_bytes=VMEM_LIMIT),
        name="in_projection",
    )(x, mod, norm1.reshape(1, d), w_dn, w_ml, w_gate_t)


def _iota2(n):
    return (lax.broadcasted_iota(jnp.int32, (n, n), 0), lax.broadcasted_iota(jnp.int32, (n, n), 1))


def _to_columns(x):
    ii, jj = _iota2(x.shape[1])
    eye = (ii == jj).astype(BF16)
    hi, mid, lo = _split3(x)
    f = lambda t: lax.dot_general(eye, t, NT_DIMS, preferred_element_type=F32)
    return f(hi) + f(mid) + f(lo)


def _masks(fwd, n):
    ii, jj = _iota2(n)
    if fwd:
        return ii >= jj, ii > jj
    return ii <= jj, ii < jj


def _row_parity(shape):
    return lax.broadcasted_iota(jnp.int32, shape, len(shape) - 2) % 2


def _cumulate_gate_rows(logdecay):
    n_chunks, _, n = logdecay.shape
    ii, jj = _iota2(n)
    prefix = (ii <= jj).astype(BF16)
    suffix = (ii >= jj).astype(BF16)
    ones = jnp.ones((n, n), BF16)
    flat = logdecay.reshape(n_chunks * GATE_ROWS, n)
    par = _row_parity(flat.shape)
    cum = jnp.where(par == 0, _dot_exact_rhs(flat, prefix), _dot_exact_rhs(flat, suffix))
    tot = _dot_exact_rhs(flat, ones)
    row = lax.broadcasted_iota(jnp.int32, flat.shape, 0) % GATE_ROWS
    return cum, tot, row


def _conv_block(u_ref, w, t0, rows, total):
    main = u_ref[0, pl.ds(t0, rows), :]
    lo = jnp.maximum(t0 - 8, 0)
    hi = jnp.minimum(t0 + rows, total - 8)
    prev = jnp.where(t0 > 0, u_ref[0, pl.ds(pl.multiple_of(lo, 8), 8), :], 0.0)
    nxt = jnp.where(t0 + rows < total, u_ref[0, pl.ds(pl.multiple_of(hi, 8), 8), :], 0.0)
    ext = jnp.concatenate([prev, main, nxt], axis=0)
    acc = None
    for j in range(CONV_K):
        off = 8 + j - CONV_K // 2
        term = ext[off:off + rows, :] * w[j:j + 1, :]
        acc = term if acc is None else acc + term
    return _silu(acc)


def _conv_block_inner(u_ref, w, t0, rows):
    acc = None
    for j in range(CONV_K):
        term = u_ref[0, pl.ds(t0 + (j - CONV_K // 2), rows), :] * w[j:j + 1, :]
        acc = term if acc is None else acc + term
    return _silu(acc)


def _l2norm(x):
    return x * lax.rsqrt(jnp.sum(x * x, axis=-1, keepdims=True) + NORM_EPS)


def _unit_tri_inverses(a_list, between_levels=()):
    n = a_list[0].shape[0]
    ii, jj = _iota2(n)
    eye = (ii == jj).astype(F32)
    pair = (ii >> 1) == (jj >> 1)
    ts = [eye - jnp.where(pair, a, 0.0) for a in a_list]
    for level in range(1, n.bit_length() - 1):
        same_big = (ii >> (level + 1)) == (jj >> (level + 1))
        same_small = (ii >> level) == (jj >> level)
        couple = same_big & jnp.logical_not(same_small)
        es = [jnp.where(couple, a, 0.0).astype(BF16) for a in a_list]
        tbs = [t.astype(BF16) for t in ts]
        tes = [jnp.dot(tb, e, preferred_element_type=F32) for tb, e in zip(tbs, es)]
        ts = [t - jnp.dot(te.astype(BF16), tb, preferred_element_type=F32)
              for t, te, tb in zip(ts, tes, tbs)]
        if level <= len(between_levels):
            between_levels[level - 1]()
    return ts


def _dn_intra_chunks(qkvx, with_out, between_levels=()):
    n = qkvx[0][0].shape[0]
    cols = [_to_columns(x) for _, _, _, x in qkvx]
    kks = [_dot_nt(k, k) for _, k, _, _ in qkvx]
    qks = [_dot_nt(q, k) if with_out else None for q, k, _, _ in qkvx]
    parts = []
    for (q, k, v, x), col, kk in zip(qkvx, cols, kks):
        for d in range(2):
            beta_c = col[:, d:d + 1]
            g_c = col[:, 2 + d:3 + d]
            tot_c = col[:, 4 + d:5 + d]
            g_r = x[2 + d:3 + d, :]
            incl, strict = _masks(d == 0, n)
            decay = jnp.exp(jnp.where(incl, g_c - g_r, NEG_BIG))
            e_g = jnp.exp(g_c)
            a = jnp.where(strict, kk * beta_c * decay, 0.0)
            rhs = jnp.concatenate([k * (beta_c * e_g), v * beta_c], axis=1).astype(BF16)
            parts.append((a, rhs, decay, e_g, tot_c - g_c))
    ts = _unit_tri_inverses([p[0] for p in parts], between_levels)
    wus = [jnp.dot(t.astype(BF16), p[1], preferred_element_type=F32).astype(BF16) for t, p in zip(ts, parts)]
    k_tail_ts = [(qkvx[i // 2][1] * jnp.exp(p[4])).T.astype(BF16) for i, p in enumerate(parts)]
    state_terms = [jnp.dot(kt, wu, preferred_element_type=F32) for kt, wu in zip(k_tail_ts, wus)]
    if with_out:
        scores = [(qks[i // 2] * p[2]).astype(BF16) for i, p in enumerate(parts)]
        out_terms = [jnp.dot(sc, wu, preferred_element_type=F32) for sc, wu in zip(scores, wus)]
    res = []
    for ci, (q, k, v, x) in enumerate(qkvx):
        per_dir = []
        for d in range(2):
            i = 2 * ci + d
            st = state_terms[i]
            if with_out:
                ot = out_terms[i]
                per_dir.append((-st[:, :HEAD_DIM], st[:, HEAD_DIM:],
                                q * parts[i][3] - ot[:, :HEAD_DIM], ot[:, HEAD_DIM:]))
            else:
                per_dir.append((-st[:, :HEAD_DIM], st[:, HEAD_DIM:], None, None))
        res.append(per_dir)
    return res


def _dn_kernel(sc_ref, ql_ref, kl_ref, vl_ref, gate_ref, qc_ref, kc_ref, vc_ref,
               gl_ref, gc_ref, wq_ref, wk_ref, wv_ref, nw_ref, y_ref,
               qs, ks, vs, rows_s, lhs_s, add_s, obuf, s_ref):
    head = pl.program_id(1)
    n = DN_CHUNK
    t_lat = ql_ref.shape[1]
    t_ctx = qc_ref.shape[1]
    nc_lat = t_lat // n
    nc_ctx = t_ctx // n
    half = nc_lat // 2
    conv_rows = 256
    ctx_group = 2 if nc_ctx % 2 == 0 else 1
    lat_group = 8 if nc_lat % 16 == 0 else 2

    a_log_f, a_log_b = sc_ref[head, 0], sc_ref[head, 1]
    dtb_f, dtb_b = sc_ref[head, 2], sc_ref[head, 3]

    def chunk_rows(c):
        return pl.ds(pl.multiple_of(c * n, n), n)

    def prep(src_refs, total):
        def block(t0, conv):
            q = _l2norm(conv(src_refs[0], wq_ref[...], t0))
            qs[pl.ds(t0, conv_rows), :] = q * (HEAD_DIM ** -0.5)
            ks[pl.ds(t0, conv_rows), :] = _l2norm(conv(src_refs[1], wk_ref[...], t0))
            vs[pl.ds(t0, conv_rows), :] = conv(src_refs[2], wv_ref[...], t0)

        edge = lambda u_ref, w, t0: _conv_block(u_ref, w, t0, conv_rows, total)
        inner = lambda u_ref, w, t0: _conv_block_inner(u_ref, w, t0, conv_rows)
        n_blocks = total // conv_rows
        block(0, edge)
        if n_blocks > 1:
            block((n_blocks - 1) * conv_rows, edge)

        def body(i, carry):
            block(pl.multiple_of(i * conv_rows, conv_rows), inner)
            return carry
        lax.fori_loop(1, n_blocks - 1, body, 0)

    def gate_rows(g_ref, n_chunks):
        beta_raw = g_ref[0, 0, 0]
        alpha_raw = g_ref[0, 0, 1]
        par = _row_parity(alpha_raw.shape)
        a_vec = jnp.exp(jnp.where(par == 0, a_log_f, a_log_b))
        dtb = jnp.where(par == 0, dtb_f, dtb_b)
        cum, tot, row = _cumulate_gate_rows(-a_vec * _softplus(alpha_raw + dtb))
        beta = _sigmoid(beta_raw).reshape(cum.shape)
        packed = jnp.where(row < 2, beta, jnp.where(row < 4, cum, tot))
        rows_s[0:n_chunks] = packed.reshape(n_chunks, GATE_ROWS, n)

    def intra_group(cs, with_out, between_levels=()):
        res = _dn_intra_chunks([(qs[chunk_rows(c), :], ks[chunk_rows(c), :], vs[chunk_rows(c), :], rows_s[c])
                                for c in cs], with_out, between_levels)
        for c, per_dir in zip(cs, res):
            for d in range(2):
                s_mul, s_add, o_mul, o_add = per_dir[d]
                lhs_s[d, c, 0:HEAD_DIM, :] = s_mul.astype(BF16)
                add_s[d, c, 0:HEAD_DIM, :] = s_add
                if with_out:
                    lhs_s[d, c, HEAD_DIM:, :] = o_mul.astype(BF16)
                    add_s[d, c, HEAD_DIM:, :] = o_add

    def state_steps(cf, cb, with_out):
        dc = ((0, cf), (1, cb))
        rows = slice(None) if with_out else slice(0, HEAD_DIM)
        ss = [s_ref[d] for d, _ in dc]
        rs = [jnp.dot(lhs_s[d, c, rows, :], s.astype(BF16), preferred_element_type=F32) + add_s[d, c, rows, :]
              for (d, c), s in zip(dc, ss)]
        for (d, c), s, r in zip(dc, ss, rs):
            s_ref[d] = s * jnp.exp(rows_s[c][4 + d:5 + d, :]) + r[:HEAD_DIM]
        return [r[HEAD_DIM:] if with_out else None for r in rs]

    def finalize(o, r):
        ms = jnp.mean(o * o, axis=-1, keepdims=True)
        return o * lax.rsqrt(ms + NORM_EPS) * nw_ref[...] * _silu(gate_ref[0, r, :])

    s_ref[...] = jnp.zeros_like(s_ref)
    prep((qc_ref, kc_ref, vc_ref), t_ctx)
    gate_rows(gc_ref, nc_ctx)
    for g in range(nc_ctx // ctx_group):
        intra_group([g * ctx_group + j for j in range(ctx_group)], False)

    def ctx_body(i, carry):
        state_steps(i, nc_ctx - 1 - i, False)
        return carry
    lax.fori_loop(0, nc_ctx, ctx_body, 0)

    prep((ql_ref, kl_ref, vl_ref), t_lat)
    gate_rows(gl_ref, nc_lat)
    def first_visit(i):
        cb = nc_lat - 1 - i
        o_f, o_b = state_steps(i, cb, True)
        obuf[chunk_rows(i), :] = o_f
        obuf[chunk_rows(cb), :] = o_b

    side = lat_group // 2
    n_groups = half // side

    def group_chunks(g):
        return [g * side + j for j in range(side)] + [nc_lat - 1 - g * side - j for j in range(side)]

    def group_steps(g):
        return [functools.partial(first_visit, g * side + j) for j in range(side)]

    intra_group(group_chunks(0), True)

    def group_body(g, carry):
        intra_group(group_chunks(g), True, group_steps(g - 1))
        return carry
    lax.fori_loop(1, n_groups, group_body, 0)
    for step in group_steps(n_groups - 1):
        step()

    def second_visit(i):
        cb = nc_lat - 1 - i
        o_f, o_b = state_steps(i, cb, True)
        return o_f + obuf[chunk_rows(i), :], o_b + obuf[chunk_rows(cb), :]

    def finish(i, sums):
        for c, o in zip((i, nc_lat - 1 - i), sums):
            r = chunk_rows(c)
            y_ref[0, r, :] = finalize(o, r).astype(y_ref.dtype)

    def second_body(i, sums):
        new_sums = second_visit(i)
        finish(i - 1, sums)
        return new_sums

    finish(nc_lat - 1, lax.fori_loop(half + 1, nc_lat, second_body, second_visit(half)))


def _deltanet(p_lat, p_ctx, g_lat, g_ctx, scalars, dn_conv, dn_norm):
    b, t_lat, _ = p_lat.shape
    t_ctx = p_ctx.shape[1]
    n = DN_CHUNK
    nc_lat, nc_ctx = t_lat // n, t_ctx // n
    assert n == HEAD_DIM and nc_lat % 16 == 0 and t_lat % 256 == 0 and t_ctx % 256 == 0 and t_ctx <= t_lat
    h = N_HEADS
    col = lambda off: (lambda i, j: (i, 0, off + j))
    lat_spec = lambda off: pl.BlockSpec((1, t_lat, HEAD_DIM), col(off))
    ctx_spec = lambda off: pl.BlockSpec((1, t_ctx, HEAD_DIM), col(off))
    conv_spec = lambda off: pl.BlockSpec((CONV_K, HEAD_DIM), lambda i, j: (0, off + j))
    gate_spec = lambda nc: pl.BlockSpec((1, 1, 2, nc, GATE_ROWS, n), lambda i, j: (i, j, 0, 0, 0, 0))
    seq = lambda dt: pltpu.VMEM((t_lat, HEAD_DIM), dt)
    step_terms = lambda dt: pltpu.VMEM((2, nc_lat, HEAD_DIM + n, HEAD_DIM), dt)
    return pl.pallas_call(
        _dn_kernel,
        grid=(b, h),
        in_specs=[pl.BlockSpec(memory_space=pltpu.SMEM),
                  lat_spec(0), lat_spec(h), lat_spec(2 * h), lat_spec(3 * h),
                  ctx_spec(0), ctx_spec(h), ctx_spec(2 * h),
                  gate_spec(nc_lat), gate_spec(nc_ctx),
                  conv_spec(0), conv_spec(h), conv_spec(2 * h),
                  pl.BlockSpec((1, HEAD_DIM), lambda i, j: (0, 0))],
        out_specs=pl.BlockSpec((1, t_lat, HEAD_DIM), lambda i, j: (i, 0, j)),
        out_shape=jax.ShapeDtypeStruct((b, t_lat, h * HEAD_DIM), BF16),
        scratch_shapes=[seq(F32), seq(F32), seq(F32),
                        pltpu.VMEM((nc_lat, GATE_ROWS, n), F32),
                        step_terms(BF16), step_terms(F32),
                        seq(F32),
                        pltpu.VMEM((2, HEAD_DIM, HEAD_DIM), F32)],
        compiler_params=pltpu.CompilerParams(
            dimension_semantics=("arbitrary", "arbitrary"), vmem_limit_bytes=VMEM_LIMIT),
        name="deltanet_scan",
    )(scalars, p_lat, p_lat, p_lat, p_lat, p_ctx, p_ctx, p_ctx, g_lat, g_ctx,
      dn_conv, dn_conv, dn_conv, dn_norm.reshape(1, HEAD_DIM))


def _ml_group(problems, states, with_out):
    n = problems[0][1].shape[0]
    ones_blk = jnp.ones((n, HEAD_DIM), BF16)

    def lane_spread(rows):
        ones2 = jnp.ones((2, HEAD_DIM), BF16)
        return [lax.dot_general(jnp.concatenate(_split2(row), axis=0), ones2, TN_DIMS,
                                preferred_element_type=F32) for row in rows]

    a_rows = [x[d:d + 1, :] for d, _, _, _, x in problems]
    a_spreads = lane_spread(a_rows)
    b_spreads = lane_spread([x[2 + d:3 + d, :] for d, _, _, _, x in problems]) if with_out else None
    ms = [m for _, m in states]
    chain = []
    for (d, _, _, _, x), a_r in zip(problems, a_rows):
        mx = jnp.maximum(ms[d], jnp.max(a_r, axis=1, keepdims=True))
        chain.append((ms[d], mx))
        ms[d] = x[4 + d:5 + d, 0:1] + mx
    pre = []
    for (d, q, k, v, x), a_s, (_, mx) in zip(problems, a_spreads, chain):
        v_ext = jnp.concatenate([v.astype(BF16), ones_blk], axis=1)
        pre.append((v_ext, (k * jnp.exp(a_s - mx)).astype(BF16)))
    ups = [lax.dot_general(kw, v_ext, TN_DIMS, preferred_element_type=F32) for v_ext, kw in pre]
    if with_out:
        cm_rows = [jnp.max(jnp.where(_masks(d != 0, n)[0], a_s[:, :n], NEG_BIG), axis=0, keepdims=True)
                   for (d, _, _, _, _), a_s in zip(problems, a_spreads)]
        cm_spreads = lane_spread(cm_rows)
        qks = [_dot_nt(q, k) for _, q, k, _, _ in problems]
        scores = []
        for (d, _, _, _, _), a_r, cm_s, qk in zip(problems, a_rows, cm_spreads, qks):
            incl, _ = _masks(d == 0, n)
            expo = jnp.where(incl, jnp.broadcast_to(a_r, (n, n)) - cm_s[:, :n], NEG_BIG)
            scores.append((qk * jnp.exp(expo)).astype(BF16))
        intra = [jnp.dot(s, v_ext, preferred_element_type=F32) for s, (v_ext, _) in zip(scores, pre)]
    cs = [c_ext for c_ext, _ in states]
    starts = []
    for (d, _, _, _, _), (m, mx), up in zip(problems, chain, ups):
        starts.append((cs[d], m))
        cs[d] = jnp.exp(m - mx) * cs[d] + up
    states = list(zip(cs, ms))
    if not with_out:
        return states, [None] * len(problems)
    inter = [_dot(q, c_ext) for (_, q, _, _, _), (c_ext, _) in zip(problems, starts)]
    hs = []
    for (_, m), cm_s, b_s, qc, sv in zip(starts, cm_spreads, b_spreads, inter, intra):
        mm = jnp.maximum(m, cm_s)
        w_inter = jnp.exp(m - mm)
        w_intra = jnp.exp(cm_s - mm)
        num = w_inter * qc[:, :HEAD_DIM] + w_intra * sv[:, :HEAD_DIM]
        den = w_inter * qc[:, HEAD_DIM:] + w_intra * sv[:, HEAD_DIM:]
        hs.append(num / jnp.maximum(jnp.abs(den), jnp.exp(-(b_s + mm))))
    return states, hs


def _ml_kernel(sc_ref, ql_ref, kl_ref, vl_ref, og_ref, qc_ref, kc_ref, vc_ref,
               gl_ref, gc_ref, nw_ref, y_ref, rl, rc, obuf, c_ref, m_ref):
    head = pl.program_id(1)
    t_lat = ql_ref.shape[1]
    t_ctx = qc_ref.shape[1]
    nc_lat = t_lat // ML_CHUNK
    nc_ctx = t_ctx // ML_CHUNK
    half = nc_lat // 2
    k_scale = HEAD_DIM ** -0.5

    igb_f, igb_b = sc_ref[head, 0], sc_ref[head, 1]
    fgb_f, fgb_b = sc_ref[head, 2], sc_ref[head, 3]

    def gate_rows(g_ref, dst, n_chunks):
        ig_raw = g_ref[0, 0, 0]
        fg_raw = g_ref[0, 0, 1]
        par = _row_parity(fg_raw.shape)
        lf = -_softplus(-(fg_raw + jnp.where(par == 0, fgb_f, fgb_b)))
        cum, tot, row = _cumulate_gate_rows(lf)
        ic = (ig_raw + jnp.where(par == 0, igb_f, igb_b)).reshape(n_chunks * GATE_ROWS, ML_CHUNK)
        packed = jnp.where(row < 2, ic - cum, jnp.where(row < 4, cum, tot))
        dst[...] = packed.reshape(n_chunks, GATE_ROWS, ML_CHUNK)

    gate_rows(gl_ref, rl, nc_lat)
    gate_rows(gc_ref, rc, nc_ctx)

    c_ref[...] = jnp.zeros_like(c_ref)
    m_ref[...] = jnp.zeros_like(m_ref)

    def run_steps(first_step, n_steps, total, load, gates, with_out):
        problems = []
        for j in range(n_steps):
            for d, c in ((0, first_step + j), (1, total - 1 - first_step - j)):
                problems.append((d, load(0, c), load(1, c) * k_scale, load(2, c), gates[c]))
        states = [(c_ref[d], m_ref[d, 0:1, 0:1]) for d in range(2)]
        states, hs = _ml_group(problems, states, with_out)
        for d, (c_ext, m) in enumerate(states):
            c_ref[d] = c_ext
            m_ref[d] = jnp.broadcast_to(m, m_ref.shape[1:])
        return hs

    ctx_refs = (qc_ref, kc_ref, vc_ref)
    lat_refs = (ql_ref, kl_ref, vl_ref)

    def chunk_rows(c):
        return pl.ds(pl.multiple_of(c * ML_CHUNK, ML_CHUNK), ML_CHUNK)

    def ctx_load(which, c):
        return ctx_refs[which][0, chunk_rows(c), :]

    def lat_load(which, c):
        return lat_refs[which][0, chunk_rows(c), :]

    ctx_unroll = ML_UNROLL if nc_ctx % ML_UNROLL == 0 else 1

    def ctx_body(i, carry):
        run_steps(i * ctx_unroll, ctx_unroll, nc_ctx, ctx_load, rc, False)
        return carry

    lax.fori_loop(0, nc_ctx // ctx_unroll, ctx_body, 0)

    def finalize(hh, r):
        ms = jnp.mean(hh * hh, axis=-1, keepdims=True)
        y = hh * lax.rsqrt(ms + NORM_EPS) * nw_ref[...]
        return y * _sigmoid(og_ref[0, r, :])

    def lat_body(i, second):
        first_step = i * ML_UNROLL
        hs = run_steps(first_step, ML_UNROLL, nc_lat, lat_load, rl, True)
        for j in range(ML_UNROLL):
            for d, c in ((0, first_step + j), (1, nc_lat - 1 - first_step - j)):
                hh = hs[2 * j + d]
                r = chunk_rows(c)
                if second:
                    y_ref[0, r, :] = finalize(hh + obuf[r, :], r).astype(y_ref.dtype)
                else:
                    obuf[r, :] = hh

    def first_body(i, carry):
        lat_body(i, False)
        return carry

    def second_body(i, carry):
        lat_body(i, True)
        return carry

    lax.fori_loop(0, half // ML_UNROLL, first_body, 0)
    lax.fori_loop(half // ML_UNROLL, nc_lat // ML_UNROLL, second_body, 0)


def _mlstm(p_lat, p_ctx, g_lat, g_ctx, scalars, ml_norm):
    b, t_lat, _ = p_lat.shape
    t_ctx = p_ctx.shape[1]
    nc_lat, nc_ctx = t_lat // ML_CHUNK, t_ctx // ML_CHUNK
    assert nc_lat % (2 * ML_UNROLL) == 0 and ML_CHUNK % (t_lat // GRID_W) == 0
    h = N_HEADS
    lat_spec = lambda off: pl.BlockSpec((1, t_lat, HEAD_DIM), lambda i, j: (i, 0, off + j))
    ctx_spec = lambda off: pl.BlockSpec((1, t_ctx, HEAD_DIM), lambda i, j: (i, 0, off + j))
    gate_spec = lambda nc: pl.BlockSpec((1, 1, 2, nc, GATE_ROWS, ML_CHUNK), lambda i, j: (i, j, 0, 0, 0, 0))
    return pl.pallas_call(
        _ml_kernel,
        grid=(b, h),
        in_specs=[pl.BlockSpec(memory_space=pltpu.SMEM),
                  lat_spec(0), lat_spec(h), lat_spec(2 * h), lat_spec(3 * h),
                  ctx_spec(0), ctx_spec(h), ctx_spec(2 * h),
                  gate_spec(nc_lat), gate_spec(nc_ctx),
                  pl.BlockSpec((1, HEAD_DIM), lambda i, j: (0, j))],
        out_specs=pl.BlockSpec((1, t_lat, HEAD_DIM), lambda i, j: (i, 0, j)),
        out_shape=jax.ShapeDtypeStruct((b, t_lat, h * HEAD_DIM), F32),
        scratch_shapes=[pltpu.VMEM((nc_lat, GATE_ROWS, ML_CHUNK), F32), pltpu.VMEM((nc_ctx, GATE_ROWS, ML_CHUNK), F32),
                        pltpu.VMEM((t_lat, HEAD_DIM), F32), pltpu.VMEM((2, HEAD_DIM, 2 * HEAD_DIM), F32),
                        pltpu.VMEM((2, 8, HEAD_DIM), F32)],
        compiler_params=pltpu.CompilerParams(
            dimension_semantics=("arbitrary", "arbitrary"), vmem_limit_bytes=VMEM_LIMIT),
        name="mlstm_scan",
    )(scalars, p_lat, p_lat, p_lat, p_lat, p_ctx, p_ctx, p_ctx, g_lat, g_ctx,
      ml_norm.reshape(1, h * HEAD_DIM))


def _ffn_kernel(x_ref, ydn_ref, yml_ref, mod_ref, n2_ref, fn_ref, wo_dn_ref, wo_ml_ref,
                wg_ref, wu_ref, wd_ref, o_ref, h_ref, act_ref, *, f_tile):
    yml = jnp.swapaxes(yml_ref[0], 0, 1).reshape(x_ref.shape[1], yml_ref.shape[3]).astype(BF16)
    mix = (jnp.dot(ydn_ref[0], wo_dn_ref[...], preferred_element_type=F32)
           + jnp.dot(yml, wo_ml_ref[...], preferred_element_type=F32))
    x1 = x_ref[0] + mod_ref[0, 2:3, :] * mix
    h_ref[...] = _rms_mod(x1, n2_ref[...], mod_ref[0, 4:5, :], mod_ref[0, 3:4, :]).astype(BF16)
    d_ff = wg_ref.shape[1]
    for j in range(d_ff // f_tile):
        sl = slice(j * f_tile, (j + 1) * f_tile)
        gate = jnp.dot(h_ref[...], wg_ref[:, sl], preferred_element_type=F32)
        up = jnp.dot(h_ref[...], wu_ref[:, sl], preferred_element_type=F32)
        act_ref[:, sl] = (_silu(gate) * up).astype(BF16)
    x2 = x1 + mod_ref[0, 5:6, :] * jnp.dot(act_ref[...], wd_ref[...], preferred_element_type=F32)
    ms = jnp.mean(x2 * x2, axis=-1, keepdims=True)
    o_ref[0] = x2 * lax.rsqrt(ms + NORM_EPS) * fn_ref[...]


def _out_ffn(x, y_dn, y_ml, mod, norm2, final_norm, wo_dn, wo_ml, w_gate, w_up, w_down, tm):
    b, t, d = x.shape
    d_mix = y_dn.shape[2]
    d_ff = w_gate.shape[1]
    assert tm % GRID_W == 0 and (tm // GRID_W) % 8 == 0
    const = lambda shape: pl.BlockSpec(shape, lambda i, j: (0,) * len(shape),
                                       pipeline_mode=pl.Buffered(1))
    kern = functools.partial(_ffn_kernel, f_tile=256)
    return pl.pallas_call(
        kern,
        grid=(b, t // tm),
        in_specs=[pl.BlockSpec((1, tm, d), lambda i, j: (i, j, 0)),
                  pl.BlockSpec((1, tm, d_mix), lambda i, j: (i, j, 0)),
                  pl.BlockSpec((1, GRID_W, tm // GRID_W, d_mix), lambda i, j: (i, 0, j, 0)),
                  pl.BlockSpec((1, N_MOD, d), lambda i, j: (i, 0, 0)),
                  const((1, d)), const((1, d)),
                  const((d_mix, d)), const((d_mix, d)),
                  const((d, d_ff)), const((d, d_ff)), const((d_ff, d))],
        out_specs=pl.BlockSpec((1, tm, d), lambda i, j: (i, j, 0)),
        out_shape=jax.ShapeDtypeStruct((b, t, d), F32),
        scratch_shapes=[pltpu.VMEM((tm, d), BF16), pltpu.VMEM((tm, d_ff), BF16)],
        compiler_params=pltpu.CompilerParams(
            dimension_semantics=("arbitrary", "arbitrary"), vmem_limit_bytes=VMEM_LIMIT),
        name="out_ffn",
    )(x, y_dn, y_ml, mod, norm2.reshape(1, d), final_norm.reshape(1, d),
      wo_dn, wo_ml, w_gate, w_up, w_down)


def _gate_weight_rows(w_in, d_group):
    h = N_HEADS
    cols = []
    for mixer in range(2):
        base = mixer * (4 * d_group + 4 * h) + 4 * d_group
        for head in range(h):
            for slab in range(2):
                cols += [base + slab * 2 * h + head, base + slab * 2 * h + h + head]
    return w_in[:, jnp.array(cols)].T


def _chunk_major_gates(gt, chunk, col_major):
    b, _, t = gt.shape
    if col_major:
        rows = t // GRID_W
        cols = chunk // rows
        g = gt.reshape(b, N_HEADS, 2, 2, rows, GRID_W // cols, cols)
        g = g.transpose(0, 1, 2, 5, 3, 6, 4).reshape(b, N_HEADS, 2, t // chunk, 2, chunk)
    else:
        g = gt.reshape(b, N_HEADS, 2, 2, t // chunk, chunk).transpose(0, 1, 2, 4, 3, 5)
    return jnp.tile(g, (1, 1, 1, 1, GATE_ROWS // 2, 1))


def kernel(x, c, ctx, c_ctx, w_mod, b_mod, norm1, w_in, dn_conv, dn_a_log, dn_dt_bias, dn_norm,
           ml_ig_bias, ml_fg_bias, ml_norm, w_out, norm2, w_ffn_in, w_ffn_out, final_norm):
    depth = w_mod.shape[0]
    assert depth == 1, "context outputs are only skipped for a single layer"
    b, t_lat, d = x.shape
    h = N_HEADS
    d_group = h * HEAD_DIM
    d_ff = w_ffn_out.shape[1]
    layer = 0

    pad_rows = -(b + 1) % 8
    cc = jnp.concatenate([c, c_ctx[None, :], jnp.zeros((pad_rows, d), F32)], axis=0)
    mod = _modulation(cc, w_mod[layer], b_mod[layer])
    mod_lat = mod[:, :b].transpose(1, 0, 2)
    mod_ctx = jnp.broadcast_to(mod[:, b][None], (b, N_MOD, d))

    w = w_in[layer]
    dn_cols = 4 * d_group + 4 * h
    w_dn = w[:, :4 * d_group].astype(BF16)
    w_ml = w[:, dn_cols:dn_cols + 4 * d_group].astype(BF16)
    w_gate_t = _gate_weight_rows(w, d_group).astype(BF16)
    pdn_lat, pml_lat, gt_lat = _in_projection(x, mod_lat, norm1[layer], w_dn, w_ml, w_gate_t,
                                              tm=512, col_major=True)
    pdn_ctx, pml_ctx, gt_ctx = _in_projection(ctx, mod_ctx, norm1[layer], w_dn, w_ml, w_gate_t,
                                              tm=ctx.shape[1], col_major=False)
    pml_lat = pml_lat.reshape(b, t_lat, 4 * d_group)

    dn_gate_rows = h * 2 * 2
    g_dn_lat = _chunk_major_gates(gt_lat[:, :dn_gate_rows], DN_CHUNK, False)
    g_dn_ctx = _chunk_major_gates(gt_ctx[:, :dn_gate_rows], DN_CHUNK, False)
    g_ml_lat = _chunk_major_gates(gt_lat[:, dn_gate_rows:], ML_CHUNK, True)
    g_ml_ctx = _chunk_major_gates(gt_ctx[:, dn_gate_rows:], ML_CHUNK, False)

    dn_scal = jnp.concatenate([dn_a_log[layer].T, dn_dt_bias[layer].T], axis=1)
    ml_scal = jnp.concatenate([ml_ig_bias[layer].T, ml_fg_bias[layer].T], axis=1)

    y_dn = _deltanet(pdn_lat, pdn_ctx, g_dn_lat, g_dn_ctx, dn_scal, dn_conv[layer], dn_norm[layer])
    y_ml = _mlstm(pml_lat, pml_ctx, g_ml_lat, g_ml_ctx, ml_scal, ml_norm[layer])
    y_ml = y_ml.reshape(b, GRID_W, t_lat // GRID_W, d_group)

    wo = w_out[layer].astype(BF16)
    wf = w_ffn_in[layer].astype(BF16)
    return _out_ffn(x, y_dn, y_ml, mod_lat, norm2[layer], final_norm,
                    wo[:d_group], wo[d_group:], wf[:, :d_ff], wf[:, d_ff:],
                    w_ffn_out[layer].astype(BF16), tm=512)
```

```python
import functools

import jax
import jax.numpy as jnp
from jax import lax
from jax.experimental import pallas as pl
from jax.experimental.pallas import tpu as pltpu

F32 = jnp.float32
BF16 = jnp.bfloat16

DN_CHUNK = 128
ML_CHUNK = 128
ML_UNROLL = 8
GRID_W = 64
HEAD_DIM = 128
N_HEADS = 4
CONV_K = 5
NORM_EPS = 1e-6
N_MOD = 6
GATE_ROWS = 8
NEG_BIG = -1e30
VMEM_LIMIT = 56 * 1024 * 1024

NT_DIMS = (((1,), (1,)), ((), ()))
TN_DIMS = (((0,), (0,)), ((), ()))


def _dot(a, b):
    return jnp.dot(a.astype(BF16), b.astype(BF16), preferred_element_type=F32)


def _dot_nt(a, b):
    return lax.dot_general(a.astype(BF16), b.astype(BF16), NT_DIMS, preferred_element_type=F32)


def _split3(x):
    hi = x.astype(BF16)
    r1 = x - hi.astype(F32)
    mid = r1.astype(BF16)
    lo = (r1 - mid.astype(F32)).astype(BF16)
    return hi, mid, lo


def _split2(x):
    hi = x.astype(BF16)
    return hi, (x - hi.astype(F32)).astype(BF16)


def _dot_exact_rhs(x, m_bf16):
    hi, mid, lo = _split3(x)
    f = lambda t: jnp.dot(t, m_bf16, preferred_element_type=F32)
    return f(hi) + f(mid) + f(lo)


def _softplus(x):
    return jnp.maximum(x, 0.0) + jnp.log(1.0 + jnp.exp(-jnp.abs(x)))


def _sigmoid(x):
    return 1.0 / (1.0 + jnp.exp(-x))


def _silu(x):
    return x * _sigmoid(x)


def _mod_kernel(c_ref, w_ref, b_ref, o_ref):
    sc = _silu(c_ref[...])
    o_ref[0] = jnp.dot(sc, w_ref[...], preferred_element_type=F32,
                       precision=lax.Precision.HIGHEST) + b_ref[0]


def _modulation(cc, w_mod, b_mod):
    rows, d = cc.shape
    return pl.pallas_call(
        _mod_kernel,
        grid=(N_MOD,),
        in_specs=[pl.BlockSpec((rows, d), lambda j: (0, 0)),
                  pl.BlockSpec((d, d), lambda j: (0, j)),
                  pl.BlockSpec((1, 1, d), lambda j: (j, 0, 0))],
        out_specs=pl.BlockSpec((1, rows, d), lambda j: (j, 0, 0)),
        out_shape=jax.ShapeDtypeStruct((N_MOD, rows, d), F32),
        compiler_params=pltpu.CompilerParams(vmem_limit_bytes=VMEM_LIMIT),
        name="modulation",
    )(cc, w_mod, b_mod.reshape(N_MOD, 1, d))


def _rms_mod(x, gain, scale, shift):
    ms = jnp.mean(x * x, axis=-1, keepdims=True)
    return (x * lax.rsqrt(ms + NORM_EPS) * gain) * (1.0 + scale) + shift


def _inproj_kernel(x_ref, mod_ref, n1_ref, wdn_ref, wml_ref, wg_ref, pdn_ref, pml_ref, gt_ref,
                   h_ref, hcm_ref, *, n_tile, col_major):
    h = _rms_mod(x_ref[0], n1_ref[...], mod_ref[0, 1:2, :], mod_ref[0, 0:1, :])
    h_ref[...] = h.astype(BF16)
    for j in range(wdn_ref.shape[1] // n_tile):
        cols = slice(j * n_tile, (j + 1) * n_tile)
        pdn_ref[0, :, cols] = jnp.dot(h_ref[...], wdn_ref[:, cols], preferred_element_type=F32)
    if col_major:
        tm, d = h.shape
        hcm_ref[...] = jnp.swapaxes(h.reshape(tm // GRID_W, GRID_W, d), 0, 1).reshape(tm, d).astype(BF16)
    lhs_ref = hcm_ref if col_major else h_ref
    for j in range(wml_ref.shape[1] // n_tile):
        cols = slice(j * n_tile, (j + 1) * n_tile)
        res = jnp.dot(lhs_ref[...], wml_ref[:, cols], preferred_element_type=F32)
        if col_major:
            pml_ref[0, :, :, cols] = res.reshape(pml_ref.shape[1], pml_ref.shape[2], n_tile)
        else:
            pml_ref[0, :, cols] = res
    gt_ref[0] = lax.dot_general(wg_ref[...], h_ref[...], NT_DIMS, preferred_element_type=F32)


def _in_projection(x, mod, norm1, w_dn, w_ml, w_gate_t, tm, col_major):
    b, t, d = x.shape
    n_dn, n_ml = w_dn.shape[1], w_ml.shape[1]
    n_gate = w_gate_t.shape[0]
    kern = functools.partial(_inproj_kernel, n_tile=512, col_major=col_major)
    if col_major:
        assert tm % GRID_W == 0 and (tm // GRID_W) % 8 == 0 and t % tm == 0
        rows = t // GRID_W
        ml_spec = pl.BlockSpec((1, GRID_W, tm // GRID_W, n_ml), lambda i, j: (i, 0, j, 0))
        ml_shape = jax.ShapeDtypeStruct((b, GRID_W, rows, n_ml), F32)
    else:
        ml_spec = pl.BlockSpec((1, tm, n_ml), lambda i, j: (i, j, 0))
        ml_shape = jax.ShapeDtypeStruct((b, t, n_ml), F32)
    const = lambda shape: pl.BlockSpec(shape, lambda i, j: (0,) * len(shape), pipeline_mode=pl.Buffered(1))
    return pl.pallas_call(
        kern,
        grid=(b, t // tm),
        in_specs=[pl.BlockSpec((1, tm, d), lambda i, j: (i, j, 0)),
                  pl.BlockSpec((1, N_MOD, d), lambda i, j: (i, 0, 0)),
                  const((1, d)), const((d, n_dn)), const((d, n_ml)), const((n_gate, d))],
        out_specs=[pl.BlockSpec((1, tm, n_dn), lambda i, j: (i, j, 0)),
                   ml_spec,
                   pl.BlockSpec((1, n_gate, tm), lambda i, j: (i, 0, j))],
        out_shape=[jax.ShapeDtypeStruct((b, t, n_dn), F32),
                   ml_shape,
                   jax.ShapeDtypeStruct((b, n_gate, t), F32)],
        scratch_shapes=[pltpu.VMEM((tm, d), BF16), pltpu.VMEM((tm, d), BF16)],
        compiler_params=pltpu.CompilerParams(
            dimension_semantics=("arbitrary", "arbitrary"), vmem_limit_bytes=VMEM_LIMIT),
        name="in_projection",
    )(x, mod, norm1.reshape(1, d), w_dn, w_ml, w_gate_t)


def _iota2(n):
    return (lax.broadcasted_iota(jnp.int32, (n, n), 0), lax.broadcasted_iota(jnp.int32, (n, n), 1))


def _to_columns(x):
    ii, jj = _iota2(x.shape[1])
    eye = (ii == jj).astype(BF16)
    hi, mid, lo = _split3(x)
    f = lambda t: lax.dot_general(eye, t, NT_DIMS, preferred_element_type=F32)
    return f(hi) + f(mid) + f(lo)


def _masks(fwd, n):
    ii, jj = _iota2(n)
    if fwd:
        return ii >= jj, ii > jj
    return ii <= jj, ii < jj


def _row_parity(shape):
    return lax.broadcasted_iota(jnp.int32, shape, len(shape) - 2) % 2


def _cumulate_gate_rows(logdecay):
    n_chunks, _, n = logdecay.shape
    ii, jj = _iota2(n)
    prefix = (ii <= jj).astype(BF16)
    suffix = (ii >= jj).astype(BF16)
    ones = jnp.ones((n, n), BF16)
    flat = logdecay.reshape(n_chunks * GATE_ROWS, n)
    par = _row_parity(flat.shape)
    cum = jnp.where(par == 0, _dot_exact_rhs(flat, prefix), _dot_exact_rhs(flat, suffix))
    tot = _dot_exact_rhs(flat, ones)
    row = lax.broadcasted_iota(jnp.int32, flat.shape, 0) % GATE_ROWS
    return cum, tot, row


def _conv_block(u_ref, w, t0, rows, total):
    main = u_ref[0, pl.ds(t0, rows), :]
    lo = jnp.maximum(t0 - 8, 0)
    hi = jnp.minimum(t0 + rows, total - 8)
    prev = jnp.where(t0 > 0, u_ref[0, pl.ds(pl.multiple_of(lo, 8), 8), :], 0.0)
    nxt = jnp.where(t0 + rows < total, u_ref[0, pl.ds(pl.multiple_of(hi, 8), 8), :], 0.0)
    ext = jnp.concatenate([prev, main, nxt], axis=0)
    acc = None
    for j in range(CONV_K):
        off = 8 + j - CONV_K // 2
        term = ext[off:off + rows, :] * w[j:j + 1, :]
        acc = term if acc is None else acc + term
    return _silu(acc)


def _conv_block_inner(u_ref, w, t0, rows):
    acc = None
    for j in range(CONV_K):
        term = u_ref[0, pl.ds(t0 + (j - CONV_K // 2), rows), :] * w[j:j + 1, :]
        acc = term if acc is None else acc + term
    return _silu(acc)


def _l2norm(x):
    return x * lax.rsqrt(jnp.sum(x * x, axis=-1, keepdims=True) + NORM_EPS)


def _unit_tri_inverses(a_list, between_levels=()):
    n = a_list[0].shape[0]
    ii, jj = _iota2(n)
    eye = (ii == jj).astype(F32)
    pair = (ii >> 1) == (jj >> 1)
    ts = [eye - jnp.where(pair, a, 0.0) for a in a_list]
    for level in range(1, n.bit_length() - 1):
        same_big = (ii >> (level + 1)) == (jj >> (level + 1))
        same_small = (ii >> level) == (jj >> level)
        couple = same_big & jnp.logical_not(same_small)
        es = [jnp.where(couple, a, 0.0).astype(BF16) for a in a_list]
        tbs = [t.astype(BF16) for t in ts]
        tes = [jnp.dot(tb, e, preferred_element_type=F32) for tb, e in zip(tbs, es)]
        ts = [t - jnp.dot(te.astype(BF16), tb, preferred_element_type=F32)
              for t, te, tb in zip(ts, tes, tbs)]
        if level <= len(between_levels):
            between_levels[level - 1]()
    return ts


def _dn_intra_chunks(qkvx, with_out, between_levels=()):
    n = qkvx[0][0].shape[0]
    cols = [_to_columns(x) for _, _, _, x in qkvx]
    kks = [_dot_nt(k, k) for _, k, _, _ in qkvx]
    qks = [_dot_nt(q, k) if with_out else None for q, k, _, _ in qkvx]
    parts = []
    for (q, k, v, x), col, kk in zip(qkvx, cols, kks):
        for d in range(2):
            beta_c = col[:, d:d + 1]
            g_c = col[:, 2 + d:3 + d]
            tot_c = col[:, 4 + d:5 + d]
            g_r = x[2 + d:3 + d, :]
            incl, strict = _masks(d == 0, n)
            decay = jnp.exp(jnp.where(incl, g_c - g_r, NEG_BIG))
            e_g = jnp.exp(g_c)
            a = jnp.where(strict, kk * beta_c * decay, 0.0)
            rhs = jnp.concatenate([k * (beta_c * e_g), v * beta_c], axis=1).astype(BF16)
            parts.append((a, rhs, decay, e_g, tot_c - g_c))
    ts = _unit_tri_inverses([p[0] for p in parts], between_levels)
    wus = [jnp.dot(t.astype(BF16), p[1], preferred_element_type=F32).astype(BF16) for t, p in zip(ts, parts)]
    k_tail_ts = [(qkvx[i // 2][1] * jnp.exp(p[4])).T.astype(BF16) for i, p in enumerate(parts)]
    state_terms = [jnp.dot(kt, wu, preferred_element_type=F32) for kt, wu in zip(k_tail_ts, wus)]
    if with_out:
        scores = [(qks[i // 2] * p[2]).astype(BF16) for i, p in enumerate(parts)]
        out_terms = [jnp.dot(sc, wu, preferred_element_type=F32) for sc, wu in zip(scores, wus)]
    res = []
    for ci, (q, k, v, x) in enumerate(qkvx):
        per_dir = []
        for d in range(2):
            i = 2 * ci + d
            st = state_terms[i]
            if with_out:
                ot = out_terms[i]
                per_dir.append((-st[:, :HEAD_DIM], st[:, HEAD_DIM:],
                                q * parts[i][3] - ot[:, :HEAD_DIM], ot[:, HEAD_DIM:]))
            else:
                per_dir.append((-st[:, :HEAD_DIM], st[:, HEAD_DIM:], None, None))
        res.append(per_dir)
    return res


def _dn_kernel(sc_ref, ql_ref, kl_ref, vl_ref, gate_ref, qc_ref, kc_ref, vc_ref,
               gl_ref, gc_ref, wq_ref, wk_ref, wv_ref, nw_ref, y_ref,
               qs, ks, vs, rows_s, lhs_s, add_s, obuf, s_ref):
    head = pl.program_id(1)
    n = DN_CHUNK
    t_lat = ql_ref.shape[1]
    t_ctx = qc_ref.shape[1]
    nc_lat = t_lat // n
    nc_ctx = t_ctx // n
    half = nc_lat // 2
    conv_rows = 256
    ctx_group = 2 if nc_ctx % 2 == 0 else 1
    lat_group = 8 if nc_lat % 16 == 0 else 2

    a_log_f, a_log_b = sc_ref[head, 0], sc_ref[head, 1]
    dtb_f, dtb_b = sc_ref[head, 2], sc_ref[head, 3]

    def chunk_rows(c):
        return pl.ds(pl.multiple_of(c * n, n), n)

    def prep(src_refs, total):
        def block(t0, conv):
            q = _l2norm(conv(src_refs[0], wq_ref[...], t0))
            qs[pl.ds(t0, conv_rows), :] = q * (HEAD_DIM ** -0.5)
            ks[pl.ds(t0, conv_rows), :] = _l2norm(conv(src_refs[1], wk_ref[...], t0))
            vs[pl.ds(t0, conv_rows), :] = conv(src_refs[2], wv_ref[...], t0)

        edge = lambda u_ref, w, t0: _conv_block(u_ref, w, t0, conv_rows, total)
        inner = lambda u_ref, w, t0: _conv_block_inner(u_ref, w, t0, conv_rows)
        n_blocks = total // conv_rows
        block(0, edge)
        if n_blocks > 1:
            block((n_blocks - 1) * conv_rows, edge)

        def body(i, carry):
            block(pl.multiple_of(i * conv_rows, conv_rows), inner)
            return carry
        lax.fori_loop(1, n_blocks - 1, body, 0)

    def gate_rows(g_ref, n_chunks):
        beta_raw = g_ref[0, 0, 0]
        alpha_raw = g_ref[0, 0, 1]
        par = _row_parity(alpha_raw.shape)
        a_vec = jnp.exp(jnp.where(par == 0, a_log_f, a_log_b))
        dtb = jnp.where(par == 0, dtb_f, dtb_b)
        cum, tot, row = _cumulate_gate_rows(-a_vec * _softplus(alpha_raw + dtb))
        beta = _sigmoid(beta_raw).reshape(cum.shape)
        packed = jnp.where(row < 2, beta, jnp.where(row < 4, cum, tot))
        rows_s[0:n_chunks] = packed.reshape(n_chunks, GATE_ROWS, n)

    def intra_group(cs, with_out, between_levels=()):
        res = _dn_intra_chunks([(qs[chunk_rows(c), :], ks[chunk_rows(c), :], vs[chunk_rows(c), :], rows_s[c])
                                for c in cs], with_out, between_levels)
        for c, per_dir in zip(cs, res):
            for d in range(2):
                s_mul, s_add, o_mul, o_add = per_dir[d]
                lhs_s[d, c, 0:HEAD_DIM, :] = s_mul.astype(BF16)
                add_s[d, c, 0:HEAD_DIM, :] = s_add
                if with_out:
                    lhs_s[d, c, HEAD_DIM:, :] = o_mul.astype(BF16)
                    add_s[d, c, HEAD_DIM:, :] = o_add

    def state_steps(cf, cb, with_out):
        dc = ((0, cf), (1, cb))
        rows = slice(None) if with_out else slice(0, HEAD_DIM)
        ss = [s_ref[d] for d, _ in dc]
        rs = [jnp.dot(lhs_s[d, c, rows, :], s.astype(BF16), preferred_element_type=F32) + add_s[d, c, rows, :]
              for (d, c), s in zip(dc, ss)]
        for (d, c), s, r in zip(dc, ss, rs):
            s_ref[d] = s * jnp.exp(rows_s[c][4 + d:5 + d, :]) + r[:HEAD_DIM]
        return [r[HEAD_DIM:] if with_out else None for r in rs]

    def finalize(o, r):
        ms = jnp.mean(o * o, axis=-1, keepdims=True)
        return o * lax.rsqrt(ms + NORM_EPS) * nw_ref[...] * _silu(gate_ref[0, r, :])

    s_ref[...] = jnp.zeros_like(s_ref)
    prep((qc_ref, kc_ref, vc_ref), t_ctx)
    gate_rows(gc_ref, nc_ctx)
    for g in range(nc_ctx // ctx_group):
        intra_group([g * ctx_group + j for j in range(ctx_group)], False)

    def ctx_body(i, carry):
        state_steps(i, nc_ctx - 1 - i, False)
        return carry
    lax.fori_loop(0, nc_ctx, ctx_body, 0)

    prep((ql_ref, kl_ref, vl_ref), t_lat)
    gate_rows(gl_ref, nc_lat)
    def first_visit(i):
        cb = nc_lat - 1 - i
        o_f, o_b = state_steps(i, cb, True)
        obuf[chunk_rows(i), :] = o_f
        obuf[chunk_rows(cb), :] = o_b

    side = lat_group // 2
    n_groups = half // side

    def group_chunks(g):
        return [g * side + j for j in range(side)] + [nc_lat - 1 - g * side - j for j in range(side)]

    def group_steps(g):
        return [functools.partial(first_visit, g * side + j) for j in range(side)]

    intra_group(group_chunks(0), True)

    def group_body(g, carry):
        intra_group(group_chunks(g), True, group_steps(g - 1))
        return carry
    lax.fori_loop(1, n_groups, group_body, 0)
    for step in group_steps(n_groups - 1):
        step()

    def second_visit(i):
        cb = nc_lat - 1 - i
        o_f, o_b = state_steps(i, cb, True)
        return o_f + obuf[chunk_rows(i), :], o_b + obuf[chunk_rows(cb), :]

    def finish(i, sums):
        for c, o in zip((i, nc_lat - 1 - i), sums):
            r = chunk_rows(c)
            y_ref[0, r, :] = finalize(o, r).astype(y_ref.dtype)

    def second_body(i, sums):
        new_sums = second_visit(i)
        finish(i - 1, sums)
        return new_sums

    finish(nc_lat - 1, lax.fori_loop(half + 1, nc_lat, second_body, second_visit(half)))


def _deltanet(p_lat, p_ctx, g_lat, g_ctx, scalars, dn_conv, dn_norm):
    b, t_lat, _ = p_lat.shape
    t_ctx = p_ctx.shape[1]
    n = DN_CHUNK
    nc_lat, nc_ctx = t_lat // n, t_ctx // n
    assert n == HEAD_DIM and nc_lat % 16 == 0 and t_lat % 256 == 0 and t_ctx % 256 == 0 and t_ctx <= t_lat
    h = N_HEADS
    col = lambda off: (lambda i, j: (i, 0, off + j))
    lat_spec = lambda off: pl.BlockSpec((1, t_lat, HEAD_DIM), col(off))
    ctx_spec = lambda off: pl.BlockSpec((1, t_ctx, HEAD_DIM), col(off))
    conv_spec = lambda off: pl.BlockSpec((CONV_K, HEAD_DIM), lambda i, j: (0, off + j))
    gate_spec = lambda nc: pl.BlockSpec((1, 1, 2, nc, GATE_ROWS, n), lambda i, j: (i, j, 0, 0, 0, 0))
    seq = lambda dt: pltpu.VMEM((t_lat, HEAD_DIM), dt)
    step_terms = lambda dt: pltpu.VMEM((2, nc_lat, HEAD_DIM + n, HEAD_DIM), dt)
    return pl.pallas_call(
        _dn_kernel,
        grid=(b, h),
        in_specs=[pl.BlockSpec(memory_space=pltpu.SMEM),
                  lat_spec(0), lat_spec(h), lat_spec(2 * h), lat_spec(3 * h),
                  ctx_spec(0), ctx_spec(h), ctx_spec(2 * h),
                  gate_spec(nc_lat), gate_spec(nc_ctx),
                  conv_spec(0), conv_spec(h), conv_spec(2 * h),
                  pl.BlockSpec((1, HEAD_DIM), lambda i, j: (0, 0))],
        out_specs=pl.BlockSpec((1, t_lat, HEAD_DIM), lambda i, j: (i, 0, j)),
        out_shape=jax.ShapeDtypeStruct((b, t_lat, h * HEAD_DIM), BF16),
        scratch_shapes=[seq(F32), seq(F32), seq(F32),
                        pltpu.VMEM((nc_lat, GATE_ROWS, n), F32),
                        step_terms(BF16), step_terms(F32),
                        seq(F32),
                        pltpu.VMEM((2, HEAD_DIM, HEAD_DIM), F32)],
        compiler_params=pltpu.CompilerParams(
            dimension_semantics=("arbitrary", "arbitrary"), vmem_limit_bytes=VMEM_LIMIT),
        name="deltanet_scan",
    )(scalars, p_lat, p_lat, p_lat, p_lat, p_ctx, p_ctx, p_ctx, g_lat, g_ctx,
      dn_conv, dn_conv, dn_conv, dn_norm.reshape(1, HEAD_DIM))


def _ml_group(problems, states, with_out):
    n = problems[0][1].shape[0]
    ones_blk = jnp.ones((n, HEAD_DIM), BF16)

    def lane_spread(rows):
        ones2 = jnp.ones((2, HEAD_DIM), BF16)
        return [lax.dot_general(jnp.concatenate(_split2(row), axis=0), ones2, TN_DIMS,
                                preferred_element_type=F32) for row in rows]

    a_rows = [x[d:d + 1, :] for d, _, _, _, x in problems]
    a_spreads = lane_spread(a_rows)
    b_spreads = lane_spread([x[2 + d:3 + d, :] for d, _, _, _, x in problems]) if with_out else None
    ms = [m for _, m in states]
    chain = []
    for (d, _, _, _, x), a_r in zip(problems, a_rows):
        mx = jnp.maximum(ms[d], jnp.max(a_r, axis=1, keepdims=True))
        chain.append((ms[d], mx))
        ms[d] = x[4 + d:5 + d, 0:1] + mx
    pre = []
    for (d, q, k, v, x), a_s, (_, mx) in zip(problems, a_spreads, chain):
        v_ext = jnp.concatenate([v.astype(BF16), ones_blk], axis=1)
        pre.append((v_ext, (k * jnp.exp(a_s - mx)).astype(BF16)))
    ups = [lax.dot_general(kw, v_ext, TN_DIMS, preferred_element_type=F32) for v_ext, kw in pre]
    if with_out:
        cm_rows = [jnp.max(jnp.where(_masks(d != 0, n)[0], a_s[:, :n], NEG_BIG), axis=0, keepdims=True)
                   for (d, _, _, _, _), a_s in zip(problems, a_spreads)]
        cm_spreads = lane_spread(cm_rows)
        qks = [_dot_nt(q, k) for _, q, k, _, _ in problems]
        scores = []
        for (d, _, _, _, _), a_r, cm_s, qk in zip(problems, a_rows, cm_spreads, qks):
            incl, _ = _masks(d == 0, n)
            expo = jnp.where(incl, jnp.broadcast_to(a_r, (n, n)) - cm_s[:, :n], NEG_BIG)
            scores.append((qk * jnp.exp(expo)).astype(BF16))
        intra = [jnp.dot(s, v_ext, preferred_element_type=F32) for s, (v_ext, _) in zip(scores, pre)]
    cs = [c_ext for c_ext, _ in states]
    starts = []
    for (d, _, _, _, _), (m, mx), up in zip(problems, chain, ups):
        starts.append((cs[d], m))
        cs[d] = jnp.exp(m - mx) * cs[d] + up
    states = list(zip(cs, ms))
    if not with_out:
        return states, [None] * len(problems)
    inter = [_dot(q, c_ext) for (_, q, _, _, _), (c_ext, _) in zip(problems, starts)]
    hs = []
    for (_, m), cm_s, b_s, qc, sv in zip(starts, cm_spreads, b_spreads, inter, intra):
        mm = jnp.maximum(m, cm_s)
        w_inter = jnp.exp(m - mm)
        w_intra = jnp.exp(cm_s - mm)
        num = w_inter * qc[:, :HEAD_DIM] + w_intra * sv[:, :HEAD_DIM]
        den = w_inter * qc[:, HEAD_DIM:] + w_intra * sv[:, HEAD_DIM:]
        hs.append(num / jnp.maximum(jnp.abs(den), jnp.exp(-(b_s + mm))))
    return states, hs


def _ml_kernel(sc_ref, ql_ref, kl_ref, vl_ref, og_ref, qc_ref, kc_ref, vc_ref,
               gl_ref, gc_ref, nw_ref, y_ref, rl, rc, obuf, c_ref, m_ref):
    head = pl.program_id(1)
    t_lat = ql_ref.shape[1]
    t_ctx = qc_ref.shape[1]
    nc_lat = t_lat // ML_CHUNK
    nc_ctx = t_ctx // ML_CHUNK
    half = nc_lat // 2
    k_scale = HEAD_DIM ** -0.5

    igb_f, igb_b = sc_ref[head, 0], sc_ref[head, 1]
    fgb_f, fgb_b = sc_ref[head, 2], sc_ref[head, 3]

    def gate_rows(g_ref, dst, n_chunks):
        ig_raw = g_ref[0, 0, 0]
        fg_raw = g_ref[0, 0, 1]
        par = _row_parity(fg_raw.shape)
        lf = -_softplus(-(fg_raw + jnp.where(par == 0, fgb_f, fgb_b)))
        cum, tot, row = _cumulate_gate_rows(lf)
        ic = (ig_raw + jnp.where(par == 0, igb_f, igb_b)).reshape(n_chunks * GATE_ROWS, ML_CHUNK)
        packed = jnp.where(row < 2, ic - cum, jnp.where(row < 4, cum, tot))
        dst[...] = packed.reshape(n_chunks, GATE_ROWS, ML_CHUNK)

    gate_rows(gl_ref, rl, nc_lat)
    gate_rows(gc_ref, rc, nc_ctx)

    c_ref[...] = jnp.zeros_like(c_ref)
    m_ref[...] = jnp.zeros_like(m_ref)

    def run_steps(first_step, n_steps, total, load, gates, with_out):
        problems = []
        for j in range(n_steps):
            for d, c in ((0, first_step + j), (1, total - 1 - first_step - j)):
                problems.append((d, load(0, c), load(1, c) * k_scale, load(2, c), gates[c]))
        states = [(c_ref[d], m_ref[d, 0:1, 0:1]) for d in range(2)]
        states, hs = _ml_group(problems, states, with_out)
        for d, (c_ext, m) in enumerate(states):
            c_ref[d] = c_ext
            m_ref[d] = jnp.broadcast_to(m, m_ref.shape[1:])
        return hs

    ctx_refs = (qc_ref, kc_ref, vc_ref)
    lat_refs = (ql_ref, kl_ref, vl_ref)

    def chunk_rows(c):
        return pl.ds(pl.multiple_of(c * ML_CHUNK, ML_CHUNK), ML_CHUNK)

    def ctx_load(which, c):
        return ctx_refs[which][0, chunk_rows(c), :]

    def lat_load(which, c):
        return lat_refs[which][0, chunk_rows(c), :]

    ctx_unroll = ML_UNROLL if nc_ctx % ML_UNROLL == 0 else 1

    def ctx_body(i, carry):
        run_steps(i * ctx_unroll, ctx_unroll, nc_ctx, ctx_load, rc, False)
        return carry

    lax.fori_loop(0, nc_ctx // ctx_unroll, ctx_body, 0)

    def finalize(hh, r):
        ms = jnp.mean(hh * hh, axis=-1, keepdims=True)
        y = hh * lax.rsqrt(ms + NORM_EPS) * nw_ref[...]
        return y * _sigmoid(og_ref[0, r, :])

    def lat_body(i, second):
        first_step = i * ML_UNROLL
        hs = run_steps(first_step, ML_UNROLL, nc_lat, lat_load, rl, True)
        for j in range(ML_UNROLL):
            for d, c in ((0, first_step + j), (1, nc_lat - 1 - first_step - j)):
                hh = hs[2 * j + d]
                r = chunk_rows(c)
                if second:
                    y_ref[0, r, :] = finalize(hh + obuf[r, :], r).astype(y_ref.dtype)
                else:
                    obuf[r, :] = hh

    for i in range(half // ML_UNROLL):
        lat_body(i, False)
    for i in range(half // ML_UNROLL, nc_lat // ML_UNROLL):
        lat_body(i, True)


def _mlstm(p_lat, p_ctx, g_lat, g_ctx, scalars, ml_norm):
    b, t_lat, _ = p_lat.shape
    t_ctx = p_ctx.shape[1]
    nc_lat, nc_ctx = t_lat // ML_CHUNK, t_ctx // ML_CHUNK
    assert nc_lat % (2 * ML_UNROLL) == 0 and ML_CHUNK % (t_lat // GRID_W) == 0
    h = N_HEADS
    lat_spec = lambda off: pl.BlockSpec((1, t_lat, HEAD_DIM), lambda i, j: (i, 0, off + j))
    ctx_spec = lambda off: pl.BlockSpec((1, t_ctx, HEAD_DIM), lambda i, j: (i, 0, off + j))
    gate_spec = lambda nc: pl.BlockSpec((1, 1, 2, nc, GATE_ROWS, ML_CHUNK), lambda i, j: (i, j, 0, 0, 0, 0))
    return pl.pallas_call(
        _ml_kernel,
        grid=(b, h),
        in_specs=[pl.BlockSpec(memory_space=pltpu.SMEM),
                  lat_spec(0), lat_spec(h), lat_spec(2 * h), lat_spec(3 * h),
                  ctx_spec(0), ctx_spec(h), ctx_spec(2 * h),
                  gate_spec(nc_lat), gate_spec(nc_ctx),
                  pl.BlockSpec((1, HEAD_DIM), lambda i, j: (0, j))],
        out_specs=pl.BlockSpec((1, t_lat, HEAD_DIM), lambda i, j: (i, 0, j)),
        out_shape=jax.ShapeDtypeStruct((b, t_lat, h * HEAD_DIM), F32),
        scratch_shapes=[pltpu.VMEM((nc_lat, GATE_ROWS, ML_CHUNK), F32), pltpu.VMEM((nc_ctx, GATE_ROWS, ML_CHUNK), F32),
                        pltpu.VMEM((t_lat, HEAD_DIM), F32), pltpu.VMEM((2, HEAD_DIM, 2 * HEAD_DIM), F32),
                        pltpu.VMEM((2, 8, HEAD_DIM), F32)],
        compiler_params=pltpu.CompilerParams(
            dimension_semantics=("arbitrary", "arbitrary"), vmem_limit_bytes=VMEM_LIMIT),
        name="mlstm_scan",
    )(scalars, p_lat, p_lat, p_lat, p_lat, p_ctx, p_ctx, p_ctx, g_lat, g_ctx,
      ml_norm.reshape(1, h * HEAD_DIM))


def _ffn_kernel(x_ref, ydn_ref, yml_ref, mod_ref, n2_ref, fn_ref, wo_dn_ref, wo_ml_ref,
                wg_ref, wu_ref, wd_ref, o_ref, h_ref, act_ref, *, f_tile):
    yml = jnp.swapaxes(yml_ref[0], 0, 1).reshape(x_ref.shape[1], yml_ref.shape[3]).astype(BF16)
    mix = (jnp.dot(ydn_ref[0], wo_dn_ref[...], preferred_element_type=F32)
           + jnp.dot(yml, wo_ml_ref[...], preferred_element_type=F32))
    x1 = x_ref[0] + mod_ref[0, 2:3, :] * mix
    h_ref[...] = _rms_mod(x1, n2_ref[...], mod_ref[0, 4:5, :], mod_ref[0, 3:4, :]).astype(BF16)
    d_ff = wg_ref.shape[1]
    for j in range(d_ff // f_tile):
        sl = slice(j * f_tile, (j + 1) * f_tile)
        gate = jnp.dot(h_ref[...], wg_ref[:, sl], preferred_element_type=F32)
        up = jnp.dot(h_ref[...], wu_ref[:, sl], preferred_element_type=F32)
        act_ref[:, sl] = (_silu(gate) * up).astype(BF16)
    x2 = x1 + mod_ref[0, 5:6, :] * jnp.dot(act_ref[...], wd_ref[...], preferred_element_type=F32)
    ms = jnp.mean(x2 * x2, axis=-1, keepdims=True)
    o_ref[0] = x2 * lax.rsqrt(ms + NORM_EPS) * fn_ref[...]


def _out_ffn(x, y_dn, y_ml, mod, norm2, final_norm, wo_dn, wo_ml, w_gate, w_up, w_down, tm):
    b, t, d = x.shape
    d_mix = y_dn.shape[2]
    d_ff = w_gate.shape[1]
    assert tm % GRID_W == 0 and (tm // GRID_W) % 8 == 0
    const = lambda shape: pl.BlockSpec(shape, lambda i, j: (0,) * len(shape),
                                       pipeline_mode=pl.Buffered(1))
    kern = functools.partial(_ffn_kernel, f_tile=256)
    return pl.pallas_call(
        kern,
        grid=(b, t // tm),
        in_specs=[pl.BlockSpec((1, tm, d), lambda i, j: (i, j, 0)),
                  pl.BlockSpec((1, tm, d_mix), lambda i, j: (i, j, 0)),
                  pl.BlockSpec((1, GRID_W, tm // GRID_W, d_mix), lambda i, j: (i, 0, j, 0)),
                  pl.BlockSpec((1, N_MOD, d), lambda i, j: (i, 0, 0)),
                  const((1, d)), const((1, d)),
                  const((d_mix, d)), const((d_mix, d)),
                  const((d, d_ff)), const((d, d_ff)), const((d_ff, d))],
        out_specs=pl.BlockSpec((1, tm, d), lambda i, j: (i, j, 0)),
        out_shape=jax.ShapeDtypeStruct((b, t, d), F32),
        scratch_shapes=[pltpu.VMEM((tm, d), BF16), pltpu.VMEM((tm, d_ff), BF16)],
        compiler_params=pltpu.CompilerParams(
            dimension_semantics=("arbitrary", "arbitrary"), vmem_limit_bytes=VMEM_LIMIT),
        name="out_ffn",
    )(x, y_dn, y_ml, mod, norm2.reshape(1, d), final_norm.reshape(1, d),
      wo_dn, wo_ml, w_gate, w_up, w_down)


def _gate_weight_rows(w_in, d_group):
    h = N_HEADS
    cols = []
    for mixer in range(2):
        base = mixer * (4 * d_group + 4 * h) + 4 * d_group
        for head in range(h):
            for slab in range(2):
                cols += [base + slab * 2 * h + head, base + slab * 2 * h + h + head]
    return w_in[:, jnp.array(cols)].T


def _chunk_major_gates(gt, chunk, col_major):
    b, _, t = gt.shape
    if col_major:
        rows = t // GRID_W
        cols = chunk // rows
        g = gt.reshape(b, N_HEADS, 2, 2, rows, GRID_W // cols, cols)
        g = g.transpose(0, 1, 2, 5, 3, 6, 4).reshape(b, N_HEADS, 2, t // chunk, 2, chunk)
    else:
        g = gt.reshape(b, N_HEADS, 2, 2, t // chunk, chunk).transpose(0, 1, 2, 4, 3, 5)
    return jnp.tile(g, (1, 1, 1, 1, GATE_ROWS // 2, 1))


def kernel(x, c, ctx, c_ctx, w_mod, b_mod, norm1, w_in, dn_conv, dn_a_log, dn_dt_bias, dn_norm,
           ml_ig_bias, ml_fg_bias, ml_norm, w_out, norm2, w_ffn_in, w_ffn_out, final_norm):
    depth = w_mod.shape[0]
    assert depth == 1, "context outputs are only skipped for a single layer"
    b, t_lat, d = x.shape
    h = N_HEADS
    d_group = h * HEAD_DIM
    d_ff = w_ffn_out.shape[1]
    layer = 0

    pad_rows = -(b + 1) % 8
    cc = jnp.concatenate([c, c_ctx[None, :], jnp.zeros((pad_rows, d), F32)], axis=0)
    mod = _modulation(cc, w_mod[layer], b_mod[layer])
    mod_lat = mod[:, :b].transpose(1, 0, 2)
    mod_ctx = jnp.broadcast_to(mod[:, b][None], (b, N_MOD, d))

    w = w_in[layer]
    dn_cols = 4 * d_group + 4 * h
    w_dn = w[:, :4 * d_group].astype(BF16)
    w_ml = w[:, dn_cols:dn_cols + 4 * d_group].astype(BF16)
    w_gate_t = _gate_weight_rows(w, d_group).astype(BF16)
    pdn_lat, pml_lat, gt_lat = _in_projection(x, mod_lat, norm1[layer], w_dn, w_ml, w_gate_t,
                                              tm=512, col_major=True)
    pdn_ctx, pml_ctx, gt_ctx = _in_projection(ctx, mod_ctx, norm1[layer], w_dn, w_ml, w_gate_t,
                                              tm=ctx.shape[1], col_major=False)
    pml_lat = pml_lat.reshape(b, t_lat, 4 * d_group)

    dn_gate_rows = h * 2 * 2
    g_dn_lat = _chunk_major_gates(gt_lat[:, :dn_gate_rows], DN_CHUNK, False)
    g_dn_ctx = _chunk_major_gates(gt_ctx[:, :dn_gate_rows], DN_CHUNK, False)
    g_ml_lat = _chunk_major_gates(gt_lat[:, dn_gate_rows:], ML_CHUNK, True)
    g_ml_ctx = _chunk_major_gates(gt_ctx[:, dn_gate_rows:], ML_CHUNK, False)

    dn_scal = jnp.concatenate([dn_a_log[layer].T, dn_dt_bias[layer].T], axis=1)
    ml_scal = jnp.concatenate([ml_ig_bias[layer].T, ml_fg_bias[layer].T], axis=1)

    y_dn = _deltanet(pdn_lat, pdn_ctx, g_dn_lat, g_dn_ctx, dn_scal, dn_conv[layer], dn_norm[layer])
    y_ml = _mlstm(pml_lat, pml_ctx, g_ml_lat, g_ml_ctx, ml_scal, ml_norm[layer])
    y_ml = y_ml.reshape(b, GRID_W, t_lat // GRID_W, d_group)

    wo = w_out[layer].astype(BF16)
    wf = w_ffn_in[layer].astype(BF16)
    return _out_ffn(x, y_dn, y_ml, mod_lat, norm2[layer], final_norm,
                    wo[:d_group], wo[d_group:], wf[:, :d_ff], wf[:, d_ff:],
                    w_ffn_out[layer].astype(BF16), tm=512)
```
